```python
import math
import jax, jax.numpy as jnp
from jax import lax
import numpy as np

D_MODEL = 1024
BATCH = 2
SEQ = 16384
DEPTH = 2

GRID_W = 64
CTX_LEN = 256
N_MOD = 9
MACARON_WEIGHT = 0.5
NORM_EPS = 1e-6
FFN_HIDDEN = 2816
GLA_HEADS = 4
GLA_DK = 64
GLA_DV = 128
GLA_QK = GLA_HEADS * GLA_DK
GLA_V = GLA_HEADS * GLA_DV
GLA_GATE_RANK = 16
GLA_TAU = 16.0
GLA_CHUNK = 64
S5_WIDTH = 512
S5_GROUP = 16
S5_GROUPS = S5_WIDTH // S5_GROUP
S5_STATE = 64
S5_DT_MIN = 1e-3
S5_DT_MAX = 1e-1
ATT_HEADS = 8
ATT_KV_HEADS = 2
ATT_GROUP = ATT_HEADS // ATT_KV_HEADS
ATT_HEAD_DIM = 64
ATT_Q = ATT_HEADS * ATT_HEAD_DIM
ATT_KV = ATT_KV_HEADS * ATT_HEAD_DIM
ATT_BLOCK = 128
ROPE_PAIRS = ATT_HEAD_DIM // 4
ROPE_THETA = 10000.0
BRANCH_WIDTH = 512
N_BRANCH = 3
IN_NAMES = ('gla_q', 'gla_k', 'gla_v', 'gla_gf', 'gla_gb', 'gla_r', 's5_u', 'att_q', 'att_k', 'att_v')
IN_WIDTHS = (GLA_QK, GLA_QK, GLA_V, GLA_GATE_RANK, GLA_GATE_RANK, GLA_V, S5_WIDTH, ATT_Q, ATT_KV, ATT_KV)
D_IN = sum(IN_WIDTHS)
F32 = jnp.float32

kernel_name = 'hybrid_gla_s5_gqa_macaron_prefix_dit'


def rms_norm(x, w):
    xf = x.astype(F32)
    y = xf * lax.rsqrt(jnp.mean(xf * xf, axis=-1, keepdims=True) + NORM_EPS)
    return (y * w.astype(F32)).astype(x.dtype)


def modulate(h, shift, scale):
    return h * (1 + scale) + shift


def swiglu(h, w_g, w_u, w_d):
    return (jax.nn.silu(h @ w_g) * (h @ w_u)) @ w_d


def ffn_sublayer(h, shift, scale, gate, g_norm, w_g, w_u, w_d):
    n = modulate(rms_norm(h, g_norm), shift, scale)
    return h + MACARON_WEIGHT * gate * swiglu(n, w_g, w_u, w_d)


def split_in(z):
    offsets = np.cumsum(IN_WIDTHS)[:-1].tolist()
    return dict(zip(IN_NAMES, jnp.split(z, offsets, axis=-1)))


def axial_rope_tables(n_tokens):
    rows = n_tokens // GRID_W
    row = jnp.repeat(jnp.arange(rows, dtype=F32), GRID_W)
    col = jnp.tile(jnp.arange(GRID_W, dtype=F32), rows)
    inv = ROPE_THETA ** (-jnp.arange(ROPE_PAIRS, dtype=F32) / ROPE_PAIRS)
    ang = jnp.stack([row[:, None] * inv, col[:, None] * inv], axis=1)
    return jnp.cos(ang), jnp.sin(ang)


def apply_rope(t, cos, sin):
    B, L, H, hd = t.shape
    tf = t.reshape(B, L, H, 2, 2, ROPE_PAIRS).astype(F32)
    t1, t2 = tf[..., 0, :], tf[..., 1, :]
    cs, sn = cos[None, :, None], sin[None, :, None]
    out = jnp.stack([t1 * cs - t2 * sn, t2 * cs + t1 * sn], axis=-2)
    return out.reshape(B, L, H, hd).astype(t.dtype)


def gla_inputs(z, gate_w, gate_b):
    B, L, _ = z['gla_q'].shape
    heads = lambda t, d: t.reshape(B, L, GLA_HEADS, d).astype(F32)
    q = heads(z['gla_q'], GLA_DK) * GLA_DK ** -0.5
    k = heads(z['gla_k'], GLA_DK)
    v = heads(z['gla_v'], GLA_DV)

    def log_decay(zg, w, b):
        return heads(jax.nn.log_sigmoid((zg @ w + b).astype(F32)), GLA_DK) / GLA_TAU

    return q, k, v, log_decay(z['gla_gf'], gate_w[0], gate_b[0]), log_decay(z['gla_gb'], gate_w[1], gate_b[1])


def gla_chunked(q, k, v, log_a, s0):
    B, L, H, dk = q.shape
    n = L // GLA_CHUNK
    rs = lambda t: t.reshape(B, n, GLA_CHUNK, H, t.shape[-1])
    q, k, v, g = rs(q), rs(k), rs(v), rs(log_a)
    b = jnp.cumsum(g, axis=2)
    b_last = b[:, :, -1:]
    q_in = q * jnp.exp(b)
    k_in = k * jnp.exp(-b)
    k_out = k * jnp.exp(b_last - b)
    mask = jnp.tril(jnp.ones((GLA_CHUNK, GLA_CHUNK), F32))
    att = jnp.einsum('bnihd,bnjhd->bnhij', q_in, k_in) * mask
    o_intra = jnp.einsum('bnhij,bnjhe->bnihe', att, v)
    d_state = jnp.einsum('bnjhd,bnjhe->bnhde', k_out, v)
    decay = jnp.exp(b_last[:, :, 0])

    def step(S, inp):
        dec, ds = inp
        return dec[..., None] * S + ds, S

    s_fin, s_prev = lax.scan(step, s0, (jnp.moveaxis(decay, 1, 0), jnp.moveaxis(d_state, 1, 0)))
    o_inter = jnp.einsum('bnihd,bnhde->bnihe', q_in, jnp.moveaxis(s_prev, 0, 1))
    return (o_intra + o_inter).reshape(B, L, H, -1), s_fin


def gla_bidir(q, k, v, g_f, g_b, s0_f, s0_b):
    flip = lambda t: t[:, ::-1]
    o_f, s_f = gla_chunked(q, k, v, g_f, s0_f)
    o_b, s_b = gla_chunked(flip(q), flip(k), flip(v), flip(g_b), s0_b)
    return o_f + flip(o_b), s_f, s_b


def gla_readout(o, r, norm_w):
    B, L = o.shape[:2]
    y = rms_norm(o, norm_w).reshape(B, L, GLA_V).astype(r.dtype)
    return y * jax.nn.silu(r)


def s5_discretise(a_re, a_im, log_dt, b_re, b_im):
    dt = jnp.exp(log_dt.astype(F32))[:, None]
    ar, ai = a_re.astype(F32), a_im.astype(F32)
    mag = jnp.exp(dt * ar)
    abar_re, abar_im = mag * jnp.cos(dt * ai), mag * jnp.sin(dt * ai)
    den = ar * ar + ai * ai
    xr, xi = abar_re - 1.0, abar_im
    coef_re = (xr * ar + xi * ai) / den
    coef_im = (xi * ar - xr * ai) / den
    br, bi = b_re.astype(F32), b_im.astype(F32)
    bbar_re = coef_re[..., None] * br - coef_im[..., None] * bi
    bbar_im = coef_re[..., None] * bi + coef_im[..., None] * br
    return abar_re, abar_im, bbar_re, bbar_im


def complex_scan_combine(e1, e2):
    a1r, a1i, b1r, b1i = e1
    a2r, a2i, b2r, b2i = e2
    return (a2r * a1r - a2i * a1i, a2r * a1i + a2i * a1r,
            a2r * b1r - a2i * b1i + b2r, a2r * b1i + a2i * b1r + b2i)


def s5_direction(u, disc, h0):
    abar_re, abar_im, bbar_re, bbar_im = disc
    h0_re, h0_im = h0
    bu_re = jnp.einsum('blgc,gpc->blgp', u, bbar_re)
    bu_im = jnp.einsum('blgc,gpc->blgp', u, bbar_im)
    bu_re = bu_re.at[:, 0].add(abar_re * h0_re - abar_im * h0_im)
    bu_im = bu_im.at[:, 0].add(abar_re * h0_im + abar_im * h0_re)
    shape = bu_re.shape
    _, _, h_re, h_im = lax.associative_scan(
        complex_scan_combine,
        (jnp.broadcast_to(abar_re, shape), jnp.broadcast_to(abar_im, shape), bu_re, bu_im), axis=1)
    return h_re, h_im


def s5_states(u_flat, disc_f, disc_b, h0_f, h0_b):
    B, L, _ = u_flat.shape
    u = u_flat.reshape(B, L, S5_GROUPS, S5_GROUP).astype(F32)
    hf_re, hf_im = s5_direction(u, disc_f, h0_f)
    hb_re, hb_im = s5_direction(u[:, ::-1], disc_b, h0_b)
    return (hf_re + hb_re[:, ::-1], hf_im + hb_im[:, ::-1],
            (hf_re[:, -1], hf_im[:, -1]), (hb_re[:, -1], hb_im[:, -1]))


def s5_readout(h_re, h_im, u_flat, c_re, c_im, d, glu_w, glu_b):
    B, L, _ = u_flat.shape
    u = u_flat.reshape(B, L, S5_GROUPS, S5_GROUP).astype(F32)
    y = (jnp.einsum('blgp,gcp->blgc', h_re, c_re.astype(F32))
         - jnp.einsum('blgp,gcp->blgc', h_im, c_im.astype(F32))
         + d.reshape(S5_GROUPS, S5_GROUP).astype(F32) * u)
    y = jax.nn.gelu(y.reshape(B, L, S5_WIDTH)).astype(u_flat.dtype)
    return y * jax.nn.sigmoid(y @ glu_w + glu_b)


def attn_inputs(z, q_norm_w, k_norm_w):
    B, L, _ = z['att_q'].shape
    q = rms_norm(z['att_q'].reshape(B, L, ATT_HEADS, ATT_HEAD_DIM), q_norm_w)
    k = rms_norm(z['att_k'].reshape(B, L, ATT_KV_HEADS, ATT_HEAD_DIM), k_norm_w)
    v = z['att_v'].reshape(B, L, ATT_KV_HEADS, ATT_HEAD_DIM)
    return q, k, v


def attend(q, k, v):
    B, Lq = q.shape[:2]
    nb = Lq // ATT_BLOCK
    qb = jnp.moveaxis(q.reshape(B, nb, ATT_BLOCK, ATT_KV_HEADS, ATT_GROUP, ATT_HEAD_DIM), 1, 0)
    scale = ATT_HEAD_DIM ** -0.5

    def one_block(qblk):
        s = jnp.einsum('bqkgd,bskd->bkgqs', qblk, k, preferred_element_type=F32) * scale
        p = jax.nn.softmax(s, axis=-1).astype(v.dtype)
        return jnp.einsum('bkgqs,bskd->bqkgd', p, v)

    o = lax.map(one_block, qb)
    return jnp.moveaxis(o, 0, 1).reshape(B, Lq, ATT_Q)


def merge_branches(n, y_gla, y_s5, y_att, p):
    g = jax.nn.sigmoid((n @ p['w_branch_gate'] + p['b_branch_gate']).astype(F32)).astype(n.dtype)
    g_gla, g_s5, g_att = jnp.split(g, N_BRANCH, axis=-1)
    wp = p['w_branch_proj']
    m = g_gla * (y_gla @ wp[0]) + g_s5 * (y_s5 @ wp[1]) + g_att * (y_att @ wp[2])
    return m @ p['w_out']


def token_mixer(n_ctx, n_lat, p, rope_cos, rope_sin, need_ctx_out):
    B = n_lat.shape[0]
    zc = split_in(n_ctx @ p['w_in'])
    zl = split_in(n_lat @ p['w_in'])
    qc, kc, vc, gfc, gbc = gla_inputs(zc, p['gla_gate_w'], p['gla_gate_b'])
    ql, kl, vl, gfl, gbl = gla_inputs(zl, p['gla_gate_w'], p['gla_gate_b'])
    s0 = jnp.zeros((B, GLA_HEADS, GLA_DK, GLA_DV), F32)
    o_gc, sf_c, sb_c = gla_bidir(qc, kc, vc, gfc, gbc, s0, s0)
    o_gl, _, _ = gla_bidir(ql, kl, vl, gfl, gbl, sf_c, sb_c)
    disc_f = s5_discretise(p['s5_a_re'][0], p['s5_a_im'][0], p['s5_log_dt'][0], p['s5_b_re'], p['s5_b_im'])
    disc_b = s5_discretise(p['s5_a_re'][1], p['s5_a_im'][1], p['s5_log_dt'][1], p['s5_b_re'], p['s5_b_im'])
    h0 = (jnp.zeros((B, S5_GROUPS, S5_STATE), F32), jnp.zeros((B, S5_GROUPS, S5_STATE), F32))
    hc_re, hc_im, hf_c, hb_c = s5_states(zc['s5_u'], disc_f, disc_b, h0, h0)
    hl_re, hl_im, _, _ = s5_states(zl['s5_u'], disc_f, disc_b, hf_c, hb_c)
    aqc, akc, avc = attn_inputs(zc, p['attn_q_norm_w'], p['attn_k_norm_w'])
    aql, akl, avl = attn_inputs(zl, p['attn_q_norm_w'], p['attn_k_norm_w'])
    aql = apply_rope(aql, rope_cos, rope_sin)
    akl = apply_rope(akl, rope_cos, rope_sin)
    y_att_l = attend(aql, jnp.concatenate([akc, akl], axis=1), jnp.concatenate([avc, avl], axis=1))
    y_gla_l = gla_readout(o_gl, zl['gla_r'], p['gla_norm_w'])
    y_s5_l = s5_readout(hl_re, hl_im, zl['s5_u'], p['s5_c_re'], p['s5_c_im'], p['s5_d'], p['s5_glu_w'], p['s5_glu_b'])
    out_l = merge_branches(n_lat, y_gla_l, y_s5_l, y_att_l, p)
    if not need_ctx_out:
        return None, out_l
    y_att_c = attend(aqc, akc, avc)
    y_gla_c = gla_readout(o_gc, zc['gla_r'], p['gla_norm_w'])
    y_s5_c = s5_readout(hc_re, hc_im, zc['s5_u'], p['s5_c_re'], p['s5_c_im'], p['s5_d'], p['s5_glu_w'], p['s5_glu_b'])
    out_c = merge_branches(n_ctx, y_gla_c, y_s5_c, y_att_c, p)
    return out_c, out_l


def setup_inputs(seed: int = 0) -> dict:
    key = jax.random.key(seed)
    ks = iter(jax.random.split(key, 40))
    D = D_MODEL

    def nrm(shape, scale):
        return jax.random.normal(next(ks), shape, F32) * scale

    a_im_base = jnp.pi * jnp.arange(S5_STATE, dtype=F32)
    return {
        'x': nrm((BATCH, SEQ, D), 1.0),
        'c': nrm((BATCH, D), 1.0),
        'ctx': nrm((BATCH, CTX_LEN, D), 1.0),
        'c_ctx': nrm((D,), 1.0),
        'w_ada': nrm((DEPTH, D, N_MOD * D), 0.5 * D ** -0.5),
        'b_ada': nrm((DEPTH, N_MOD * D), 0.02),
        'norm_w': 1.0 + nrm((DEPTH, 3, D), 0.02),
        'w_ffn_gate': nrm((DEPTH, 2, D, FFN_HIDDEN), D ** -0.5),
        'w_ffn_up': nrm((DEPTH, 2, D, FFN_HIDDEN), D ** -0.5),
        'w_ffn_down': nrm((DEPTH, 2, FFN_HIDDEN, D), FFN_HIDDEN ** -0.5),
        'w_in': nrm((DEPTH, D, D_IN), D ** -0.5),
        'gla_gate_w': nrm((DEPTH, 2, GLA_GATE_RANK, GLA_QK), GLA_GATE_RANK ** -0.5),
        'gla_gate_b': nrm((DEPTH, 2, GLA_QK), 0.1),
        'gla_norm_w': 1.0 + nrm((DEPTH, GLA_DV), 0.02),
        's5_a_re': -0.5 + nrm((DEPTH, 2, S5_GROUPS, S5_STATE), 0.01),
        's5_a_im': a_im_base + nrm((DEPTH, 2, S5_GROUPS, S5_STATE), 0.01),
        's5_log_dt': jax.random.uniform(next(ks), (DEPTH, 2, S5_GROUPS), F32,
                                        minval=math.log(S5_DT_MIN), maxval=math.log(S5_DT_MAX)),
        's5_b_re': nrm((DEPTH, S5_GROUPS, S5_STATE, S5_GROUP), (2 * S5_GROUP) ** -0.5),
        's5_b_im': nrm((DEPTH, S5_GROUPS, S5_STATE, S5_GROUP), (2 * S5_GROUP) ** -0.5),
        's5_c_re': nrm((DEPTH, S5_GROUPS, S5_GROUP, S5_STATE), S5_STATE ** -0.5),
        's5_c_im': nrm((DEPTH, S5_GROUPS, S5_GROUP, S5_STATE), S5_STATE ** -0.5),
        's5_d': nrm((DEPTH, S5_WIDTH), 1.0),
        's5_glu_w': nrm((DEPTH, S5_WIDTH, S5_WIDTH), S5_WIDTH ** -0.5),
        's5_glu_b': nrm((DEPTH, S5_WIDTH), 0.02),
        'attn_q_norm_w': 1.0 + nrm((DEPTH, ATT_HEAD_DIM), 0.02),
        'attn_k_norm_w': 1.0 + nrm((DEPTH, ATT_HEAD_DIM), 0.02),
        'w_branch_gate': nrm((DEPTH, D, N_BRANCH * D), D ** -0.5),
        'b_branch_gate': nrm((DEPTH, N_BRANCH * D), 0.02),
        'w_branch_proj': nrm((DEPTH, N_BRANCH, BRANCH_WIDTH, D), BRANCH_WIDTH ** -0.5),
        'w_out': nrm((DEPTH, D, D), D ** -0.5),
        'final_norm_w': 1.0 + nrm((D,), 0.02),
    }


def reference(x, c, ctx, c_ctx, w_ada, b_ada, norm_w, w_ffn_gate, w_ffn_up, w_ffn_down, w_in,
              gla_gate_w, gla_gate_b, gla_norm_w, s5_a_re, s5_a_im, s5_log_dt, s5_b_re, s5_b_im,
              s5_c_re, s5_c_im, s5_d, s5_glu_w, s5_glu_b, attn_q_norm_w, attn_k_norm_w,
              w_branch_gate, b_branch_gate, w_branch_proj, w_out, final_norm_w):
    rope_cos, rope_sin = axial_rope_tables(x.shape[1])
    sc = jax.nn.silu(c)
    scc = jax.nn.silu(c_ctx)
    h_lat, h_ctx = x, ctx
    for i in range(DEPTH):
        last = i == DEPTH - 1
        mod_l = jnp.split((sc @ w_ada[i] + b_ada[i])[:, None, :], N_MOD, axis=-1)
        mod_c = jnp.split((scc @ w_ada[i] + b_ada[i])[None, None, :], N_MOD, axis=-1)
        h_lat = ffn_sublayer(h_lat, mod_l[0], mod_l[1], mod_l[2], norm_w[i, 0],
                             w_ffn_gate[i, 0], w_ffn_up[i, 0], w_ffn_down[i, 0])
        h_ctx = ffn_sublayer(h_ctx, mod_c[0], mod_c[1], mod_c[2], norm_w[i, 0],
                             w_ffn_gate[i, 0], w_ffn_up[i, 0], w_ffn_down[i, 0])
        n_l = modulate(rms_norm(h_lat, norm_w[i, 1]), mod_l[3], mod_l[4])
        n_c = modulate(rms_norm(h_ctx, norm_w[i, 1]), mod_c[3], mod_c[4])
        p = {
            'w_in': w_in[i], 'gla_gate_w': gla_gate_w[i], 'gla_gate_b': gla_gate_b[i],
            'gla_norm_w': gla_norm_w[i], 's5_a_re': s5_a_re[i], 's5_a_im': s5_a_im[i],
            's5_log_dt': s5_log_dt[i], 's5_b_re': s5_b_re[i], 's5_b_im': s5_b_im[i],
            's5_c_re': s5_c_re[i], 's5_c_im': s5_c_im[i], 's5_d': s5_d[i],
            's5_glu_w': s5_glu_w[i], 's5_glu_b': s5_glu_b[i],
            'attn_q_norm_w': attn_q_norm_w[i], 'attn_k_norm_w': attn_k_norm_w[i],
            'w_branch_gate': w_branch_gate[i], 'b_branch_gate': b_branch_gate[i],
            'w_branch_proj': w_branch_proj[i], 'w_out': w_out[i],
        }
        out_c, out_l = token_mixer(n_c, n_l, p, rope_cos, rope_sin, not last)
        h_lat = h_lat + mod_l[5] * out_l
        h_lat = ffn_sublayer(h_lat, mod_l[6], mod_l[7], mod_l[8], norm_w[i, 2],
                             w_ffn_gate[i, 1], w_ffn_up[i, 1], w_ffn_down[i, 1])
        if not last:
            h_ctx = h_ctx + mod_c[5] * out_c
            h_ctx = ffn_sublayer(h_ctx, mod_c[6], mod_c[7], mod_c[8], norm_w[i, 2],
                                 w_ffn_gate[i, 1], w_ffn_up[i, 1], w_ffn_down[i, 1])
    return rms_norm(h_lat, final_norm_w)
```

```python
import functools

import jax
import jax.numpy as jnp
import numpy as np
from jax import lax
from jax.experimental import pallas as pl
from jax.experimental.pallas import tpu as pltpu

F32 = jnp.float32
BF16 = jnp.bfloat16

N_MOD = 9
MACARON_WEIGHT = 0.5
NORM_EPS = 1e-6
GRID_W = 64
GLA_HEADS = 4
GLA_DK = 64
GLA_DV = 128
GLA_QK = GLA_HEADS * GLA_DK
GLA_V = GLA_HEADS * GLA_DV
GLA_GATE_RANK = 16
GLA_TAU = 16.0
GLA_CHUNK = 64
S5_WIDTH = 512
S5_GROUP = 16
S5_GROUPS = S5_WIDTH // S5_GROUP
S5_STATE = 64
ATT_HEADS = 8
ATT_KV_HEADS = 2
ATT_GROUP = ATT_HEADS // ATT_KV_HEADS
ATT_HEAD_DIM = 64
ATT_Q = ATT_HEADS * ATT_HEAD_DIM
ATT_KV = ATT_KV_HEADS * ATT_HEAD_DIM
ROPE_PAIRS = ATT_HEAD_DIM // 4
ROPE_THETA = 10000.0
IN_WIDTHS = (GLA_QK, GLA_QK, GLA_V, GLA_GATE_RANK, GLA_GATE_RANK, GLA_V, S5_WIDTH, ATT_Q, ATT_KV, ATT_KV)

LANES = 128
MXU_DIM = 256
VMEM_LIMIT = 56 * 1024 * 1024

S5_T = MXU_DIM // S5_GROUP
S5_CW = S5_T * S5_GROUP

TILES = dict(ffn=512, inproj=512, merge=512, gla=512, s5_scan=32, att_q=128, att_k=1280)


def _params(*sem):
    return pltpu.CompilerParams(dimension_semantics=sem, vmem_limit_bytes=VMEM_LIMIT)


def _const_spec(shape):
    nd = len(shape)
    return pl.BlockSpec(shape, lambda *_: (0,) * nd, pipeline_mode=pl.Buffered(1))


def _dotg(a, b, contract):
    return lax.dot_general(a, b, (((contract[0],), (contract[1],)), ((), ())),
                           precision=lax.Precision.DEFAULT, preferred_element_type=F32)


def _dot(a, b):
    return _dotg(a, b, (1, 0))


def _dot_hi_lo(x, w):
    hi = x.astype(BF16)
    lo = (x - hi.astype(F32)).astype(BF16)
    return _dot(hi, w) + _dot(lo, w)


def _rms_mod(h, g_norm, shift, scale):
    ms = jnp.mean(h * h, axis=-1, keepdims=True)
    y = h * lax.rsqrt(ms + NORM_EPS) * g_norm
    return y * (1.0 + scale) + shift


def _log_sigmoid(x):
    return jnp.minimum(x, 0.0) - jnp.log(1.0 + jnp.exp(-jnp.abs(x)))


def _ada_kernel(c_ref, w_ref, b_ref, o_ref):
    c = c_ref[...]
    sc = (c * jax.nn.sigmoid(c)).astype(BF16)
    o_ref[...] = _dot(sc, w_ref[...].astype(BF16)) + b_ref[...]


def _ada_call(cond, w, b):
    rows, d = cond.shape
    n = w.shape[1]
    tn = 9 * LANES
    return pl.pallas_call(
        _ada_kernel,
        grid=(n // tn,),
        in_specs=[pl.BlockSpec((rows, d), lambda j: (0, 0)),
                  pl.BlockSpec((d, tn), lambda j: (0, j)),
                  pl.BlockSpec((1, tn), lambda j: (0, j))],
        out_specs=pl.BlockSpec((rows, tn), lambda j: (0, j)),
        out_shape=jax.ShapeDtypeStruct((rows, n), F32),
        compiler_params=_params("arbitrary"),
        name="adaln",
    )(cond, w, b.reshape(1, n))


def _ffn_kernel(h_ref, mod_ref, gn_ref, wg_ref, wu_ref, wd_ref, fn_ref, o_ref, *, final_norm):
    h = h_ref[0]
    shift, scale, gate = mod_ref[0, 0:1], mod_ref[0, 1:2], mod_ref[0, 2:3]
    n = _rms_mod(h, gn_ref[...], shift, scale).astype(BF16)
    g = _dot(n, wg_ref[...])
    u = _dot(n, wu_ref[...])
    a = (g * jax.nn.sigmoid(g) * u).astype(BF16)
    y = h + (MACARON_WEIGHT * gate) * _dot(a, wd_ref[...])
    if final_norm:
        ms = jnp.mean(y * y, axis=-1, keepdims=True)
        y = y * lax.rsqrt(ms + NORM_EPS) * fn_ref[...]
    o_ref[0] = y


def _ffn_call(h, mod3, g_norm, wg, wu, wd, final_w, *, final_norm, tm):
    b, l, d = h.shape
    f = wg.shape[1]
    tm = min(tm, l)
    return pl.pallas_call(
        functools.partial(_ffn_kernel, final_norm=final_norm),
        grid=(b, l // tm),
        in_specs=[pl.BlockSpec((1, tm, d), lambda bi, i: (bi, i, 0)),
                  pl.BlockSpec((1, 3, d), lambda bi, i: (bi, 0, 0)),
                  _const_spec((1, d)), _const_spec((d, f)), _const_spec((d, f)), _const_spec((f, d)),
                  _const_spec((1, d))],
        out_specs=pl.BlockSpec((1, tm, d), lambda bi, i: (bi, i, 0)),
        out_shape=jax.ShapeDtypeStruct((b, l, d), F32),
        compiler_params=_params("parallel", "parallel"),
        name="ffn",
    )(h, mod3, g_norm.reshape(1, d), wg, wu, wd, final_w.reshape(1, d))


def _inproj_kernel(h_ref, mod_ref, gn_ref, wm_ref, wgz_ref, gw_ref, gb_ref, qw_ref, kw_ref,
                   bdq_ref, bdk_ref, cos_ref, sa_ref, sb_ref,
                   qk_ref, v_ref, r_ref, lg_ref, u_ref, aq_ref, ak_ref, av_ref, *, rope):
    h = h_ref[0]
    n = _rms_mod(h, gn_ref[...], mod_ref[0, 0:1], mod_ref[0, 1:2]).astype(BF16)
    z = _dot(n, wm_ref[...])
    o = 0
    q = z[:, o:o + GLA_QK] * (GLA_DK ** -0.5); o += GLA_QK
    k = z[:, o:o + GLA_QK]; o += GLA_QK
    qk_ref[0, :, 0:GLA_QK] = q
    qk_ref[0, :, GLA_QK:2 * GLA_QK] = k
    v_ref[0] = z[:, o:o + GLA_V]; o += GLA_V
    r_ref[0] = z[:, o:o + GLA_V]; o += GLA_V
    u_ref[0] = z[:, o:o + S5_WIDTH]; o += S5_WIDTH
    xq = z[:, o:o + ATT_Q]; o += ATT_Q
    xk = z[:, o:o + ATT_KV]; o += ATT_KV
    av_ref[0] = z[:, o:o + ATT_KV].astype(BF16)

    zg = _dot(n, wgz_ref[...]).astype(BF16)
    lg_ref[0] = _log_sigmoid(_dot(zg, gw_ref[...]) + gb_ref[...]) * (1.0 / GLA_TAU)

    def head_norm(x, bd_ref, w_ref):
        ss = _dot_hi_lo(x * x, bd_ref[...])
        return x * lax.rsqrt(ss * (1.0 / ATT_HEAD_DIM) + NORM_EPS) * w_ref[...]

    def rotary(x, width):
        if not rope:
            return x
        c, sa, sb = cos_ref[:, 0:width], sa_ref[:, 0:width], sb_ref[:, 0:width]
        return (x * c + pltpu.roll(x, width - ROPE_PAIRS, 1) * sa + pltpu.roll(x, ROPE_PAIRS, 1) * sb)

    aq_ref[0] = rotary(head_norm(xq, bdq_ref, qw_ref), ATT_Q).astype(BF16)
    ak_ref[0] = rotary(head_norm(xk, bdk_ref, kw_ref), ATT_KV).astype(BF16)


def _inproj_call(h, mod2, g_norm, lw, tables, *, rope, tm):
    b, l, d = h.shape
    tm = min(tm, l)
    tok = lambda w: pl.BlockSpec((1, tm, w), lambda bi, i: (bi, i, 0))
    tab = pl.BlockSpec((tm, ATT_Q), lambda bi, i: (i, 0))
    wm = lw["w_main"]
    outs = [(2 * GLA_QK, F32), (GLA_V, F32), (GLA_V, F32), (2 * GLA_QK, F32), (S5_WIDTH, F32),
            (ATT_Q, BF16), (ATT_KV, BF16), (ATT_KV, BF16)]
    return pl.pallas_call(
        functools.partial(_inproj_kernel, rope=rope),
        grid=(b, l // tm),
        in_specs=[tok(d), pl.BlockSpec((1, 2, d), lambda bi, i: (bi, 0, 0)), _const_spec((1, d)),
                  _const_spec(wm.shape), _const_spec((d, LANES)), _const_spec((LANES, 2 * GLA_QK)),
                  _const_spec((1, 2 * GLA_QK)), _const_spec((1, ATT_Q)), _const_spec((1, ATT_KV)),
                  _const_spec((ATT_Q, ATT_Q)), _const_spec((ATT_KV, ATT_KV)), tab, tab, tab],
        out_specs=[tok(w) for w, _ in outs],
        out_shape=[jax.ShapeDtypeStruct((b, l, w), dt) for w, dt in outs],
        compiler_params=_params("parallel", "parallel"),
        name="inproj",
    )(h, mod2, g_norm.reshape(1, d), wm, lw["w_gz"], lw["gate_w_bd"], lw["gate_b"], lw["q_norm_w"],
      lw["k_norm_w"], lw["bd_q"], lw["bd_k"], *tables)


def _gla_kernel(qk_ref, v_ref, lg_ref, s0_ref, tri_ref, of_ref, r_ref, nw_ref, o_ref, sfin_ref, s_scr,
                *, reverse, readout, n_chunks):
    i = pl.program_id(1)

    @pl.when(i == 0)
    def _():
        s_scr[...] = s0_ref[0]

    tri = tri_ref[...]
    c = GLA_CHUNK
    row = lax.broadcasted_iota(jnp.int32, (c, c), 0)
    col = lax.broadcasted_iota(jnp.int32, (c, c), 1)
    keep = (col >= row) if reverse else (col <= row)
    lane = lax.broadcasted_iota(jnp.int32, (c, LANES), 1)
    low = lane < GLA_DK
    order = range(n_chunks - 1, -1, -1) if reverse else range(n_chunks)
    for ci in order:
        rows = slice(ci * c, (ci + 1) * c)
        g = lg_ref[0, rows, :]
        g_hi = g.astype(BF16)
        b = _dot(tri, g_hi) + _dot(tri, (g - g_hi.astype(F32)).astype(BF16))
        b_last = b[0:1, :] if reverse else b[c - 1:c, :]
        q_in = qk_ref[0, rows, 0:GLA_QK] * jnp.exp(b)
        k = qk_ref[0, rows, GLA_QK:2 * GLA_QK]
        k_in = (k * jnp.exp(-b)).astype(BF16)
        k_out = (k * jnp.exp(b_last - b)).astype(BF16)
        dec = jnp.exp(jnp.broadcast_to(b_last, (LANES, GLA_QK)).T)
        for pair in range(GLA_HEADS // 2):
            lanes = slice(pair * LANES, (pair + 1) * LANES)
            s_pair = s_scr[lanes, :]
            s_pair_b = s_pair.astype(BF16)
            qp, kp_in, kp_out = q_in[:, lanes], k_in[:, lanes], k_out[:, lanes]
            ds = []
            for e in range(2):
                hd = 2 * pair + e
                vcols = slice(hd * GLA_DV, (hd + 1) * GLA_DV)
                qm = jnp.where(low if e == 0 else jnp.logical_not(low), qp, 0.0).astype(BF16)
                v_h = v_ref[0, rows, vcols].astype(BF16)
                att = _dotg(qm, kp_in, (1, 1))
                att = jnp.where(keep, att, 0.0).astype(BF16)
                o_h = _dot(att, v_h) + _dot(qm, s_pair_b)
                ds.append(_dotg(kp_out, v_h, (0, 0)))
                if readout:
                    o_h = o_h + of_ref[0, rows, vcols]
                    ms = jnp.mean(o_h * o_h, axis=-1, keepdims=True)
                    y = o_h * lax.rsqrt(ms + NORM_EPS) * nw_ref[...]
                    rr = r_ref[0, rows, vcols]
                    o_h = y * (rr * jax.nn.sigmoid(rr))
                o_ref[0, rows, vcols] = o_h
            d_state = jnp.concatenate([ds[0][0:GLA_DK], ds[1][GLA_DK:2 * GLA_DK]], axis=0)
            s_scr[lanes, :] = dec[lanes, :] * s_pair + d_state

    @pl.when(i == pl.num_programs(1) - 1)
    def _():
        sfin_ref[0] = s_scr[...]


def _gla_call(qk, v, lg, s0, o_f, r, norm_w, *, reverse, readout, tb):
    b, l, _ = v.shape
    tb = min(tb, l)
    nb = l // tb
    idx = (lambda bi, i: (bi, nb - 1 - i, 0)) if reverse else (lambda bi, i: (bi, i, 0))
    tok = lambda w: pl.BlockSpec((1, tb, w), idx)
    lg_off = 1 if reverse else 0
    lg_spec = pl.BlockSpec((1, tb, GLA_QK), (lambda bi, i: (bi, nb - 1 - i, lg_off)) if reverse
                           else (lambda bi, i: (bi, i, lg_off)))
    state = pl.BlockSpec((1, GLA_QK, GLA_DV), lambda bi, i: (bi, 0, 0))
    ones = np.triu(np.ones((GLA_CHUNK, GLA_CHUNK), np.float32)) if reverse else \
        np.tril(np.ones((GLA_CHUNK, GLA_CHUNK), np.float32))
    return pl.pallas_call(
        functools.partial(_gla_kernel, reverse=reverse, readout=readout, n_chunks=tb // GLA_CHUNK),
        grid=(b, nb),
        in_specs=[tok(2 * GLA_QK), tok(GLA_V), lg_spec, state, _const_spec((GLA_CHUNK, GLA_CHUNK)),
                  tok(GLA_V), tok(GLA_V), _const_spec((1, GLA_DV))],
        out_specs=[tok(GLA_V), state],
        out_shape=[jax.ShapeDtypeStruct((b, l, GLA_V), F32), jax.ShapeDtypeStruct((b, GLA_QK, GLA_DV), F32)],
        scratch_shapes=[pltpu.VMEM((GLA_QK, GLA_DV), F32)],
        compiler_params=_params("parallel", "arbitrary"),
        name="gla_bwd" if reverse else "gla_fwd",
    )(qk, v, lg, s0, jnp.asarray(ones, BF16), o_f, r, norm_w.reshape(1, GLA_DV))


def _gla_bidir(qk, v, lg, r, norm_w, s0_f, s0_b, *, tb):
    o_f, s_f = _gla_call(qk, v, lg, s0_f, v, r, norm_w, reverse=False, readout=False, tb=tb)
    y, s_b = _gla_call(qk, v, lg, s0_b, o_f, r, norm_w, reverse=True, readout=True, tb=tb)
    return y, s_f, s_b


def _s5_operators(a_re, a_im, log_dt, b_re, b_im, c_re, c_im):
    hp = lax.Precision.HIGHEST
    t = S5_T
    cr, ci_ = c_re.astype(F32), c_im.astype(F32)
    ks = jnp.arange(t + 1, dtype=F32)[:, None, None]
    per_dir = []
    for d in range(2):
        dt = jnp.exp(log_dt[d].astype(F32))[:, None]
        ar, ai = a_re[d].astype(F32), a_im[d].astype(F32)
        mag = jnp.exp(ks * dt * ar)
        pw_re, pw_im = mag * jnp.cos(ks * dt * ai), mag * jnp.sin(ks * dt * ai)
        den = ar * ar + ai * ai
        xr, xi = pw_re[1] - 1.0, pw_im[1]
        coef_re = (xr * ar + xi * ai) / den
        coef_im = (xi * ar - xr * ai) / den
        br, bi = b_re.astype(F32), b_im.astype(F32)
        bb_re = coef_re[..., None] * br - coef_im[..., None] * bi
        bb_im = coef_re[..., None] * bi + coef_im[..., None] * br
        ca_re = cr[None] * pw_re[:, :, None, :] - ci_[None] * pw_im[:, :, None, :]
        ca_im = cr[None] * pw_im[:, :, None, :] + ci_[None] * pw_re[:, :, None, :]
        kk = (jnp.einsum("kgop,gpi->kgoi", ca_re, bb_re, precision=hp)
              - jnp.einsum("kgop,gpi->kgoi", ca_im, bb_im, precision=hp))
        s_idx = jnp.arange(t)[:, None]
        t_idx = jnp.arange(t)[None, :]
        lag = (t_idx - s_idx) if d == 0 else (s_idx - t_idx)
        toe = jnp.where((lag >= 0)[:, :, None, None, None], kk[jnp.clip(lag, 0, t)], 0.0)
        m = jnp.transpose(toe, (2, 0, 4, 1, 3)).reshape(S5_GROUPS, S5_CW, S5_CW)
        p_pow = (t - 1 - jnp.arange(t)) if d == 0 else jnp.arange(t)
        pr, pi = pw_re[p_pow], pw_im[p_pow]
        p_re = pr[:, :, :, None] * bb_re[None] - pi[:, :, :, None] * bb_im[None]
        p_im = pr[:, :, :, None] * bb_im[None] + pi[:, :, :, None] * bb_re[None]
        flat_p = lambda x: jnp.transpose(x, (1, 0, 3, 2)).reshape(S5_GROUPS, S5_CW, S5_STATE)
        q_pow = (jnp.arange(t) + 1) if d == 0 else (t - jnp.arange(t))
        q_re, q_im = ca_re[q_pow], -ca_im[q_pow]
        flat_q = lambda x: jnp.transpose(x, (1, 3, 0, 2)).reshape(S5_GROUPS, S5_STATE, S5_CW)
        per_dir.append(dict(m=m, p_re=flat_p(p_re), p_im=flat_p(p_im), q_re=flat_q(q_re), q_im=flat_q(q_im),
                            at_re=pw_re[t], at_im=pw_im[t]))
    f, bk = per_dir
    zq = jnp.zeros_like(f["q_re"])
    p_cat = jnp.concatenate([f["p_re"], bk["p_re"], f["p_im"], bk["p_im"]], axis=-1)
    w = jnp.concatenate([f["m"] + bk["m"],
                         f["q_re"], zq, f["q_im"], zq,
                         zq, bk["q_re"], zq, bk["q_im"]], axis=1)
    at_re = jnp.concatenate([f["at_re"], bk["at_re"]], axis=-1)
    at_im = jnp.concatenate([f["at_im"], bk["at_im"]], axis=-1)
    return p_cat.astype(BF16), w.astype(BF16), at_re, at_im


def _s5_x_kernel(u_ref, p_ref, x_ref):
    x_ref[...] = _dot(u_ref[0], p_ref[0])


def _s5_x_call(uf, p_cat):
    g, r, cw = uf.shape
    return pl.pallas_call(
        _s5_x_kernel,
        grid=(g,),
        in_specs=[pl.BlockSpec((1, r, cw), lambda gi: (gi, 0, 0)),
                  pl.BlockSpec((1, cw, cw), lambda gi: (gi, 0, 0))],
        out_specs=pl.BlockSpec((r, cw), lambda gi: (0, gi)),
        out_shape=jax.ShapeDtypeStruct((r, g * cw), F32),
        compiler_params=_params("parallel"),
        name="s5_x",
    )(uf, p_cat)


def _s5_scan_kernel(xf_ref, xb_ref, are_ref, aim_ref, h0_ref, hf_ref, hb_ref, hfin_ref, hr_scr, hi_scr, *, nb):
    i = pl.program_id(0)

    @pl.when(i == 0)
    def _():
        hr_scr[...] = h0_ref[:, 0]
        hi_scr[...] = h0_ref[:, 1]

    ar, ai = are_ref[...][None], aim_ref[...][None]
    fwd = lax.broadcasted_iota(jnp.int32, (1, S5_GROUPS, LANES), 2) < S5_STATE

    def step(j, carry):
        hr, hi = carry
        jb = nb - 1 - j
        xr = jnp.where(fwd, xf_ref[:, j, :, 0:LANES], xb_ref[:, jb, :, 0:LANES])
        xi = jnp.where(fwd, xf_ref[:, j, :, LANES:2 * LANES], xb_ref[:, jb, :, LANES:2 * LANES])
        hf_ref[:, j, :, 0:LANES] = hr
        hf_ref[:, j, :, LANES:2 * LANES] = hi
        hb_ref[:, jb, :, 0:LANES] = hr
        hb_ref[:, jb, :, LANES:2 * LANES] = hi
        return ar * hr - ai * hi + xr, ar * hi + ai * hr + xi

    hr, hi = lax.fori_loop(0, nb, step, (hr_scr[...], hi_scr[...]))
    hr_scr[...] = hr
    hi_scr[...] = hi

    @pl.when(i == pl.num_programs(0) - 1)
    def _():
        hfin_ref[:, 0] = hr
        hfin_ref[:, 1] = hi


def _s5_scan_call(x4, at_re, at_im, h0, *, nb):
    b, n, g, w = x4.shape
    nb = min(nb, n)
    steps = n // nb
    blk = lambda rev: pl.BlockSpec((b, nb, g, w), (lambda i: (0, steps - 1 - i, 0, 0)) if rev
                                   else (lambda i: (0, i, 0, 0)))
    st = pl.BlockSpec((b, 2, g, LANES), lambda i: (0, 0, 0, 0))
    return pl.pallas_call(
        functools.partial(_s5_scan_kernel, nb=nb),
        grid=(steps,),
        in_specs=[blk(False), blk(True), _const_spec((g, LANES)), _const_spec((g, LANES)), st],
        out_specs=[blk(False), blk(True), st],
        out_shape=[jax.ShapeDtypeStruct(x4.shape, F32), jax.ShapeDtypeStruct(x4.shape, F32),
                   jax.ShapeDtypeStruct((b, 2, g, LANES), F32)],
        scratch_shapes=[pltpu.VMEM((b, g, LANES), F32), pltpu.VMEM((b, g, LANES), F32)],
        compiler_params=_params("arbitrary"),
        name="s5_scan",
    )(x4, x4, at_re, at_im, h0)


def _s5_y_kernel(u_ref, hf_ref, hb_ref, w_ref, y_ref):
    cw = S5_CW
    y_ref[0] = (_dot(u_ref[0], w_ref[0, 0:cw]) + _dot(hf_ref[...].astype(BF16), w_ref[0, cw:2 * cw])
                + _dot(hb_ref[...].astype(BF16), w_ref[0, 2 * cw:3 * cw]))


def _s5_y_call(uf, hf2, hb2, w):
    g, r, cw = uf.shape
    return pl.pallas_call(
        _s5_y_kernel,
        grid=(g,),
        in_specs=[pl.BlockSpec((1, r, cw), lambda gi: (gi, 0, 0)),
                  pl.BlockSpec((r, cw), lambda gi: (0, gi)),
                  pl.BlockSpec((r, cw), lambda gi: (0, gi)),
                  pl.BlockSpec((1, 3 * cw, cw), lambda gi: (gi, 0, 0))],
        out_specs=pl.BlockSpec((1, r, cw), lambda gi: (gi, 0, 0)),
        out_shape=jax.ShapeDtypeStruct((g, r, cw), F32),
        compiler_params=_params("parallel"),
        name="s5_y",
    )(uf, hf2, hb2, w)


def _s5_mix(u, ops, h0, *, nb):
    p_cat, w, at_re, at_im = ops
    b, l, _ = u.shape
    n = l // S5_T
    uf = u.astype(BF16).reshape(b, n, S5_T, S5_GROUPS, S5_GROUP)
    uf = jnp.transpose(uf, (3, 0, 1, 2, 4)).reshape(S5_GROUPS, b * n, S5_CW)
    x = _s5_x_call(uf, p_cat)
    hf, hb, hfin = _s5_scan_call(x.reshape(b, n, S5_GROUPS, S5_CW), at_re, at_im, h0, nb=nb)
    y = _s5_y_call(uf, hf.reshape(b * n, S5_GROUPS * S5_CW), hb.reshape(b * n, S5_GROUPS * S5_CW), w)
    y = y.reshape(S5_GROUPS, b, n, S5_T, S5_GROUP)
    y = jnp.transpose(y, (1, 2, 3, 0, 4)).reshape(b, l, S5_WIDTH)
    return y, hfin


def _attn_kernel(q_ref, kt_ref, v_ref, o_ref, m_scr, acc_scr, *, tk, n_kv):
    rows = q_ref.shape[2] * q_ref.shape[3]
    q = q_ref[0, 0].reshape(rows, LANES)
    m_scr[...] = jnp.full(m_scr.shape, -jnp.inf, F32)
    acc_scr[...] = jnp.zeros(acc_scr.shape, F32)

    def body(j, carry):
        off = pl.multiple_of(j * tk, tk)
        s = _dot(q, kt_ref[0, :, pl.ds(off, tk)])
        m_prev = m_scr[...]
        m_new = jnp.maximum(m_prev, jnp.max(s, axis=1, keepdims=True))
        alpha = jnp.exp(m_prev - m_new)
        p = jnp.exp(s - jnp.tile(m_new, (1, tk // LANES))).astype(BF16)
        acc_scr[...] = acc_scr[...] * jnp.tile(alpha, (1, 2)) + _dot(p, v_ref[0, pl.ds(off, tk), :])
        m_scr[...] = m_new
        return carry

    lax.fori_loop(0, n_kv, body, 0)
    acc = acc_scr[...]
    out = acc[:, 0:LANES] / acc[:, LANES:2 * LANES]
    o_ref[0, 0] = out.reshape(q_ref.shape[2:]).astype(o_ref.dtype)


def _attn_call(qs, kt, v_ext, *, tq, tk):
    b, kvh, grp, l, _ = qs.shape
    lk = kt.shape[2]
    tq, tk = min(tq, l), min(tk, lk)
    rows = grp * tq
    blk = pl.BlockSpec((1, 1, grp, tq, LANES), lambda bi, ki, i: (bi, ki, 0, i, 0))
    return pl.pallas_call(
        functools.partial(_attn_kernel, tk=tk, n_kv=lk // tk),
        grid=(b, kvh, l // tq),
        in_specs=[blk,
                  pl.BlockSpec((1, LANES, lk), lambda bi, ki, i: (bi, 0, 0), pipeline_mode=pl.Buffered(1)),
                  pl.BlockSpec((1, lk, 2 * LANES), lambda bi, ki, i: (bi, 0, 0), pipeline_mode=pl.Buffered(1))],
        out_specs=blk,
        out_shape=jax.ShapeDtypeStruct(qs.shape, BF16),
        scratch_shapes=[pltpu.VMEM((rows, LANES), F32), pltpu.VMEM((rows, 2 * LANES), F32)],
        compiler_params=_params("parallel", "parallel", "arbitrary"),
        name="attention",
    )(qs, kt, v_ext)


def _stack_q(aq):
    b, l, _ = aq.shape
    q = aq.reshape(b, l, ATT_KV_HEADS, ATT_GROUP, ATT_HEAD_DIM)
    q = jnp.transpose(q, (0, 2, 3, 1, 4))
    z = jnp.zeros_like(q[:, 0])
    return jnp.stack([jnp.concatenate([q[:, 0], z], -1), jnp.concatenate([z, q[:, 1]], -1)], axis=1)


def _unstack_o(o):
    b, _, _, l, _ = o.shape
    half = ATT_GROUP * ATT_HEAD_DIM
    kv0 = jnp.transpose(o[:, 0, :, :, 0:ATT_HEAD_DIM], (0, 2, 1, 3)).reshape(b, l, half)
    kv1 = jnp.transpose(o[:, 1, :, :, ATT_HEAD_DIM:], (0, 2, 1, 3)).reshape(b, l, half)
    return jnp.concatenate([kv0, kv1], axis=-1)


def _kv_layout(ak, av):
    kt = jnp.transpose(ak, (0, 2, 1))
    v_ext = jnp.concatenate([av, jnp.ones_like(av)], axis=-1)
    return kt, v_ext


def _merge_kernel(h_ref, mod_ref, gn_ref, yg_ref, ys_ref, u_ref, ya_ref, wbg_ref, bbg_ref, wp_ref, wo_ref,
                  d_ref, gw_ref, gb_ref, o_ref):
    h = h_ref[0]
    d = h.shape[-1]
    n = _rms_mod(h, gn_ref[...], mod_ref[0, 0:1], mod_ref[0, 1:2]).astype(BF16)
    g = jax.nn.sigmoid(_dot(n, wbg_ref[...]) + bbg_ref[...])
    y = jax.nn.gelu(ys_ref[0] + d_ref[...] * u_ref[0])
    y_s5 = y * jax.nn.sigmoid(_dot(y.astype(BF16), gw_ref[...]) + gb_ref[...])
    m = (g[:, 0:d] * _dot(yg_ref[0].astype(BF16), wp_ref[0])
         + g[:, d:2 * d] * _dot(y_s5.astype(BF16), wp_ref[1])
         + g[:, 2 * d:3 * d] * _dot(ya_ref[0], wp_ref[2]))
    o_ref[0] = h + mod_ref[0, 2:3] * _dot(m.astype(BF16), wo_ref[...])


def _merge_call(h, mod3, g_norm, y_gla, y_s5raw, u, y_att, lw, *, tm):
    b, l, d = h.shape
    tm = min(tm, l)
    tok = lambda w: pl.BlockSpec((1, tm, w), lambda bi, i: (bi, i, 0))
    bw = y_gla.shape[-1]
    return pl.pallas_call(
        _merge_kernel,
        grid=(b, l // tm),
        in_specs=[tok(d), pl.BlockSpec((1, 3, d), lambda bi, i: (bi, 0, 0)), _const_spec((1, d)),
                  tok(bw), tok(bw), tok(bw), tok(bw),
                  _const_spec((d, 3 * d)), _const_spec((1, 3 * d)), _const_spec((3, bw, d)), _const_spec((d, d)),
                  _const_spec((1, bw)), _const_spec((bw, bw)), _const_spec((1, bw))],
        out_specs=tok(d),
        out_shape=jax.ShapeDtypeStruct((b, l, d), F32),
        compiler_params=_params("parallel", "parallel"),
        name="merge",
    )(h, mod3, g_norm.reshape(1, d), y_gla, y_s5raw, u, y_att, lw["w_bgate"], lw["b_bgate"], lw["w_bproj"],
      lw["w_out"], lw["s5_d"], lw["glu_w"], lw["glu_b"])


def _rope_tables(n_tokens):
    rows = n_tokens // GRID_W
    row = jnp.repeat(jnp.arange(rows, dtype=F32), GRID_W)
    col = jnp.tile(jnp.arange(GRID_W, dtype=F32), rows)
    inv = ROPE_THETA ** (-jnp.arange(ROPE_PAIRS, dtype=F32) / ROPE_PAIRS)
    ang_r, ang_c = row[:, None] * inv, col[:, None] * inv
    zero = jnp.zeros_like(ang_r)
    cos = jnp.concatenate([jnp.cos(ang_r), jnp.cos(ang_r), jnp.cos(ang_c), jnp.cos(ang_c)], axis=-1)
    sin_a = jnp.concatenate([-jnp.sin(ang_r), zero, -jnp.sin(ang_c), zero], axis=-1)
    sin_b = jnp.concatenate([zero, jnp.sin(ang_r), zero, jnp.sin(ang_c)], axis=-1)
    tile = lambda t: jnp.tile(t, (1, ATT_HEADS))
    return tile(cos), tile(sin_a), tile(sin_b)


def _block_diag_ones(width, seg):
    idx = np.arange(width) // seg
    return jnp.asarray((idx[:, None] == idx[None, :]).astype(np.float32), BF16)


def _layer_weights(i, w_in, gla_gate_w, gla_gate_b, attn_q_norm_w, attn_k_norm_w, w_branch_gate,
                   b_branch_gate, w_branch_proj, w_out, s5_d, s5_glu_w, s5_glu_b):
    offs = np.concatenate([[0], np.cumsum(IN_WIDTHS)])
    col = lambda k: w_in[i][:, offs[k]:offs[k + 1]]
    w_main = jnp.concatenate([col(0), col(1), col(2), col(5), col(6), col(7), col(8), col(9)], axis=1)
    d = w_in.shape[1]
    rk = GLA_GATE_RANK
    w_gz = jnp.zeros((d, LANES), F32).at[:, 0:rk].set(col(3)).at[:, rk:2 * rk].set(col(4))
    gw = jnp.zeros((LANES, 2 * GLA_QK), F32)
    gw = gw.at[0:rk, 0:GLA_QK].set(gla_gate_w[i, 0]).at[rk:2 * rk, GLA_QK:].set(gla_gate_w[i, 1])
    return dict(
        w_main=w_main.astype(BF16), w_gz=w_gz.astype(BF16), gate_w_bd=gw.astype(BF16),
        gate_b=gla_gate_b[i].reshape(1, 2 * GLA_QK),
        q_norm_w=(jnp.tile(attn_q_norm_w[i], ATT_HEADS) * (ATT_HEAD_DIM ** -0.5)).reshape(1, ATT_Q),
        k_norm_w=jnp.tile(attn_k_norm_w[i], ATT_KV_HEADS).reshape(1, ATT_KV),
        bd_q=_block_diag_ones(ATT_Q, ATT_HEAD_DIM), bd_k=_block_diag_ones(ATT_KV, ATT_HEAD_DIM),
        w_bgate=w_branch_gate[i].astype(BF16), b_bgate=b_branch_gate[i].reshape(1, -1),
        w_bproj=w_branch_proj[i].astype(BF16), w_out=w_out[i].astype(BF16),
        s5_d=s5_d[i].reshape(1, -1), glu_w=s5_glu_w[i].astype(BF16), glu_b=s5_glu_b[i].reshape(1, -1),
    )


def _mixer_inputs(h, mod, g_norm, lw, tables, *, rope, tm):
    mod2 = mod[:, 3:5]
    return _inproj_call(h, mod2, g_norm, lw, tables, rope=rope, tm=tm)


def kernel(x, c, ctx, c_ctx, w_ada, b_ada, norm_w, w_ffn_gate, w_ffn_up, w_ffn_down, w_in, gla_gate_w,
           gla_gate_b, gla_norm_w, s5_a_re, s5_a_im, s5_log_dt, s5_b_re, s5_b_im, s5_c_re, s5_c_im, s5_d,
           s5_glu_w, s5_glu_b, attn_q_norm_w, attn_k_norm_w, w_branch_gate, b_branch_gate, w_branch_proj,
           w_out, final_norm_w):
    bsz, seq, d = x.shape
    depth = w_ada.shape[0]
    tables = _rope_tables(seq)
    ctx_tables = tuple(t[0:ctx.shape[1]] for t in tables)
    cond = jnp.zeros((8, d), F32).at[0:bsz].set(c).at[bsz].set(c_ctx)
    tm_ffn, tm_in, tm_merge, tb_gla, nb_s5, tq, tk = (TILES[k] for k in
                                                      ("ffn", "inproj", "merge", "gla", "s5_scan", "att_q", "att_k"))

    h_lat, h_ctx = x, ctx
    for i in range(depth):
        last = i == depth - 1
        mods = _ada_call(cond, w_ada[i], b_ada[i])
        mod_l = mods[0:bsz].reshape(bsz, N_MOD, d)
        mod_c = jnp.broadcast_to(mods[bsz].reshape(1, N_MOD, d), (bsz, N_MOD, d))
        lw = _layer_weights(i, w_in, gla_gate_w, gla_gate_b, attn_q_norm_w, attn_k_norm_w, w_branch_gate,
                            b_branch_gate, w_branch_proj, w_out, s5_d, s5_glu_w, s5_glu_b)
        ffn_w = [(w_ffn_gate[i, j].astype(BF16), w_ffn_up[i, j].astype(BF16), w_ffn_down[i, j].astype(BF16))
                 for j in range(2)]

        def ffn(h, mod, j, fin=False):
            return _ffn_call(h, mod[:, 6 * j:6 * j + 3], norm_w[i, 2 * j], *ffn_w[j], final_norm_w,
                             final_norm=fin, tm=tm_ffn)

        h_lat = ffn(h_lat, mod_l, 0)
        h_ctx = ffn(h_ctx, mod_c, 0)

        qk_c, v_c, r_c, lg_c, u_c, aq_c, ak_c, av_c = _mixer_inputs(h_ctx, mod_c, norm_w[i, 1], lw, ctx_tables,
                                                                    rope=False, tm=tm_in)
        qk_l, v_l, r_l, lg_l, u_l, aq_l, ak_l, av_l = _mixer_inputs(h_lat, mod_l, norm_w[i, 1], lw, tables,
                                                                    rope=True, tm=tm_in)
        s0 = jnp.zeros((bsz, GLA_QK, GLA_DV), F32)
        yg_c, sf_c, sb_c = _gla_bidir(qk_c, v_c, lg_c, r_c, gla_norm_w[i], s0, s0, tb=tb_gla)
        yg_l, _, _ = _gla_bidir(qk_l, v_l, lg_l, r_l, gla_norm_w[i], sf_c, sb_c, tb=tb_gla)
        ops = _s5_operators(s5_a_re[i], s5_a_im[i], s5_log_dt[i], s5_b_re[i], s5_b_im[i], s5_c_re[i], s5_c_im[i])
        h0 = jnp.zeros((bsz, 2, S5_GROUPS, LANES), F32)
        ys_c, hfin_c = _s5_mix(u_c, ops, h0, nb=nb_s5)
        ys_l, _ = _s5_mix(u_l, ops, hfin_c, nb=nb_s5)
        kt, v_ext = _kv_layout(jnp.concatenate([ak_c, ak_l], axis=1), jnp.concatenate([av_c, av_l], axis=1))
        ya_l = _unstack_o(_attn_call(_stack_q(aq_l), kt, v_ext, tq=tq, tk=tk))
        h_lat = _merge_call(h_lat, mod_l[:, 3:6], norm_w[i, 1], yg_l, ys_l, u_l, ya_l, lw, tm=tm_merge)
        h_lat = ffn(h_lat, mod_l, 1, last)
        if not last:
            kt_c, v_ext_c = _kv_layout(ak_c, av_c)
            ya_c = _unstack_o(_attn_call(_stack_q(aq_c), kt_c, v_ext_c, tq=tq, tk=tk))
            h_ctx = _merge_call(h_ctx, mod_c[:, 3:6], norm_w[i, 1], yg_c, ys_c, u_c, ya_c, lw, tm=tm_merge)
            h_ctx = ffn(h_ctx, mod_c, 1)
    return h_lat
```

```python
import functools

import jax
import jax.numpy as jnp
import numpy as np
from jax import lax
from jax.experimental import pallas as pl
from jax.experimental.pallas import tpu as pltpu

F32 = jnp.float32
BF16 = jnp.bfloat16

N_MOD = 9
MACARON_WEIGHT = 0.5
NORM_EPS = 1e-6
GRID_W = 64
GLA_HEADS = 4
GLA_DK = 64
GLA_DV = 128
GLA_QK = GLA_HEADS * GLA_DK
GLA_V = GLA_HEADS * GLA_DV
GLA_GATE_RANK = 16
GLA_TAU = 16.0
GLA_CHUNK = 64
S5_WIDTH = 512
S5_GROUP = 16
S5_GROUPS = S5_WIDTH // S5_GROUP
S5_STATE = 64
ATT_HEADS = 8
ATT_KV_HEADS = 2
ATT_GROUP = ATT_HEADS // ATT_KV_HEADS
ATT_HEAD_DIM = 64
ATT_Q = ATT_HEADS * ATT_HEAD_DIM
ATT_KV = ATT_KV_HEADS * ATT_HEAD_DIM
ROPE_PAIRS = ATT_HEAD_DIM // 4
ROPE_THETA = 10000.0
IN_WIDTHS = (GLA_QK, GLA_QK, GLA_V, GLA_GATE_RANK, GLA_GATE_RANK, GLA_V, S5_WIDTH, ATT_Q, ATT_KV, ATT_KV)

LANES = 128
MXU_DIM = 256
VMEM_LIMIT = 56 * 1024 * 1024

LOG2_E = 1.4426950408889634

S5_T = MXU_DIM // S5_GROUP
S5_CW = S5_T * S5_GROUP

TILES = dict(ffn=512, inproj=512, merge=512, gla=512, s5_scan=32, att_q=128, att_k=1280)


def _params(*sem):
    return pltpu.CompilerParams(dimension_semantics=sem, vmem_limit_bytes=VMEM_LIMIT)


def _const_spec(shape):
    nd = len(shape)
    return pl.BlockSpec(shape, lambda *_: (0,) * nd, pipeline_mode=pl.Buffered(1))


def _dotg(a, b, contract):
    return lax.dot_general(a, b, (((contract[0],), (contract[1],)), ((), ())),
                           precision=lax.Precision.DEFAULT, preferred_element_type=F32)


def _dot(a, b):
    return _dotg(a, b, (1, 0))


def _dot_hi_lo(x, w):
    hi = x.astype(BF16)
    lo = (x - hi.astype(F32)).astype(BF16)
    return _dot(hi, w) + _dot(lo, w)


def _rms_mod(h, g_norm, shift, scale):
    ms = jnp.mean(h * h, axis=-1, keepdims=True)
    y = h * lax.rsqrt(ms + NORM_EPS) * g_norm
    return y * (1.0 + scale) + shift


def _log_sigmoid(x):
    return jnp.minimum(x, 0.0) - jnp.log(1.0 + jnp.exp(-jnp.abs(x)))


def _ada_kernel(c_ref, w_ref, b_ref, o_ref):
    c = c_ref[...]
    sc = (c * jax.nn.sigmoid(c)).astype(BF16)
    o_ref[...] = _dot(sc, w_ref[...].astype(BF16)) + b_ref[...]


def _ada_call(cond, w, b):
    rows, d = cond.shape
    n = w.shape[1]
    tn = 9 * LANES
    return pl.pallas_call(
        _ada_kernel,
        grid=(n // tn,),
        in_specs=[pl.BlockSpec((rows, d), lambda j: (0, 0)),
                  pl.BlockSpec((d, tn), lambda j: (0, j)),
                  pl.BlockSpec((1, tn), lambda j: (0, j))],
        out_specs=pl.BlockSpec((rows, tn), lambda j: (0, j)),
        out_shape=jax.ShapeDtypeStruct((rows, n), F32),
        compiler_params=_params("arbitrary"),
        name="adaln",
    )(cond, w, b.reshape(1, n))


def _ffn_kernel(h_ref, mod_ref, gn_ref, wg_ref, wu_ref, wd_ref, fn_ref, o_ref, *, final_norm):
    h = h_ref[0]
    shift, scale, gate = mod_ref[0, 0:1], mod_ref[0, 1:2], mod_ref[0, 2:3]
    n = _rms_mod(h, gn_ref[...], shift, scale).astype(BF16)
    g = _dot(n, wg_ref[...])
    u = _dot(n, wu_ref[...])
    a = (g * jax.nn.sigmoid(g) * u).astype(BF16)
    y = h + (MACARON_WEIGHT * gate) * _dot(a, wd_ref[...])
    if final_norm:
        ms = jnp.mean(y * y, axis=-1, keepdims=True)
        y = y * lax.rsqrt(ms + NORM_EPS) * fn_ref[...]
    o_ref[0] = y


def _ffn_call(h, mod3, g_norm, wg, wu, wd, final_w, *, final_norm, tm):
    b, l, d = h.shape
    f = wg.shape[1]
    tm = min(tm, l)
    return pl.pallas_call(
        functools.partial(_ffn_kernel, final_norm=final_norm),
        grid=(b, l // tm),
        in_specs=[pl.BlockSpec((1, tm, d), lambda bi, i: (bi, i, 0)),
                  pl.BlockSpec((1, 3, d), lambda bi, i: (bi, 0, 0)),
                  _const_spec((1, d)), _const_spec((d, f)), _const_spec((d, f)), _const_spec((f, d)),
                  _const_spec((1, d))],
        out_specs=pl.BlockSpec((1, tm, d), lambda bi, i: (bi, i, 0)),
        out_shape=jax.ShapeDtypeStruct((b, l, d), F32),
        compiler_params=_params("parallel", "parallel"),
        name="ffn",
    )(h, mod3, g_norm.reshape(1, d), wg, wu, wd, final_w.reshape(1, d))


def _inproj_kernel(h_ref, mod_ref, gn_ref, wm_ref, wgz_ref, gw_ref, gb_ref, qw_ref, kw_ref,
                   bdq_ref, bdk_ref, cos_ref, sa_ref, sb_ref,
                   qk_ref, v_ref, r_ref, lg_ref, u_ref, aq_ref, ak_ref, av_ref, *, rope):
    h = h_ref[0]
    n = _rms_mod(h, gn_ref[...], mod_ref[0, 0:1], mod_ref[0, 1:2]).astype(BF16)
    z = _dot(n, wm_ref[...])
    o = 0
    q = z[:, o:o + GLA_QK] * (GLA_DK ** -0.5); o += GLA_QK
    k = z[:, o:o + GLA_QK]; o += GLA_QK
    qk_ref[0, :, 0:GLA_QK] = q
    qk_ref[0, :, GLA_QK:2 * GLA_QK] = k
    v_ref[0] = z[:, o:o + GLA_V]; o += GLA_V
    r_ref[0] = z[:, o:o + GLA_V]; o += GLA_V
    u_ref[0] = z[:, o:o + S5_WIDTH]; o += S5_WIDTH
    xq = z[:, o:o + ATT_Q]; o += ATT_Q
    xk = z[:, o:o + ATT_KV]; o += ATT_KV
    av_ref[0] = z[:, o:o + ATT_KV].astype(BF16)

    zg = _dot(n, wgz_ref[...]).astype(BF16)
    lg_ref[0] = _log_sigmoid(_dot(zg, gw_ref[...]) + gb_ref[...]) * (1.0 / GLA_TAU)

    def head_norm(x, bd_ref, w_ref):
        ss = _dot_hi_lo(x * x, bd_ref[...])
        return x * lax.rsqrt(ss * (1.0 / ATT_HEAD_DIM) + NORM_EPS) * w_ref[...]

    def rotary(x, width):
        if not rope:
            return x
        c, sa, sb = cos_ref[:, 0:width], sa_ref[:, 0:width], sb_ref[:, 0:width]
        return (x * c + pltpu.roll(x, width - ROPE_PAIRS, 1) * sa + pltpu.roll(x, ROPE_PAIRS, 1) * sb)

    aq_ref[0] = rotary(head_norm(xq, bdq_ref, qw_ref), ATT_Q).astype(BF16)
    ak_ref[0] = rotary(head_norm(xk, bdk_ref, kw_ref), ATT_KV).astype(BF16)


def _inproj_call(h, mod2, g_norm, lw, tables, *, rope, tm):
    b, l, d = h.shape
    tm = min(tm, l)
    tok = lambda w: pl.BlockSpec((1, tm, w), lambda bi, i: (bi, i, 0))
    tab = pl.BlockSpec((tm, ATT_Q), lambda bi, i: (i, 0))
    wm = lw["w_main"]
    outs = [(2 * GLA_QK, F32), (GLA_V, F32), (GLA_V, F32), (2 * GLA_QK, F32), (S5_WIDTH, F32),
            (ATT_Q, BF16), (ATT_KV, BF16), (ATT_KV, BF16)]
    return pl.pallas_call(
        functools.partial(_inproj_kernel, rope=rope),
        grid=(b, l // tm),
        in_specs=[tok(d), pl.BlockSpec((1, 2, d), lambda bi, i: (bi, 0, 0)), _const_spec((1, d)),
                  _const_spec(wm.shape), _const_spec((d, LANES)), _const_spec((LANES, 2 * GLA_QK)),
                  _const_spec((1, 2 * GLA_QK)), _const_spec((1, ATT_Q)), _const_spec((1, ATT_KV)),
                  _const_spec((ATT_Q, ATT_Q)), _const_spec((ATT_KV, ATT_KV)), tab, tab, tab],
        out_specs=[tok(w) for w, _ in outs],
        out_shape=[jax.ShapeDtypeStruct((b, l, w), dt) for w, dt in outs],
        compiler_params=_params("parallel", "parallel"),
        name="inproj",
    )(h, mod2, g_norm.reshape(1, d), wm, lw["w_gz"], lw["gate_w_bd"], lw["gate_b"], lw["q_norm_w"],
      lw["k_norm_w"], lw["bd_q"], lw["bd_k"], *tables)


def _gla_kernel(qk_ref, v_ref, lg_ref, s0_ref, tri_ref, of_ref, r_ref, nw_ref, o_ref, sfin_ref, s_scr,
                *, reverse, readout, n_chunks):
    i = pl.program_id(1)

    @pl.when(i == 0)
    def _():
        s_scr[...] = s0_ref[0]

    tri = tri_ref[...]
    c = GLA_CHUNK
    row = lax.broadcasted_iota(jnp.int32, (c, c), 0)
    col = lax.broadcasted_iota(jnp.int32, (c, c), 1)
    keep = (col >= row) if reverse else (col <= row)
    lane = lax.broadcasted_iota(jnp.int32, (c, LANES), 1)
    low = lane < GLA_DK
    order = range(n_chunks - 1, -1, -1) if reverse else range(n_chunks)
    for ci in order:
        rows = slice(ci * c, (ci + 1) * c)
        g = lg_ref[0, rows, :]
        g_hi = g.astype(BF16)
        b = _dot(tri, g_hi) + _dot(tri, (g - g_hi.astype(F32)).astype(BF16))
        b_last = b[0:1, :] if reverse else b[c - 1:c, :]
        q_in = qk_ref[0, rows, 0:GLA_QK] * jnp.exp(b)
        k = qk_ref[0, rows, GLA_QK:2 * GLA_QK]
        k_in = (k * jnp.exp(-b)).astype(BF16)
        k_out = (k * jnp.exp(b_last - b)).astype(BF16)
        dec = jnp.exp(jnp.broadcast_to(b_last, (LANES, GLA_QK)).T)
        for pair in range(GLA_HEADS // 2):
            lanes = slice(pair * LANES, (pair + 1) * LANES)
            s_pair = s_scr[lanes, :]
            s_pair_b = s_pair.astype(BF16)
            qp, kp_in, kp_out = q_in[:, lanes], k_in[:, lanes], k_out[:, lanes]
            ds = []
            for e in range(2):
                hd = 2 * pair + e
                vcols = slice(hd * GLA_DV, (hd + 1) * GLA_DV)
                qm = jnp.where(low if e == 0 else jnp.logical_not(low), qp, 0.0).astype(BF16)
                v_h = v_ref[0, rows, vcols].astype(BF16)
                att = _dotg(qm, kp_in, (1, 1))
                att = jnp.where(keep, att, 0.0).astype(BF16)
                o_h = _dot(att, v_h) + _dot(qm, s_pair_b)
                ds.append(_dotg(kp_out, v_h, (0, 0)))
                if readout:
                    o_h = o_h + of_ref[0, rows, vcols]
                    ms = jnp.mean(o_h * o_h, axis=-1, keepdims=True)
                    y = o_h * lax.rsqrt(ms + NORM_EPS) * nw_ref[...]
                    rr = r_ref[0, rows, vcols]
                    o_h = y * (rr * jax.nn.sigmoid(rr))
                o_ref[0, rows, vcols] = o_h
            d_state = jnp.concatenate([ds[0][0:GLA_DK], ds[1][GLA_DK:2 * GLA_DK]], axis=0)
            s_scr[lanes, :] = dec[lanes, :] * s_pair + d_state

    @pl.when(i == pl.num_programs(1) - 1)
    def _():
        sfin_ref[0] = s_scr[...]


def _gla_call(qk, v, lg, s0, o_f, r, norm_w, *, reverse, readout, tb):
    b, l, _ = v.shape
    tb = min(tb, l)
    nb = l // tb
    idx = (lambda bi, i: (bi, nb - 1 - i, 0)) if reverse else (lambda bi, i: (bi, i, 0))
    tok = lambda w: pl.BlockSpec((1, tb, w), idx)
    lg_off = 1 if reverse else 0
    lg_spec = pl.BlockSpec((1, tb, GLA_QK), (lambda bi, i: (bi, nb - 1 - i, lg_off)) if reverse
                           else (lambda bi, i: (bi, i, lg_off)))
    state = pl.BlockSpec((1, GLA_QK, GLA_DV), lambda bi, i: (bi, 0, 0))
    ones = np.triu(np.ones((GLA_CHUNK, GLA_CHUNK), np.float32)) if reverse else \
        np.tril(np.ones((GLA_CHUNK, GLA_CHUNK), np.float32))
    return pl.pallas_call(
        functools.partial(_gla_kernel, reverse=reverse, readout=readout, n_chunks=tb // GLA_CHUNK),
        grid=(b, nb),
        in_specs=[tok(2 * GLA_QK), tok(GLA_V), lg_spec, state, _const_spec((GLA_CHUNK, GLA_CHUNK)),
                  tok(GLA_V), tok(GLA_V), _const_spec((1, GLA_DV))],
        out_specs=[tok(GLA_V), state],
        out_shape=[jax.ShapeDtypeStruct((b, l, GLA_V), F32), jax.ShapeDtypeStruct((b, GLA_QK, GLA_DV), F32)],
        scratch_shapes=[pltpu.VMEM((GLA_QK, GLA_DV), F32)],
        compiler_params=_params("parallel", "arbitrary"),
        name="gla_bwd" if reverse else "gla_fwd",
    )(qk, v, lg, s0, jnp.asarray(ones, BF16), o_f, r, norm_w.reshape(1, GLA_DV))


def _gla_bidir(qk, v, lg, r, norm_w, s0_f, s0_b, *, tb):
    o_f, s_f = _gla_call(qk, v, lg, s0_f, v, r, norm_w, reverse=False, readout=False, tb=tb)
    y, s_b = _gla_call(qk, v, lg, s0_b, o_f, r, norm_w, reverse=True, readout=True, tb=tb)
    return y, s_f, s_b


def _s5_operators(a_re, a_im, log_dt, b_re, b_im, c_re, c_im):
    hp = lax.Precision.HIGHEST
    t = S5_T
    cr, ci_ = c_re.astype(F32), c_im.astype(F32)
    ks = jnp.arange(t + 1, dtype=F32)[:, None, None]
    per_dir = []
    for d in range(2):
        dt = jnp.exp(log_dt[d].astype(F32))[:, None]
        ar, ai = a_re[d].astype(F32), a_im[d].astype(F32)
        mag = jnp.exp(ks * dt * ar)
        pw_re, pw_im = mag * jnp.cos(ks * dt * ai), mag * jnp.sin(ks * dt * ai)
        den = ar * ar + ai * ai
        xr, xi = pw_re[1] - 1.0, pw_im[1]
        coef_re = (xr * ar + xi * ai) / den
        coef_im = (xi * ar - xr * ai) / den
        br, bi = b_re.astype(F32), b_im.astype(F32)
        bb_re = coef_re[..., None] * br - coef_im[..., None] * bi
        bb_im = coef_re[..., None] * bi + coef_im[..., None] * br
        ca_re = cr[None] * pw_re[:, :, None, :] - ci_[None] * pw_im[:, :, None, :]
        ca_im = cr[None] * pw_im[:, :, None, :] + ci_[None] * pw_re[:, :, None, :]
        kk = (jnp.einsum("kgop,gpi->kgoi", ca_re, bb_re, precision=hp)
              - jnp.einsum("kgop,gpi->kgoi", ca_im, bb_im, precision=hp))
        s_idx = jnp.arange(t)[:, None]
        t_idx = jnp.arange(t)[None, :]
        lag = (t_idx - s_idx) if d == 0 else (s_idx - t_idx)
        toe = jnp.where((lag >= 0)[:, :, None, None, None], kk[jnp.clip(lag, 0, t)], 0.0)
        m = jnp.transpose(toe, (2, 0, 4, 1, 3)).reshape(S5_GROUPS, S5_CW, S5_CW)
        p_pow = (t - 1 - jnp.arange(t)) if d == 0 else jnp.arange(t)
        pr, pi = pw_re[p_pow], pw_im[p_pow]
        p_re = pr[:, :, :, None] * bb_re[None] - pi[:, :, :, None] * bb_im[None]
        p_im = pr[:, :, :, None] * bb_im[None] + pi[:, :, :, None] * bb_re[None]
        flat_p = lambda x: jnp.transpose(x, (1, 0, 3, 2)).reshape(S5_GROUPS, S5_CW, S5_STATE)
        q_pow = (jnp.arange(t) + 1) if d == 0 else (t - jnp.arange(t))
        q_re, q_im = ca_re[q_pow], -ca_im[q_pow]
        flat_q = lambda x: jnp.transpose(x, (1, 3, 0, 2)).reshape(S5_GROUPS, S5_STATE, S5_CW)
        per_dir.append(dict(m=m, p_re=flat_p(p_re), p_im=flat_p(p_im), q_re=flat_q(q_re), q_im=flat_q(q_im),
                            at_re=pw_re[t], at_im=pw_im[t]))
    f, bk = per_dir
    zq = jnp.zeros_like(f["q_re"])
    p_cat = jnp.concatenate([f["p_re"], bk["p_re"], f["p_im"], bk["p_im"]], axis=-1)
    w = jnp.concatenate([f["m"] + bk["m"],
                         f["q_re"], zq, f["q_im"], zq,
                         zq, bk["q_re"], zq, bk["q_im"]], axis=1)
    at_re = jnp.concatenate([f["at_re"], bk["at_re"]], axis=-1)
    at_im = jnp.concatenate([f["at_im"], bk["at_im"]], axis=-1)
    return p_cat.astype(BF16), w.astype(BF16), at_re, at_im


def _s5_x_kernel(u_ref, p_ref, x_ref):
    x_ref[...] = _dot(u_ref[0], p_ref[0])


def _s5_x_call(uf, p_cat):
    g, r, cw = uf.shape
    return pl.pallas_call(
        _s5_x_kernel,
        grid=(g,),
        in_specs=[pl.BlockSpec((1, r, cw), lambda gi: (gi, 0, 0)),
                  pl.BlockSpec((1, cw, cw), lambda gi: (gi, 0, 0))],
        out_specs=pl.BlockSpec((r, cw), lambda gi: (0, gi)),
        out_shape=jax.ShapeDtypeStruct((r, g * cw), F32),
        compiler_params=_params("parallel"),
        name="s5_x",
    )(uf, p_cat)


def _s5_scan_kernel(xf_ref, xb_ref, are_ref, aim_ref, h0_ref, hf_ref, hb_ref, hfin_ref, hr_scr, hi_scr, *, nb):
    i = pl.program_id(0)

    @pl.when(i == 0)
    def _():
        hr_scr[...] = h0_ref[:, 0]
        hi_scr[...] = h0_ref[:, 1]

    ar, ai = are_ref[...][None], aim_ref[...][None]
    fwd = lax.broadcasted_iota(jnp.int32, (1, S5_GROUPS, LANES), 2) < S5_STATE

    def step(j, carry):
        hr, hi = carry
        jb = nb - 1 - j
        xr = jnp.where(fwd, xf_ref[:, j, :, 0:LANES], xb_ref[:, jb, :, 0:LANES])
        xi = jnp.where(fwd, xf_ref[:, j, :, LANES:2 * LANES], xb_ref[:, jb, :, LANES:2 * LANES])
        hf_ref[:, j, :, 0:LANES] = hr
        hf_ref[:, j, :, LANES:2 * LANES] = hi
        hb_ref[:, jb, :, 0:LANES] = hr
        hb_ref[:, jb, :, LANES:2 * LANES] = hi
        return ar * hr - ai * hi + xr, ar * hi + ai * hr + xi

    hr, hi = lax.fori_loop(0, nb, step, (hr_scr[...], hi_scr[...]))
    hr_scr[...] = hr
    hi_scr[...] = hi

    @pl.when(i == pl.num_programs(0) - 1)
    def _():
        hfin_ref[:, 0] = hr
        hfin_ref[:, 1] = hi


def _s5_scan_call(x4, at_re, at_im, h0, *, nb):
    b, n, g, w = x4.shape
    nb = min(nb, n)
    steps = n // nb
    blk = lambda rev: pl.BlockSpec((b, nb, g, w), (lambda i: (0, steps - 1 - i, 0, 0)) if rev
                                   else (lambda i: (0, i, 0, 0)))
    st = pl.BlockSpec((b, 2, g, LANES), lambda i: (0, 0, 0, 0))
    return pl.pallas_call(
        functools.partial(_s5_scan_kernel, nb=nb),
        grid=(steps,),
        in_specs=[blk(False), blk(True), _const_spec((g, LANES)), _const_spec((g, LANES)), st],
        out_specs=[blk(False), blk(True), st],
        out_shape=[jax.ShapeDtypeStruct(x4.shape, F32), jax.ShapeDtypeStruct(x4.shape, F32),
                   jax.ShapeDtypeStruct((b, 2, g, LANES), F32)],
        scratch_shapes=[pltpu.VMEM((b, g, LANES), F32), pltpu.VMEM((b, g, LANES), F32)],
        compiler_params=_params("arbitrary"),
        name="s5_scan",
    )(x4, x4, at_re, at_im, h0)


def _s5_y_kernel(u_ref, hf_ref, hb_ref, w_ref, y_ref):
    cw = S5_CW
    y_ref[0] = (_dot(u_ref[0], w_ref[0, 0:cw]) + _dot(hf_ref[...].astype(BF16), w_ref[0, cw:2 * cw])
                + _dot(hb_ref[...].astype(BF16), w_ref[0, 2 * cw:3 * cw]))


def _s5_y_call(uf, hf2, hb2, w):
    g, r, cw = uf.shape
    return pl.pallas_call(
        _s5_y_kernel,
        grid=(g,),
        in_specs=[pl.BlockSpec((1, r, cw), lambda gi: (gi, 0, 0)),
                  pl.BlockSpec((r, cw), lambda gi: (0, gi)),
                  pl.BlockSpec((r, cw), lambda gi: (0, gi)),
                  pl.BlockSpec((1, 3 * cw, cw), lambda gi: (gi, 0, 0))],
        out_specs=pl.BlockSpec((1, r, cw), lambda gi: (gi, 0, 0)),
        out_shape=jax.ShapeDtypeStruct((g, r, cw), F32),
        compiler_params=_params("parallel"),
        name="s5_y",
    )(uf, hf2, hb2, w)


def _s5_mix(u, ops, h0, *, nb):
    p_cat, w, at_re, at_im = ops
    b, l, _ = u.shape
    n = l // S5_T
    uf = u.astype(BF16).reshape(b, n, S5_T, S5_GROUPS, S5_GROUP)
    uf = jnp.transpose(uf, (3, 0, 1, 2, 4)).reshape(S5_GROUPS, b * n, S5_CW)
    x = _s5_x_call(uf, p_cat)
    hf, hb, hfin = _s5_scan_call(x.reshape(b, n, S5_GROUPS, S5_CW), at_re, at_im, h0, nb=nb)
    y = _s5_y_call(uf, hf.reshape(b * n, S5_GROUPS * S5_CW), hb.reshape(b * n, S5_GROUPS * S5_CW), w)
    y = y.reshape(S5_GROUPS, b, n, S5_T, S5_GROUP)
    y = jnp.transpose(y, (1, 2, 3, 0, 4)).reshape(b, l, S5_WIDTH)
    return y, hfin


def _attn_kernel(q_ref, kt_ref, v_ref, o_ref, m_scr, acc_scr, s0_scr, s1_scr, *, tk, n_kv):
    rows = q_ref.shape[2] * q_ref.shape[3]
    q = q_ref[0, 0].reshape(rows, LANES)
    m_scr[...] = jnp.full(m_scr.shape, -jnp.inf, F32)
    acc_scr[...] = jnp.zeros(acc_scr.shape, F32)

    def scores(j, s_scr):
        off = pl.multiple_of(j * tk, tk)
        s_scr[...] = _dot(q, kt_ref[0, :, pl.ds(off, tk)])

    def softmax_pv(j, s_scr):
        off = pl.multiple_of(j * tk, tk)
        s = s_scr[...]
        m_prev = m_scr[...]
        m_new = jnp.maximum(m_prev, jnp.max(s, axis=1, keepdims=True))
        alpha = jnp.exp2(m_prev - m_new)
        p = jnp.exp2(s - jnp.tile(m_new, (1, tk // LANES))).astype(BF16)
        acc_scr[...] = acc_scr[...] * jnp.tile(alpha, (1, 2)) + _dot(p, v_ref[0, pl.ds(off, tk), :])
        m_scr[...] = m_new

    scores(0, s0_scr)

    def pair(t, carry):
        j = 2 * t
        scores(j + 1, s1_scr)
        softmax_pv(j, s0_scr)
        scores(j + 2, s0_scr)
        softmax_pv(j + 1, s1_scr)
        return carry

    lax.fori_loop(0, (n_kv - 1) // 2, pair, 0)
    if n_kv % 2 == 0:
        scores(n_kv - 1, s1_scr)
        softmax_pv(n_kv - 2, s0_scr)
        softmax_pv(n_kv - 1, s1_scr)
    else:
        softmax_pv(n_kv - 1, s0_scr)
    acc = acc_scr[...]
    out = acc[:, 0:LANES] / acc[:, LANES:2 * LANES]
    o_ref[0, 0] = out.reshape(q_ref.shape[2:]).astype(o_ref.dtype)


def _attn_call(qs, kt, v_ext, *, tq, tk):
    b, kvh, grp, l, _ = qs.shape
    lk = kt.shape[2]
    tq, tk = min(tq, l), min(tk, lk)
    rows = grp * tq
    blk = pl.BlockSpec((1, 1, grp, tq, LANES), lambda bi, ki, i: (bi, ki, 0, i, 0))
    return pl.pallas_call(
        functools.partial(_attn_kernel, tk=tk, n_kv=lk // tk),
        grid=(b, kvh, l // tq),
        in_specs=[blk,
                  pl.BlockSpec((1, LANES, lk), lambda bi, ki, i: (bi, 0, 0), pipeline_mode=pl.Buffered(1)),
                  pl.BlockSpec((1, lk, 2 * LANES), lambda bi, ki, i: (bi, 0, 0), pipeline_mode=pl.Buffered(1))],
        out_specs=blk,
        out_shape=jax.ShapeDtypeStruct(qs.shape, BF16),
        scratch_shapes=[pltpu.VMEM((rows, LANES), F32), pltpu.VMEM((rows, 2 * LANES), F32),
                        pltpu.VMEM((rows, tk), F32), pltpu.VMEM((rows, tk), F32)],
        compiler_params=_params("parallel", "parallel", "arbitrary"),
        name="attention",
    )(qs, kt, v_ext)


def _stack_q(aq):
    b, l, _ = aq.shape
    q = aq.reshape(b, l, ATT_KV_HEADS, ATT_GROUP, ATT_HEAD_DIM)
    q = jnp.transpose(q, (0, 2, 3, 1, 4))
    z = jnp.zeros_like(q[:, 0])
    return jnp.stack([jnp.concatenate([q[:, 0], z], -1), jnp.concatenate([z, q[:, 1]], -1)], axis=1)


def _unstack_o(o):
    b, _, _, l, _ = o.shape
    half = ATT_GROUP * ATT_HEAD_DIM
    kv0 = jnp.transpose(o[:, 0, :, :, 0:ATT_HEAD_DIM], (0, 2, 1, 3)).reshape(b, l, half)
    kv1 = jnp.transpose(o[:, 1, :, :, ATT_HEAD_DIM:], (0, 2, 1, 3)).reshape(b, l, half)
    return jnp.concatenate([kv0, kv1], axis=-1)


def _kv_layout(ak, av):
    kt = jnp.transpose(ak, (0, 2, 1))
    v_ext = jnp.concatenate([av, jnp.ones_like(av)], axis=-1)
    return kt, v_ext


def _merge_kernel(h_ref, mod_ref, gn_ref, yg_ref, ys_ref, u_ref, ya_ref, wbg_ref, bbg_ref, wp_ref, wo_ref,
                  d_ref, gw_ref, gb_ref, o_ref):
    h = h_ref[0]
    d = h.shape[-1]
    n = _rms_mod(h, gn_ref[...], mod_ref[0, 0:1], mod_ref[0, 1:2]).astype(BF16)
    g = jax.nn.sigmoid(_dot(n, wbg_ref[...]) + bbg_ref[...])
    y = jax.nn.gelu(ys_ref[0] + d_ref[...] * u_ref[0])
    y_s5 = y * jax.nn.sigmoid(_dot(y.astype(BF16), gw_ref[...]) + gb_ref[...])
    m = (g[:, 0:d] * _dot(yg_ref[0].astype(BF16), wp_ref[0])
         + g[:, d:2 * d] * _dot(y_s5.astype(BF16), wp_ref[1])
         + g[:, 2 * d:3 * d] * _dot(ya_ref[0], wp_ref[2]))
    o_ref[0] = h + mod_ref[0, 2:3] * _dot(m.astype(BF16), wo_ref[...])


def _merge_call(h, mod3, g_norm, y_gla, y_s5raw, u, y_att, lw, *, tm):
    b, l, d = h.shape
    tm = min(tm, l)
    tok = lambda w: pl.BlockSpec((1, tm, w), lambda bi, i: (bi, i, 0))
    bw = y_gla.shape[-1]
    return pl.pallas_call(
        _merge_kernel,
        grid=(b, l // tm),
        in_specs=[tok(d), pl.BlockSpec((1, 3, d), lambda bi, i: (bi, 0, 0)), _const_spec((1, d)),
                  tok(bw), tok(bw), tok(bw), tok(bw),
                  _const_spec((d, 3 * d)), _const_spec((1, 3 * d)), _const_spec((3, bw, d)), _const_spec((d, d)),
                  _const_spec((1, bw)), _const_spec((bw, bw)), _const_spec((1, bw))],
        out_specs=tok(d),
        out_shape=jax.ShapeDtypeStruct((b, l, d), F32),
        compiler_params=_params("parallel", "parallel"),
        name="merge",
    )(h, mod3, g_norm.reshape(1, d), y_gla, y_s5raw, u, y_att, lw["w_bgate"], lw["b_bgate"], lw["w_bproj"],
      lw["w_out"], lw["s5_d"], lw["glu_w"], lw["glu_b"])


def _rope_tables(n_tokens):
    rows = n_tokens // GRID_W
    row = jnp.repeat(jnp.arange(rows, dtype=F32), GRID_W)
    col = jnp.tile(jnp.arange(GRID_W, dtype=F32), rows)
    inv = ROPE_THETA ** (-jnp.arange(ROPE_PAIRS, dtype=F32) / ROPE_PAIRS)
    ang_r, ang_c = row[:, None] * inv, col[:, None] * inv
    zero = jnp.zeros_like(ang_r)
    cos = jnp.concatenate([jnp.cos(ang_r), jnp.cos(ang_r), jnp.cos(ang_c), jnp.cos(ang_c)], axis=-1)
    sin_a = jnp.concatenate([-jnp.sin(ang_r), zero, -jnp.sin(ang_c), zero], axis=-1)
    sin_b = jnp.concatenate([zero, jnp.sin(ang_r), zero, jnp.sin(ang_c)], axis=-1)
    tile = lambda t: jnp.tile(t, (1, ATT_HEADS))
    return tile(cos), tile(sin_a), tile(sin_b)


def _block_diag_ones(width, seg):
    idx = np.arange(width) // seg
    return jnp.asarray((idx[:, None] == idx[None, :]).astype(np.float32), BF16)


def _layer_weights(i, w_in, gla_gate_w, gla_gate_b, attn_q_norm_w, attn_k_norm_w, w_branch_gate,
                   b_branch_gate, w_branch_proj, w_out, s5_d, s5_glu_w, s5_glu_b):
    offs = np.concatenate([[0], np.cumsum(IN_WIDTHS)])
    col = lambda k: w_in[i][:, offs[k]:offs[k + 1]]
    w_main = jnp.concatenate([col(0), col(1), col(2), col(5), col(6), col(7), col(8), col(9)], axis=1)
    d = w_in.shape[1]
    rk = GLA_GATE_RANK
    w_gz = jnp.zeros((d, LANES), F32).at[:, 0:rk].set(col(3)).at[:, rk:2 * rk].set(col(4))
    gw = jnp.zeros((LANES, 2 * GLA_QK), F32)
    gw = gw.at[0:rk, 0:GLA_QK].set(gla_gate_w[i, 0]).at[rk:2 * rk, GLA_QK:].set(gla_gate_w[i, 1])
    return dict(
        w_main=w_main.astype(BF16), w_gz=w_gz.astype(BF16), gate_w_bd=gw.astype(BF16),
        gate_b=gla_gate_b[i].reshape(1, 2 * GLA_QK),
        q_norm_w=(jnp.tile(attn_q_norm_w[i], ATT_HEADS) * (ATT_HEAD_DIM ** -0.5 * LOG2_E)).reshape(1, ATT_Q),
        k_norm_w=jnp.tile(attn_k_norm_w[i], ATT_KV_HEADS).reshape(1, ATT_KV),
        bd_q=_block_diag_ones(ATT_Q, ATT_HEAD_DIM), bd_k=_block_diag_ones(ATT_KV, ATT_HEAD_DIM),
        w_bgate=w_branch_gate[i].astype(BF16), b_bgate=b_branch_gate[i].reshape(1, -1),
        w_bproj=w_branch_proj[i].astype(BF16), w_out=w_out[i].astype(BF16),
        s5_d=s5_d[i].reshape(1, -1), glu_w=s5_glu_w[i].astype(BF16), glu_b=s5_glu_b[i].reshape(1, -1),
    )


def _mixer_inputs(h, mod, g_norm, lw, tables, *, rope, tm):
    mod2 = mod[:, 3:5]
    return _inproj_call(h, mod2, g_norm, lw, tables, rope=rope, tm=tm)


def kernel(x, c, ctx, c_ctx, w_ada, b_ada, norm_w, w_ffn_gate, w_ffn_up, w_ffn_down, w_in, gla_gate_w,
           gla_gate_b, gla_norm_w, s5_a_re, s5_a_im, s5_log_dt, s5_b_re, s5_b_im, s5_c_re, s5_c_im, s5_d,
           s5_glu_w, s5_glu_b, attn_q_norm_w, attn_k_norm_w, w_branch_gate, b_branch_gate, w_branch_proj,
           w_out, final_norm_w):
    bsz, seq, d = x.shape
    depth = w_ada.shape[0]
    tables = _rope_tables(seq)
    ctx_tables = tuple(t[0:ctx.shape[1]] for t in tables)
    cond = jnp.zeros((8, d), F32).at[0:bsz].set(c).at[bsz].set(c_ctx)
    tm_ffn, tm_in, tm_merge, tb_gla, nb_s5, tq, tk = (TILES[k] for k in
                                                      ("ffn", "inproj", "merge", "gla", "s5_scan", "att_q", "att_k"))

    h_lat, h_ctx = x, ctx
    for i in range(depth):
        last = i == depth - 1
        mods = _ada_call(cond, w_ada[i], b_ada[i])
        mod_l = mods[0:bsz].reshape(bsz, N_MOD, d)
        mod_c = jnp.broadcast_to(mods[bsz].reshape(1, N_MOD, d), (bsz, N_MOD, d))
        lw = _layer_weights(i, w_in, gla_gate_w, gla_gate_b, attn_q_norm_w, attn_k_norm_w, w_branch_gate,
                            b_branch_gate, w_branch_proj, w_out, s5_d, s5_glu_w, s5_glu_b)
        ffn_w = [(w_ffn_gate[i, j].astype(BF16), w_ffn_up[i, j].astype(BF16), w_ffn_down[i, j].astype(BF16))
                 for j in range(2)]

        def ffn(h, mod, j, fin=False):
            return _ffn_call(h, mod[:, 6 * j:6 * j + 3], norm_w[i, 2 * j], *ffn_w[j], final_norm_w,
                             final_norm=fin, tm=tm_ffn)

        h_lat = ffn(h_lat, mod_l, 0)
        h_ctx = ffn(h_ctx, mod_c, 0)

        qk_c, v_c, r_c, lg_c, u_c, aq_c, ak_c, av_c = _mixer_inputs(h_ctx, mod_c, norm_w[i, 1], lw, ctx_tables,
                                                                    rope=False, tm=tm_in)
        qk_l, v_l, r_l, lg_l, u_l, aq_l, ak_l, av_l = _mixer_inputs(h_lat, mod_l, norm_w[i, 1], lw, tables,
                                                                    rope=True, tm=tm_in)
        s0 = jnp.zeros((bsz, GLA_QK, GLA_DV), F32)
        yg_c, sf_c, sb_c = _gla_bidir(qk_c, v_c, lg_c, r_c, gla_norm_w[i], s0, s0, tb=tb_gla)
        yg_l, _, _ = _gla_bidir(qk_l, v_l, lg_l, r_l, gla_norm_w[i], sf_c, sb_c, tb=tb_gla)
        ops = _s5_operators(s5_a_re[i], s5_a_im[i], s5_log_dt[i], s5_b_re[i], s5_b_im[i], s5_c_re[i], s5_c_im[i])
        h0 = jnp.zeros((bsz, 2, S5_GROUPS, LANES), F32)
        ys_c, hfin_c = _s5_mix(u_c, ops, h0, nb=nb_s5)
        ys_l, _ = _s5_mix(u_l, ops, hfin_c, nb=nb_s5)
        kt, v_ext = _kv_layout(jnp.concatenate([ak_c, ak_l], axis=1), jnp.concatenate([av_c, av_l], axis=1))
        ya_l = _unstack_o(_attn_call(_stack_q(aq_l), kt, v_ext, tq=tq, tk=tk))
        h_lat = _merge_call(h_lat, mod_l[:, 3:6], norm_w[i, 1], yg_l, ys_l, u_l, ya_l, lw, tm=tm_merge)
        h_lat = ffn(h_lat, mod_l, 1, last)
        if not last:
            kt_c, v_ext_c = _kv_layout(ak_c, av_c)
            ya_c = _unstack_o(_attn_call(_stack_q(aq_c), kt_c, v_ext_c, tq=tq, tk=tk))
            h_ctx = _merge_call(h_ctx, mod_c[:, 3:6], norm_w[i, 1], yg_c, ys_c, u_c, ya_c, lw, tm=tm_merge)
            h_ctx = ffn(h_ctx, mod_c, 1)
    return h_lat
```

```python
import functools

import jax
import jax.numpy as jnp
import numpy as np
from jax import lax
from jax.experimental import pallas as pl
from jax.experimental.pallas import tpu as pltpu

F32 = jnp.float32
BF16 = jnp.bfloat16

N_MOD = 9
MACARON_WEIGHT = 0.5
NORM_EPS = 1e-6
GRID_W = 64
GLA_HEADS = 4
GLA_DK = 64
GLA_DV = 128
GLA_QK = GLA_HEADS * GLA_DK
GLA_V = GLA_HEADS * GLA_DV
GLA_GATE_RANK = 16
GLA_TAU = 16.0
GLA_CHUNK = 64
S5_WIDTH = 512
S5_GROUP = 16
S5_GROUPS = S5_WIDTH // S5_GROUP
S5_STATE = 64
ATT_HEADS = 8
ATT_KV_HEADS = 2
ATT_GROUP = ATT_HEADS // ATT_KV_HEADS
ATT_HEAD_DIM = 64
ATT_Q = ATT_HEADS * ATT_HEAD_DIM
ATT_KV = ATT_KV_HEADS * ATT_HEAD_DIM
ROPE_PAIRS = ATT_HEAD_DIM // 4
ROPE_THETA = 10000.0
IN_WIDTHS = (GLA_QK, GLA_QK, GLA_V, GLA_GATE_RANK, GLA_GATE_RANK, GLA_V, S5_WIDTH, ATT_Q, ATT_KV, ATT_KV)

LANES = 128
MXU_DIM = 256
VMEM_LIMIT = 56 * 1024 * 1024

LOG2_E = 1.4426950408889634

S5_T = MXU_DIM // S5_GROUP
S5_CW = S5_T * S5_GROUP
S5_GPB = LANES // S5_GROUP
S5_SG = S5_GROUPS // S5_GPB
S5_UW = S5_T * LANES

TILES = dict(ffn=512, inproj=512, merge=512, gla=512, s5_scan=32, s5_rows=256, att_q=128, att_k=1280)


def _params(*sem):
    return pltpu.CompilerParams(dimension_semantics=sem, vmem_limit_bytes=VMEM_LIMIT)


def _const_spec(shape):
    nd = len(shape)
    return pl.BlockSpec(shape, lambda *_: (0,) * nd, pipeline_mode=pl.Buffered(1))


def _dotg(a, b, contract):
    return lax.dot_general(a, b, (((contract[0],), (contract[1],)), ((), ())),
                           precision=lax.Precision.DEFAULT, preferred_element_type=F32)


def _dot(a, b):
    return _dotg(a, b, (1, 0))


def _dot_hi_lo(x, w):
    hi = x.astype(BF16)
    lo = (x - hi.astype(F32)).astype(BF16)
    return _dot(hi, w) + _dot(lo, w)


def _rms_mod(h, g_norm, shift, scale):
    ms = jnp.mean(h * h, axis=-1, keepdims=True)
    y = h * lax.rsqrt(ms + NORM_EPS) * g_norm
    return y * (1.0 + scale) + shift


def _log_sigmoid(x):
    return jnp.minimum(x, 0.0) - jnp.log(1.0 + jnp.exp(-jnp.abs(x)))


def _ada_kernel(c_ref, w_ref, b_ref, o_ref):
    c = c_ref[...]
    sc = (c * jax.nn.sigmoid(c)).astype(BF16)
    o_ref[...] = _dot(sc, w_ref[...].astype(BF16)) + b_ref[...]


def _ada_call(cond, w, b):
    rows, d = cond.shape
    n = w.shape[1]
    tn = 9 * LANES
    return pl.pallas_call(
        _ada_kernel,
        grid=(n // tn,),
        in_specs=[pl.BlockSpec((rows, d), lambda j: (0, 0)),
                  pl.BlockSpec((d, tn), lambda j: (0, j)),
                  pl.BlockSpec((1, tn), lambda j: (0, j))],
        out_specs=pl.BlockSpec((rows, tn), lambda j: (0, j)),
        out_shape=jax.ShapeDtypeStruct((rows, n), F32),
        compiler_params=_params("arbitrary"),
        name="adaln",
    )(cond, w, b.reshape(1, n))


def _ffn_kernel(h_ref, mod_ref, gn_ref, wg_ref, wu_ref, wd_ref, fn_ref, o_ref, *, final_norm):
    h = h_ref[0]
    shift, scale, gate = mod_ref[0, 0:1], mod_ref[0, 1:2], mod_ref[0, 2:3]
    n = _rms_mod(h, gn_ref[...], shift, scale).astype(BF16)
    g = _dot(n, wg_ref[...])
    u = _dot(n, wu_ref[...])
    a = (g * jax.nn.sigmoid(g) * u).astype(BF16)
    y = h + (MACARON_WEIGHT * gate) * _dot(a, wd_ref[...])
    if final_norm:
        ms = jnp.mean(y * y, axis=-1, keepdims=True)
        y = y * lax.rsqrt(ms + NORM_EPS) * fn_ref[...]
    o_ref[0] = y


def _ffn_call(h, mod3, g_norm, wg, wu, wd, final_w, *, final_norm, tm):
    b, l, d = h.shape
    f = wg.shape[1]
    tm = min(tm, l)
    return pl.pallas_call(
        functools.partial(_ffn_kernel, final_norm=final_norm),
        grid=(b, l // tm),
        in_specs=[pl.BlockSpec((1, tm, d), lambda bi, i: (bi, i, 0)),
                  pl.BlockSpec((1, 3, d), lambda bi, i: (bi, 0, 0)),
                  _const_spec((1, d)), _const_spec((d, f)), _const_spec((d, f)), _const_spec((f, d)),
                  _const_spec((1, d))],
        out_specs=pl.BlockSpec((1, tm, d), lambda bi, i: (bi, i, 0)),
        out_shape=jax.ShapeDtypeStruct((b, l, d), F32),
        compiler_params=_params("parallel", "parallel"),
        name="ffn",
    )(h, mod3, g_norm.reshape(1, d), wg, wu, wd, final_w.reshape(1, d))


def _inproj_kernel(h_ref, mod_ref, gn_ref, wm_ref, wgz_ref, gw_ref, gb_ref, qw_ref, kw_ref,
                   bdq_ref, bdk_ref, cos_ref, sa_ref, sb_ref,
                   qk_ref, v_ref, r_ref, lg_ref, u_ref, uf_ref, aq_ref, ak_ref, av_ref, us_scr, *, rope):
    h = h_ref[0]
    tm = h.shape[0]
    n = _rms_mod(h, gn_ref[...], mod_ref[0, 0:1], mod_ref[0, 1:2]).astype(BF16)
    z = _dot(n, wm_ref[...])
    o = 0
    q = z[:, o:o + GLA_QK] * (GLA_DK ** -0.5); o += GLA_QK
    k = z[:, o:o + GLA_QK]; o += GLA_QK
    qk_ref[0, :, 0:GLA_QK] = q
    qk_ref[0, :, GLA_QK:2 * GLA_QK] = k
    v_ref[0] = z[:, o:o + GLA_V]; o += GLA_V
    r_ref[0] = z[:, o:o + GLA_V]; o += GLA_V
    u_ref[0] = z[:, o:o + S5_WIDTH]; o += S5_WIDTH
    for sg in range(S5_SG):
        us_scr[sg] = z[:, o - S5_WIDTH + sg * LANES:o - S5_WIDTH + (sg + 1) * LANES]
        for t in range(S5_T):
            piece = us_scr[sg, pl.ds(t, tm // S5_T, stride=S5_T), :]
            uf_ref[sg, :, t * LANES:(t + 1) * LANES] = piece.astype(BF16)
    xq = z[:, o:o + ATT_Q]; o += ATT_Q
    xk = z[:, o:o + ATT_KV]; o += ATT_KV
    av_ref[0] = z[:, o:o + ATT_KV].astype(BF16)

    zg = _dot(n, wgz_ref[...]).astype(BF16)
    lg_ref[0] = _log_sigmoid(_dot(zg, gw_ref[...]) + gb_ref[...]) * (1.0 / GLA_TAU)

    def head_norm(x, bd_ref, w_ref):
        ss = _dot_hi_lo(x * x, bd_ref[...])
        return x * lax.rsqrt(ss * (1.0 / ATT_HEAD_DIM) + NORM_EPS) * w_ref[...]

    def rotary(x, width):
        if not rope:
            return x
        c, sa, sb = cos_ref[:, 0:width], sa_ref[:, 0:width], sb_ref[:, 0:width]
        return (x * c + pltpu.roll(x, width - ROPE_PAIRS, 1) * sa + pltpu.roll(x, ROPE_PAIRS, 1) * sb)

    ak_ref[0] = rotary(head_norm(xk, bdk_ref, kw_ref), ATT_KV).astype(BF16)
    aq = rotary(head_norm(xq, bdq_ref, qw_ref), ATT_Q)
    lane = lax.broadcasted_iota(jnp.int32, (tm, LANES), 1)
    for kv in range(ATT_KV_HEADS):
        keep = (lane >= kv * ATT_HEAD_DIM) & (lane < (kv + 1) * ATT_HEAD_DIM)
        for j in range(ATT_GROUP):
            hd = kv * ATT_GROUP + j
            blk = aq[:, (hd // 2) * LANES:(hd // 2 + 1) * LANES]
            if hd % 2 != kv:
                blk = pltpu.roll(blk, ATT_HEAD_DIM, 1)
            aq_ref[0, kv, j] = jnp.where(keep, blk, 0.0).astype(BF16)


def _inproj_call(h, mod2, g_norm, lw, tables, *, rope, tm):
    b, l, d = h.shape
    tm = min(tm, l)
    tok = lambda w: pl.BlockSpec((1, tm, w), lambda bi, i: (bi, i, 0))
    tab = pl.BlockSpec((tm, ATT_Q), lambda bi, i: (i, 0))
    wm = lw["w_main"]
    nt = l // tm
    tok_out = lambda w, dt: (tok(w), jax.ShapeDtypeStruct((b, l, w), dt))
    outs = [tok_out(2 * GLA_QK, F32), tok_out(GLA_V, F32), tok_out(GLA_V, F32), tok_out(2 * GLA_QK, F32),
            tok_out(S5_WIDTH, F32),
            (pl.BlockSpec((S5_SG, tm // S5_T, S5_UW), lambda bi, i: (0, bi * nt + i, 0)),
             jax.ShapeDtypeStruct((S5_SG, b * l // S5_T, S5_UW), BF16)),
            (pl.BlockSpec((1, ATT_KV_HEADS, ATT_GROUP, tm, LANES), lambda bi, i: (bi, 0, 0, i, 0)),
             jax.ShapeDtypeStruct((b, ATT_KV_HEADS, ATT_GROUP, l, LANES), BF16)),
            tok_out(ATT_KV, BF16), tok_out(ATT_KV, BF16)]
    return pl.pallas_call(
        functools.partial(_inproj_kernel, rope=rope),
        grid=(b, nt),
        in_specs=[tok(d), pl.BlockSpec((1, 2, d), lambda bi, i: (bi, 0, 0)), _const_spec((1, d)),
                  _const_spec(wm.shape), _const_spec((d, LANES)), _const_spec((LANES, 2 * GLA_QK)),
                  _const_spec((1, 2 * GLA_QK)), _const_spec((1, ATT_Q)), _const_spec((1, ATT_KV)),
                  _const_spec((ATT_Q, ATT_Q)), _const_spec((ATT_KV, ATT_KV)), tab, tab, tab],
        out_specs=[s for s, _ in outs],
        out_shape=[t for _, t in outs],
        scratch_shapes=[pltpu.VMEM((S5_SG, tm, LANES), F32)],
        compiler_params=_params("parallel", "parallel"),
        name="inproj",
    )(h, mod2, g_norm.reshape(1, d), wm, lw["w_gz"], lw["gate_w_bd"], lw["gate_b"], lw["q_norm_w"],
      lw["k_norm_w"], lw["bd_q"], lw["bd_k"], *tables)


def _gla_kernel(qk_ref, v_ref, lg_ref, s0_ref, tri_ref, of_ref, r_ref, nw_ref, o_ref, sfin_ref, s_scr,
                *, reverse, readout, n_chunks):
    i = pl.program_id(1)

    @pl.when(i == 0)
    def _():
        s_scr[...] = s0_ref[0]

    tri = tri_ref[...]
    c = GLA_CHUNK
    row = lax.broadcasted_iota(jnp.int32, (c, c), 0)
    col = lax.broadcasted_iota(jnp.int32, (c, c), 1)
    keep = (col >= row) if reverse else (col <= row)
    lane = lax.broadcasted_iota(jnp.int32, (c, LANES), 1)
    low = lane < GLA_DK
    order = range(n_chunks - 1, -1, -1) if reverse else range(n_chunks)
    for ci in order:
        rows = slice(ci * c, (ci + 1) * c)
        g = lg_ref[0, rows, :]
        g_hi = g.astype(BF16)
        b = _dot(tri, g_hi) + _dot(tri, (g - g_hi.astype(F32)).astype(BF16))
        b_last = b[0:1, :] if reverse else b[c - 1:c, :]
        q_in = qk_ref[0, rows, 0:GLA_QK] * jnp.exp(b)
        k = qk_ref[0, rows, GLA_QK:2 * GLA_QK]
        k_in = (k * jnp.exp(-b)).astype(BF16)
        k_out = (k * jnp.exp(b_last - b)).astype(BF16)
        dec = jnp.exp(jnp.broadcast_to(b_last, (LANES, GLA_QK)).T)
        for pair in range(GLA_HEADS // 2):
            lanes = slice(pair * LANES, (pair + 1) * LANES)
            s_pair = s_scr[lanes, :]
            s_pair_b = s_pair.astype(BF16)
            qp, kp_in, kp_out = q_in[:, lanes], k_in[:, lanes], k_out[:, lanes]
            ds = []
            for e in range(2):
                hd = 2 * pair + e
                vcols = slice(hd * GLA_DV, (hd + 1) * GLA_DV)
                qm = jnp.where(low if e == 0 else jnp.logical_not(low), qp, 0.0).astype(BF16)
                v_h = v_ref[0, rows, vcols].astype(BF16)
                att = _dotg(qm, kp_in, (1, 1))
                att = jnp.where(keep, att, 0.0).astype(BF16)
                o_h = _dot(att, v_h) + _dot(qm, s_pair_b)
                ds.append(_dotg(kp_out, v_h, (0, 0)))
                if readout:
                    o_h = o_h + of_ref[0, rows, vcols]
                    ms = jnp.mean(o_h * o_h, axis=-1, keepdims=True)
                    y = o_h * lax.rsqrt(ms + NORM_EPS) * nw_ref[...]
                    rr = r_ref[0, rows, vcols]
                    o_h = y * (rr * jax.nn.sigmoid(rr))
                o_ref[0, rows, vcols] = o_h
            d_state = jnp.concatenate([ds[0][0:GLA_DK], ds[1][GLA_DK:2 * GLA_DK]], axis=0)
            s_scr[lanes, :] = dec[lanes, :] * s_pair + d_state

    @pl.when(i == pl.num_programs(1) - 1)
    def _():
        sfin_ref[0] = s_scr[...]


def _gla_call(qk, v, lg, s0, o_f, r, norm_w, *, reverse, readout, tb):
    b, l, _ = v.shape
    tb = min(tb, l)
    nb = l // tb
    idx = (lambda bi, i: (bi, nb - 1 - i, 0)) if reverse else (lambda bi, i: (bi, i, 0))
    tok = lambda w: pl.BlockSpec((1, tb, w), idx)
    lg_off = 1 if reverse else 0
    lg_spec = pl.BlockSpec((1, tb, GLA_QK), (lambda bi, i: (bi, nb - 1 - i, lg_off)) if reverse
                           else (lambda bi, i: (bi, i, lg_off)))
    state = pl.BlockSpec((1, GLA_QK, GLA_DV), lambda bi, i: (bi, 0, 0))
    ones = np.triu(np.ones((GLA_CHUNK, GLA_CHUNK), np.float32)) if reverse else \
        np.tril(np.ones((GLA_CHUNK, GLA_CHUNK), np.float32))
    return pl.pallas_call(
        functools.partial(_gla_kernel, reverse=reverse, readout=readout, n_chunks=tb // GLA_CHUNK),
        grid=(b, nb),
        in_specs=[tok(2 * GLA_QK), tok(GLA_V), lg_spec, state, _const_spec((GLA_CHUNK, GLA_CHUNK)),
                  tok(GLA_V), tok(GLA_V), _const_spec((1, GLA_DV))],
        out_specs=[tok(GLA_V), state],
        out_shape=[jax.ShapeDtypeStruct((b, l, GLA_V), F32), jax.ShapeDtypeStruct((b, GLA_QK, GLA_DV), F32)],
        scratch_shapes=[pltpu.VMEM((GLA_QK, GLA_DV), F32)],
        compiler_params=_params("parallel", "arbitrary"),
        name="gla_bwd" if reverse else "gla_fwd",
    )(qk, v, lg, s0, jnp.asarray(ones, BF16), o_f, r, norm_w.reshape(1, GLA_DV))


def _gla_bidir(qk, v, lg, r, norm_w, s0_f, s0_b, *, tb):
    o_f, s_f = _gla_call(qk, v, lg, s0_f, v, r, norm_w, reverse=False, readout=False, tb=tb)
    y, s_b = _gla_call(qk, v, lg, s0_b, o_f, r, norm_w, reverse=True, readout=True, tb=tb)
    return y, s_f, s_b


def _s5_operators(a_re, a_im, log_dt, b_re, b_im, c_re, c_im):
    hp = lax.Precision.HIGHEST
    t = S5_T
    cr, ci_ = c_re.astype(F32), c_im.astype(F32)
    ks = jnp.arange(t + 1, dtype=F32)[:, None, None]
    per_dir = []
    for d in range(2):
        dt = jnp.exp(log_dt[d].astype(F32))[:, None]
        ar, ai = a_re[d].astype(F32), a_im[d].astype(F32)
        mag = jnp.exp(ks * dt * ar)
        pw_re, pw_im = mag * jnp.cos(ks * dt * ai), mag * jnp.sin(ks * dt * ai)
        den = ar * ar + ai * ai
        xr, xi = pw_re[1] - 1.0, pw_im[1]
        coef_re = (xr * ar + xi * ai) / den
        coef_im = (xi * ar - xr * ai) / den
        br, bi = b_re.astype(F32), b_im.astype(F32)
        bb_re = coef_re[..., None] * br - coef_im[..., None] * bi
        bb_im = coef_re[..., None] * bi + coef_im[..., None] * br
        ca_re = cr[None] * pw_re[:, :, None, :] - ci_[None] * pw_im[:, :, None, :]
        ca_im = cr[None] * pw_im[:, :, None, :] + ci_[None] * pw_re[:, :, None, :]
        kk = (jnp.einsum("kgop,gpi->kgoi", ca_re, bb_re, precision=hp)
              - jnp.einsum("kgop,gpi->kgoi", ca_im, bb_im, precision=hp))
        s_idx = jnp.arange(t)[:, None]
        t_idx = jnp.arange(t)[None, :]
        lag = (t_idx - s_idx) if d == 0 else (s_idx - t_idx)
        toe = jnp.where((lag >= 0)[:, :, None, None, None], kk[jnp.clip(lag, 0, t)], 0.0)
        m = jnp.transpose(toe, (2, 0, 4, 1, 3)).reshape(S5_GROUPS, S5_CW, S5_CW)
        p_pow = (t - 1 - jnp.arange(t)) if d == 0 else jnp.arange(t)
        pr, pi = pw_re[p_pow], pw_im[p_pow]
        p_re = pr[:, :, :, None] * bb_re[None] - pi[:, :, :, None] * bb_im[None]
        p_im = pr[:, :, :, None] * bb_im[None] + pi[:, :, :, None] * bb_re[None]
        flat_p = lambda x: jnp.transpose(x, (1, 0, 3, 2)).reshape(S5_GROUPS, S5_CW, S5_STATE)
        q_pow = (jnp.arange(t) + 1) if d == 0 else (t - jnp.arange(t))
        q_re, q_im = ca_re[q_pow], -ca_im[q_pow]
        flat_q = lambda x: jnp.transpose(x, (1, 3, 0, 2)).reshape(S5_GROUPS, S5_STATE, S5_CW)
        per_dir.append(dict(m=m, p_re=flat_p(p_re), p_im=flat_p(p_im), q_re=flat_q(q_re), q_im=flat_q(q_im),
                            at_re=pw_re[t], at_im=pw_im[t]))
    f, bk = per_dir
    zq = jnp.zeros_like(f["q_re"])
    p_cat = jnp.concatenate([f["p_re"], bk["p_re"], f["p_im"], bk["p_im"]], axis=-1)
    q_f = jnp.concatenate([f["q_re"], zq, f["q_im"], zq], axis=1)
    q_b = jnp.concatenate([zq, bk["q_re"], zq, bk["q_im"]], axis=1)
    at_re = jnp.concatenate([f["at_re"], bk["at_re"]], axis=-1)
    at_im = jnp.concatenate([f["at_im"], bk["at_im"]], axis=-1)

    eye = jnp.eye(S5_GPB, dtype=F32)
    sg, gpb, t, gc = S5_SG, S5_GPB, S5_T, S5_GROUP

    def rows_tgc_cols_gk(a):
        k = a.shape[-1]
        a6 = a.reshape(sg, gpb, t, gc, k)
        out = a6[:, :, :, :, None, :] * eye[None, :, None, None, :, None]
        return jnp.transpose(out, (0, 2, 1, 3, 4, 5)).reshape(sg, S5_UW, gpb * k)

    def rows_tgc_cols_tgc(a):
        a6 = a.reshape(sg, gpb, t, gc, t, gc)
        out = a6[:, :, :, :, :, None, :] * eye[None, :, None, None, None, :, None]
        return jnp.transpose(out, (0, 2, 1, 3, 4, 5, 6)).reshape(sg, S5_UW, S5_UW)

    def rows_gk_cols_tgc(a):
        k = a.shape[1]
        a5 = a.reshape(sg, gpb, k, t, gc)
        out = a5[:, :, :, :, None, :] * eye[None, :, None, None, :, None]
        return out.reshape(sg, gpb * k, S5_UW)

    return (rows_tgc_cols_gk(p_cat).astype(BF16), rows_tgc_cols_tgc(f["m"] + bk["m"]).astype(BF16),
            rows_gk_cols_tgc(q_f).astype(BF16), rows_gk_cols_tgc(q_b).astype(BF16), at_re, at_im)


def _s5_x_kernel(u_ref, p_ref, x_ref):
    x_ref[...] = _dot(u_ref[0], p_ref[0])


def _s5_x_call(uf, p_blk, *, rt):
    sg, r, uw = uf.shape
    rt = min(rt, r)
    return pl.pallas_call(
        _s5_x_kernel,
        grid=(sg, r // rt),
        in_specs=[pl.BlockSpec((1, rt, uw), lambda si, i: (si, i, 0)),
                  pl.BlockSpec((1, uw, uw), lambda si, i: (si, 0, 0), pipeline_mode=pl.Buffered(1))],
        out_specs=pl.BlockSpec((rt, uw), lambda si, i: (i, si)),
        out_shape=jax.ShapeDtypeStruct((r, sg * uw), F32),
        compiler_params=_params("parallel", "parallel"),
        name="s5_x",
    )(uf, p_blk)


def _s5_scan_kernel(xf_ref, xb_ref, are_ref, aim_ref, h0_ref, hf_ref, hb_ref, hfin_ref, hr_scr, hi_scr, *, nb):
    i = pl.program_id(0)

    @pl.when(i == 0)
    def _():
        hr_scr[...] = h0_ref[:, 0]
        hi_scr[...] = h0_ref[:, 1]

    ar, ai = are_ref[...][None], aim_ref[...][None]
    fwd = lax.broadcasted_iota(jnp.int32, (1, S5_GROUPS, LANES), 2) < S5_STATE

    def step(j, carry):
        hr, hi = carry
        jb = nb - 1 - j
        xr = jnp.where(fwd, xf_ref[:, j, :, 0:LANES], xb_ref[:, jb, :, 0:LANES])
        xi = jnp.where(fwd, xf_ref[:, j, :, LANES:2 * LANES], xb_ref[:, jb, :, LANES:2 * LANES])
        hf_ref[:, j, :, 0:LANES] = hr
        hf_ref[:, j, :, LANES:2 * LANES] = hi
        hb_ref[:, jb, :, 0:LANES] = hr
        hb_ref[:, jb, :, LANES:2 * LANES] = hi
        return ar * hr - ai * hi + xr, ar * hi + ai * hr + xi

    hr, hi = lax.fori_loop(0, nb, step, (hr_scr[...], hi_scr[...]))
    hr_scr[...] = hr
    hi_scr[...] = hi

    @pl.when(i == pl.num_programs(0) - 1)
    def _():
        hfin_ref[:, 0] = hr
        hfin_ref[:, 1] = hi


def _s5_scan_call(x4, at_re, at_im, h0, *, nb):
    b, n, g, w = x4.shape
    nb = min(nb, n)
    steps = n // nb
    blk = lambda rev: pl.BlockSpec((b, nb, g, w), (lambda i: (0, steps - 1 - i, 0, 0)) if rev
                                   else (lambda i: (0, i, 0, 0)))
    st = pl.BlockSpec((b, 2, g, LANES), lambda i: (0, 0, 0, 0))
    return pl.pallas_call(
        functools.partial(_s5_scan_kernel, nb=nb),
        grid=(steps,),
        in_specs=[blk(False), blk(True), _const_spec((g, LANES)), _const_spec((g, LANES)), st],
        out_specs=[blk(False), blk(True), st],
        out_shape=[jax.ShapeDtypeStruct(x4.shape, F32), jax.ShapeDtypeStruct(x4.shape, F32),
                   jax.ShapeDtypeStruct((b, 2, g, LANES), F32)],
        scratch_shapes=[pltpu.VMEM((b, g, LANES), F32), pltpu.VMEM((b, g, LANES), F32)],
        compiler_params=_params("arbitrary"),
        name="s5_scan",
    )(x4, x4, at_re, at_im, h0)


def _s5_y_kernel(u_ref, hf_ref, hb_ref, m_ref, qf_ref, qb_ref, y_ref):
    y = (_dot(u_ref[0], m_ref[0]) + _dot(hf_ref[...].astype(BF16), qf_ref[0])
         + _dot(hb_ref[...].astype(BF16), qb_ref[0]))
    rt = y.shape[0]
    for t in range(S5_T):
        y_ref[pl.ds(t, rt, stride=S5_T), :] = y[:, t * LANES:(t + 1) * LANES]


def _s5_y_call(uf, hf2, hb2, m_blk, qf_blk, qb_blk, *, rt):
    sg, r, uw = uf.shape
    rt = min(rt, r)
    wspec = pl.BlockSpec((1, uw, uw), lambda si, i: (si, 0, 0), pipeline_mode=pl.Buffered(1))
    hspec = pl.BlockSpec((rt, uw), lambda si, i: (i, si))
    return pl.pallas_call(
        _s5_y_kernel,
        grid=(sg, r // rt),
        in_specs=[pl.BlockSpec((1, rt, uw), lambda si, i: (si, i, 0)), hspec, hspec, wspec, wspec, wspec],
        out_specs=pl.BlockSpec((rt * S5_T, LANES), lambda si, i: (i, si)),
        out_shape=jax.ShapeDtypeStruct((r * S5_T, sg * LANES), F32),
        compiler_params=_params("parallel", "parallel"),
        name="s5_y",
    )(uf, hf2, hb2, m_blk, qf_blk, qb_blk)


def _s5_mix(uf, ops, h0, bsz, *, nb, rt):
    p_blk, m_blk, qf_blk, qb_blk, at_re, at_im = ops
    r = uf.shape[1]
    n = r // bsz
    x = _s5_x_call(uf, p_blk, rt=rt)
    hf, hb, hfin = _s5_scan_call(x.reshape(bsz, n, S5_GROUPS, S5_CW), at_re, at_im, h0, nb=nb)
    y = _s5_y_call(uf, hf.reshape(r, S5_GROUPS * S5_CW), hb.reshape(r, S5_GROUPS * S5_CW),
                   m_blk, qf_blk, qb_blk, rt=rt)
    return y.reshape(bsz, n * S5_T, S5_WIDTH), hfin


def _attn_kernel(q_ref, kt_ref, v_ref, o_ref, m_scr, acc_scr, s0_scr, s1_scr, *, tk, n_kv):
    rows = q_ref.shape[2] * q_ref.shape[3]
    q = q_ref[0, 0].reshape(rows, LANES)
    m_scr[...] = jnp.full(m_scr.shape, -jnp.inf, F32)
    acc_scr[...] = jnp.zeros(acc_scr.shape, F32)

    def scores(j, s_scr):
        off = pl.multiple_of(j * tk, tk)
        s_scr[...] = _dot(q, kt_ref[0, :, pl.ds(off, tk)])

    def softmax_pv(j, s_scr):
        off = pl.multiple_of(j * tk, tk)
        s = s_scr[...]
        m_prev = m_scr[...]
        m_new = jnp.maximum(m_prev, jnp.max(s, axis=1, keepdims=True))
        alpha = jnp.exp2(m_prev - m_new)
        p = jnp.exp2(s - jnp.tile(m_new, (1, tk // LANES))).astype(BF16)
        acc_scr[...] = acc_scr[...] * jnp.tile(alpha, (1, 2)) + _dot(p, v_ref[0, pl.ds(off, tk), :])
        m_scr[...] = m_new

    scores(0, s0_scr)

    def pair(t, carry):
        j = 2 * t
        scores(j + 1, s1_scr)
        softmax_pv(j, s0_scr)
        scores(j + 2, s0_scr)
        softmax_pv(j + 1, s1_scr)
        return carry

    lax.fori_loop(0, (n_kv - 1) // 2, pair, 0)
    if n_kv % 2 == 0:
        scores(n_kv - 1, s1_scr)
        softmax_pv(n_kv - 2, s0_scr)
        softmax_pv(n_kv - 1, s1_scr)
    else:
        softmax_pv(n_kv - 1, s0_scr)
    acc = acc_scr[...]
    out = acc[:, 0:LANES] / acc[:, LANES:2 * LANES]
    o_ref[0, 0] = out.reshape(q_ref.shape[2:]).astype(o_ref.dtype)


def _attn_call(qs, kt, v_ext, *, tq, tk):
    b, kvh, grp, l, _ = qs.shape
    lk = kt.shape[2]
    tq, tk = min(tq, l), min(tk, lk)
    rows = grp * tq
    blk = pl.BlockSpec((1, 1, grp, tq, LANES), lambda bi, ki, i: (bi, ki, 0, i, 0))
    return pl.pallas_call(
        functools.partial(_attn_kernel, tk=tk, n_kv=lk // tk),
        grid=(b, kvh, l // tq),
        in_specs=[blk,
                  pl.BlockSpec((1, LANES, lk), lambda bi, ki, i: (bi, 0, 0), pipeline_mode=pl.Buffered(1)),
                  pl.BlockSpec((1, lk, 2 * LANES), lambda bi, ki, i: (bi, 0, 0), pipeline_mode=pl.Buffered(1))],
        out_specs=blk,
        out_shape=jax.ShapeDtypeStruct(qs.shape, BF16),
        scratch_shapes=[pltpu.VMEM((rows, LANES), F32), pltpu.VMEM((rows, 2 * LANES), F32),
                        pltpu.VMEM((rows, tk), F32), pltpu.VMEM((rows, tk), F32)],
        compiler_params=_params("parallel", "parallel", "arbitrary"),
        name="attention",
    )(qs, kt, v_ext)


def _kv_layout(ak, av):
    kt = jnp.transpose(ak, (0, 2, 1))
    v_ext = jnp.concatenate([av, jnp.ones_like(av)], axis=-1)
    return kt, v_ext


def _merge_kernel(h_ref, mod_ref, gn_ref, yg_ref, ys_ref, u_ref, ya_ref, wbg_ref, bbg_ref, wp_ref, wpa_ref,
                  wo_ref, d_ref, gw_ref, gb_ref, o_ref):
    h = h_ref[0]
    d = h.shape[-1]
    n = _rms_mod(h, gn_ref[...], mod_ref[0, 0:1], mod_ref[0, 1:2]).astype(BF16)
    g = jax.nn.sigmoid(_dot(n, wbg_ref[...]) + bbg_ref[...])
    y = jax.nn.gelu(ys_ref[0] + d_ref[...] * u_ref[0])
    y_s5 = y * jax.nn.sigmoid(_dot(y.astype(BF16), gw_ref[...]) + gb_ref[...])
    ya = jnp.concatenate([ya_ref[0, kv, j] for kv in range(ATT_KV_HEADS) for j in range(ATT_GROUP)], axis=-1)
    m = (g[:, 0:d] * _dot(yg_ref[0].astype(BF16), wp_ref[0])
         + g[:, d:2 * d] * _dot(y_s5.astype(BF16), wp_ref[1])
         + g[:, 2 * d:3 * d] * _dot(ya, wpa_ref[...]))
    o_ref[0] = h + mod_ref[0, 2:3] * _dot(m.astype(BF16), wo_ref[...])


def _merge_call(h, mod3, g_norm, y_gla, y_s5raw, u, y_att, lw, *, tm):
    b, l, d = h.shape
    tm = min(tm, l)
    tok = lambda w: pl.BlockSpec((1, tm, w), lambda bi, i: (bi, i, 0))
    bw = y_gla.shape[-1]
    att = pl.BlockSpec((1, ATT_KV_HEADS, ATT_GROUP, tm, LANES), lambda bi, i: (bi, 0, 0, i, 0))
    return pl.pallas_call(
        _merge_kernel,
        grid=(b, l // tm),
        in_specs=[tok(d), pl.BlockSpec((1, 3, d), lambda bi, i: (bi, 0, 0)), _const_spec((1, d)),
                  tok(bw), tok(bw), tok(bw), att,
                  _const_spec((d, 3 * d)), _const_spec((1, 3 * d)), _const_spec((2, bw, d)),
                  _const_spec((ATT_HEADS * LANES, d)), _const_spec((d, d)),
                  _const_spec((1, bw)), _const_spec((bw, bw)), _const_spec((1, bw))],
        out_specs=tok(d),
        out_shape=jax.ShapeDtypeStruct((b, l, d), F32),
        compiler_params=_params("parallel", "parallel"),
        name="merge",
    )(h, mod3, g_norm.reshape(1, d), y_gla, y_s5raw, u, y_att, lw["w_bgate"], lw["b_bgate"], lw["w_bproj"],
      lw["w_aproj"], lw["w_out"], lw["s5_d"], lw["glu_w"], lw["glu_b"])


def _rope_tables(n_tokens):
    rows = n_tokens // GRID_W
    row = jnp.repeat(jnp.arange(rows, dtype=F32), GRID_W)
    col = jnp.tile(jnp.arange(GRID_W, dtype=F32), rows)
    inv = ROPE_THETA ** (-jnp.arange(ROPE_PAIRS, dtype=F32) / ROPE_PAIRS)
    ang_r, ang_c = row[:, None] * inv, col[:, None] * inv
    zero = jnp.zeros_like(ang_r)
    cos = jnp.concatenate([jnp.cos(ang_r), jnp.cos(ang_r), jnp.cos(ang_c), jnp.cos(ang_c)], axis=-1)
    sin_a = jnp.concatenate([-jnp.sin(ang_r), zero, -jnp.sin(ang_c), zero], axis=-1)
    sin_b = jnp.concatenate([zero, jnp.sin(ang_r), zero, jnp.sin(ang_c)], axis=-1)
    tile = lambda t: jnp.tile(t, (1, ATT_HEADS))
    return tile(cos), tile(sin_a), tile(sin_b)


def _block_diag_ones(width, seg):
    idx = np.arange(width) // seg
    return jnp.asarray((idx[:, None] == idx[None, :]).astype(np.float32), BF16)


def _layer_weights(i, w_in, gla_gate_w, gla_gate_b, attn_q_norm_w, attn_k_norm_w, w_branch_gate,
                   b_branch_gate, w_branch_proj, w_out, s5_d, s5_glu_w, s5_glu_b):
    offs = np.concatenate([[0], np.cumsum(IN_WIDTHS)])
    col = lambda k: w_in[i][:, offs[k]:offs[k + 1]]
    w_main = jnp.concatenate([col(0), col(1), col(2), col(5), col(6), col(7), col(8), col(9)], axis=1)
    d = w_in.shape[1]
    rk = GLA_GATE_RANK
    w_gz = jnp.zeros((d, LANES), F32).at[:, 0:rk].set(col(3)).at[:, rk:2 * rk].set(col(4))
    gw = jnp.zeros((LANES, 2 * GLA_QK), F32)
    gw = gw.at[0:rk, 0:GLA_QK].set(gla_gate_w[i, 0]).at[rk:2 * rk, GLA_QK:].set(gla_gate_w[i, 1])
    wa = w_branch_proj[i, 2].reshape(ATT_KV_HEADS, ATT_GROUP, ATT_HEAD_DIM, d)
    za = jnp.zeros_like(wa[0])
    w_aproj = jnp.concatenate([jnp.concatenate([wa[0], za], axis=1), jnp.concatenate([za, wa[1]], axis=1)],
                              axis=0).reshape(ATT_HEADS * LANES, d)
    return dict(
        w_aproj=w_aproj.astype(BF16),
        w_main=w_main.astype(BF16), w_gz=w_gz.astype(BF16), gate_w_bd=gw.astype(BF16),
        gate_b=gla_gate_b[i].reshape(1, 2 * GLA_QK),
        q_norm_w=(jnp.tile(attn_q_norm_w[i], ATT_HEADS) * (ATT_HEAD_DIM ** -0.5 * LOG2_E)).reshape(1, ATT_Q),
        k_norm_w=jnp.tile(attn_k_norm_w[i], ATT_KV_HEADS).reshape(1, ATT_KV),
        bd_q=_block_diag_ones(ATT_Q, ATT_HEAD_DIM), bd_k=_block_diag_ones(ATT_KV, ATT_HEAD_DIM),
        w_bgate=w_branch_gate[i].astype(BF16), b_bgate=b_branch_gate[i].reshape(1, -1),
        w_bproj=w_branch_proj[i, 0:2].astype(BF16), w_out=w_out[i].astype(BF16),
        s5_d=s5_d[i].reshape(1, -1), glu_w=s5_glu_w[i].astype(BF16), glu_b=s5_glu_b[i].reshape(1, -1),
    )


def _mixer_inputs(h, mod, g_norm, lw, tables, *, rope, tm):
    mod2 = mod[:, 3:5]
    return _inproj_call(h, mod2, g_norm, lw, tables, rope=rope, tm=tm)


def kernel(x, c, ctx, c_ctx, w_ada, b_ada, norm_w, w_ffn_gate, w_ffn_up, w_ffn_down, w_in, gla_gate_w,
           gla_gate_b, gla_norm_w, s5_a_re, s5_a_im, s5_log_dt, s5_b_re, s5_b_im, s5_c_re, s5_c_im, s5_d,
           s5_glu_w, s5_glu_b, attn_q_norm_w, attn_k_norm_w, w_branch_gate, b_branch_gate, w_branch_proj,
           w_out, final_norm_w):
    bsz, seq, d = x.shape
    depth = w_ada.shape[0]
    tables = _rope_tables(seq)
    ctx_tables = tuple(t[0:ctx.shape[1]] for t in tables)
    cond = jnp.zeros((8, d), F32).at[0:bsz].set(c).at[bsz].set(c_ctx)
    tm_ffn, tm_in, tm_merge, tb_gla, nb_s5, rt_s5, tq, tk = (
        TILES[k] for k in ("ffn", "inproj", "merge", "gla", "s5_scan", "s5_rows", "att_q", "att_k"))

    h_lat, h_ctx = x, ctx
    for i in range(depth):
        last = i == depth - 1
        mods = _ada_call(cond, w_ada[i], b_ada[i])
        mod_l = mods[0:bsz].reshape(bsz, N_MOD, d)
        mod_c = jnp.broadcast_to(mods[bsz].reshape(1, N_MOD, d), (bsz, N_MOD, d))
        lw = _layer_weights(i, w_in, gla_gate_w, gla_gate_b, attn_q_norm_w, attn_k_norm_w, w_branch_gate,
                            b_branch_gate, w_branch_proj, w_out, s5_d, s5_glu_w, s5_glu_b)
        ffn_w = [(w_ffn_gate[i, j].astype(BF16), w_ffn_up[i, j].astype(BF16), w_ffn_down[i, j].astype(BF16))
                 for j in range(2)]

        def ffn(h, mod, j, fin=False):
            return _ffn_call(h, mod[:, 6 * j:6 * j + 3], norm_w[i, 2 * j], *ffn_w[j], final_norm_w,
                             final_norm=fin, tm=tm_ffn)

        h_lat = ffn(h_lat, mod_l, 0)
        h_ctx = ffn(h_ctx, mod_c, 0)

        qk_c, v_c, r_c, lg_c, u_c, uf_c, aq_c, ak_c, av_c = _mixer_inputs(h_ctx, mod_c, norm_w[i, 1], lw,
                                                                          ctx_tables, rope=False, tm=tm_in)
        qk_l, v_l, r_l, lg_l, u_l, uf_l, aq_l, ak_l, av_l = _mixer_inputs(h_lat, mod_l, norm_w[i, 1], lw,
                                                                          tables, rope=True, tm=tm_in)
        s0 = jnp.zeros((bsz, GLA_QK, GLA_DV), F32)
        yg_c, sf_c, sb_c = _gla_bidir(qk_c, v_c, lg_c, r_c, gla_norm_w[i], s0, s0, tb=tb_gla)
        yg_l, _, _ = _gla_bidir(qk_l, v_l, lg_l, r_l, gla_norm_w[i], sf_c, sb_c, tb=tb_gla)
        ops = _s5_operators(s5_a_re[i], s5_a_im[i], s5_log_dt[i], s5_b_re[i], s5_b_im[i], s5_c_re[i], s5_c_im[i])
        h0 = jnp.zeros((bsz, 2, S5_GROUPS, LANES), F32)
        ys_c, hfin_c = _s5_mix(uf_c, ops, h0, bsz, nb=nb_s5, rt=rt_s5)
        ys_l, _ = _s5_mix(uf_l, ops, hfin_c, bsz, nb=nb_s5, rt=rt_s5)
        kt, v_ext = _kv_layout(jnp.concatenate([ak_c, ak_l], axis=1), jnp.concatenate([av_c, av_l], axis=1))
        ya_l = _attn_call(aq_l, kt, v_ext, tq=tq, tk=tk)
        h_lat = _merge_call(h_lat, mod_l[:, 3:6], norm_w[i, 1], yg_l, ys_l, u_l, ya_l, lw, tm=tm_merge)
        h_lat = ffn(h_lat, mod_l, 1, last)
        if not last:
            kt_c, v_ext_c = _kv_layout(ak_c, av_c)
            ya_c = _attn_call(aq_c, kt_c, v_ext_c, tq=tq, tk=tk)
            h_ctx = _merge_call(h_ctx, mod_c[:, 3:6], norm_w[i, 1], yg_c, ys_c, u_c, ya_c, lw, tm=tm_merge)
            h_ctx = ffn(h_ctx, mod_c, 1)
    return h_lat
```

```python
import functools

import jax
import jax.numpy as jnp
import numpy as np
from jax import lax
from jax.experimental import pallas as pl
from jax.experimental.pallas import tpu as pltpu

F32 = jnp.float32
BF16 = jnp.bfloat16

N_MOD = 9
MACARON_WEIGHT = 0.5
NORM_EPS = 1e-6
GRID_W = 64
GLA_HEADS = 4
GLA_DK = 64
GLA_DV = 128
GLA_QK = GLA_HEADS * GLA_DK
GLA_V = GLA_HEADS * GLA_DV
GLA_GATE_RANK = 16
GLA_TAU = 16.0
GLA_CHUNK = 64
S5_WIDTH = 512
S5_GROUP = 16
S5_GROUPS = S5_WIDTH // S5_GROUP
S5_STATE = 64
ATT_HEADS = 8
ATT_KV_HEADS = 2
ATT_GROUP = ATT_HEADS // ATT_KV_HEADS
ATT_HEAD_DIM = 64
ATT_Q = ATT_HEADS * ATT_HEAD_DIM
ATT_KV = ATT_KV_HEADS * ATT_HEAD_DIM
ROPE_PAIRS = ATT_HEAD_DIM // 4
ROPE_THETA = 10000.0
IN_WIDTHS = (GLA_QK, GLA_QK, GLA_V, GLA_GATE_RANK, GLA_GATE_RANK, GLA_V, S5_WIDTH, ATT_Q, ATT_KV, ATT_KV)

LANES = 128
MXU_DIM = 256
VMEM_LIMIT = 56 * 1024 * 1024

LOG2_E = 1.4426950408889634

S5_T = MXU_DIM // S5_GROUP
S5_CW = S5_T * S5_GROUP
S5_GPB = LANES // S5_GROUP
S5_SG = S5_GROUPS // S5_GPB
S5_UW = S5_T * LANES

TILES = dict(ffn=512, inproj=512, merge=512, gla=512, s5_scan=32, s5_rows=256, att_q=128, att_k=1280)


def _params(*sem):
    return pltpu.CompilerParams(dimension_semantics=sem, vmem_limit_bytes=VMEM_LIMIT)


def _const_spec(shape):
    nd = len(shape)
    return pl.BlockSpec(shape, lambda *_: (0,) * nd, pipeline_mode=pl.Buffered(1))


def _dotg(a, b, contract):
    return lax.dot_general(a, b, (((contract[0],), (contract[1],)), ((), ())),
                           precision=lax.Precision.DEFAULT, preferred_element_type=F32)


def _dot(a, b):
    return _dotg(a, b, (1, 0))


def _dot_hi_lo(x, w):
    hi = x.astype(BF16)
    lo = (x - hi.astype(F32)).astype(BF16)
    return _dot(hi, w) + _dot(lo, w)


def _rms_mod(h, g_norm, shift, scale):
    ms = jnp.mean(h * h, axis=-1, keepdims=True)
    y = h * lax.rsqrt(ms + NORM_EPS) * g_norm
    return y * (1.0 + scale) + shift


def _log_sigmoid(x):
    return jnp.minimum(x, 0.0) - jnp.log(1.0 + jnp.exp(-jnp.abs(x)))


def _ada_kernel(c_ref, w_ref, b_ref, o_ref):
    c = c_ref[...]
    sc = (c * jax.nn.sigmoid(c)).astype(BF16)
    o_ref[...] = _dot(sc, w_ref[...].astype(BF16)) + b_ref[...]


def _ada_call(cond, w_all, b_all, layer):
    rows, d = cond.shape
    depth, _, n = w_all.shape
    tn = 9 * LANES
    return pl.pallas_call(
        _ada_kernel,
        grid=(n // tn,),
        in_specs=[pl.BlockSpec((rows, d), lambda j: (0, 0)),
                  pl.BlockSpec((None, d, tn), lambda j: (layer, 0, j)),
                  pl.BlockSpec((None, 1, tn), lambda j: (layer, 0, j))],
        out_specs=pl.BlockSpec((rows, tn), lambda j: (0, j)),
        out_shape=jax.ShapeDtypeStruct((rows, n), F32),
        compiler_params=_params("arbitrary"),
        name="adaln",
    )(cond, w_all, b_all.reshape(depth, 1, n))


def _ffn_kernel(h_ref, mod_ref, gn_ref, wg_ref, wu_ref, wd_ref, fn_ref, o_ref, *, final_norm):
    h = h_ref[0]
    shift, scale, gate = mod_ref[0, 0:1], mod_ref[0, 1:2], mod_ref[0, 2:3]
    n = _rms_mod(h, gn_ref[...], shift, scale).astype(BF16)
    g = _dot(n, wg_ref[...])
    u = _dot(n, wu_ref[...])
    a = (g * jax.nn.sigmoid(g) * u).astype(BF16)
    y = h + (MACARON_WEIGHT * gate) * _dot(a, wd_ref[...])
    if final_norm:
        ms = jnp.mean(y * y, axis=-1, keepdims=True)
        y = y * lax.rsqrt(ms + NORM_EPS) * fn_ref[...]
    o_ref[0] = y


def _ffn_call(h, mod3, g_norm, wg, wu, wd, final_w, *, layer, sub, final_norm, tm):
    b, l, d = h.shape
    f = wg.shape[-1]
    tm = min(tm, l)
    wspec = lambda r, c: pl.BlockSpec((None, None, r, c), lambda bi, i: (layer, sub, 0, 0),
                                      pipeline_mode=pl.Buffered(1))
    return pl.pallas_call(
        functools.partial(_ffn_kernel, final_norm=final_norm),
        grid=(b, l // tm),
        in_specs=[pl.BlockSpec((1, tm, d), lambda bi, i: (bi, i, 0)),
                  pl.BlockSpec((1, 3, d), lambda bi, i: (bi, 0, 0)),
                  _const_spec((1, d)), wspec(d, f), wspec(d, f), wspec(f, d),
                  _const_spec((1, d))],
        out_specs=pl.BlockSpec((1, tm, d), lambda bi, i: (bi, i, 0)),
        out_shape=jax.ShapeDtypeStruct((b, l, d), F32),
        compiler_params=_params("parallel", "parallel"),
        name="ffn",
    )(h, mod3, g_norm.reshape(1, d), wg, wu, wd, final_w.reshape(1, d))


def _inproj_kernel(h_ref, mod_ref, gn_ref, wm_ref, wgz_ref, gw_ref, gb_ref, qw_ref, kw_ref,
                   bdq_ref, bdk_ref, cos_ref, sa_ref, sb_ref,
                   qk_ref, v_ref, r_ref, lg_ref, u_ref, uf_ref, aq_ref, ak_ref, av_ref, us_scr, *, rope):
    h = h_ref[0]
    tm = h.shape[0]
    n = _rms_mod(h, gn_ref[...], mod_ref[0, 0:1], mod_ref[0, 1:2]).astype(BF16)
    z = _dot(n, wm_ref[...])
    o = 0
    q = z[:, o:o + GLA_QK] * (GLA_DK ** -0.5); o += GLA_QK
    k = z[:, o:o + GLA_QK]; o += GLA_QK
    qk_ref[0, :, 0:GLA_QK] = q
    qk_ref[0, :, GLA_QK:2 * GLA_QK] = k
    v_ref[0] = z[:, o:o + GLA_V]; o += GLA_V
    r_ref[0] = z[:, o:o + GLA_V]; o += GLA_V
    u_ref[0] = z[:, o:o + S5_WIDTH]; o += S5_WIDTH
    for sg in range(S5_SG):
        us_scr[sg] = z[:, o - S5_WIDTH + sg * LANES:o - S5_WIDTH + (sg + 1) * LANES]
        for t in range(S5_T):
            piece = us_scr[sg, pl.ds(t, tm // S5_T, stride=S5_T), :]
            uf_ref[sg, :, t * LANES:(t + 1) * LANES] = piece.astype(BF16)
    xq = z[:, o:o + ATT_Q]; o += ATT_Q
    xk = z[:, o:o + ATT_KV]; o += ATT_KV
    av_ref[0] = z[:, o:o + ATT_KV].astype(BF16)

    zg = _dot(n, wgz_ref[...]).astype(BF16)
    lg_ref[0] = _log_sigmoid(_dot(zg, gw_ref[...]) + gb_ref[...]) * (1.0 / GLA_TAU)

    def head_norm(x, bd_ref, w_ref):
        ss = _dot_hi_lo(x * x, bd_ref[...])
        return x * lax.rsqrt(ss * (1.0 / ATT_HEAD_DIM) + NORM_EPS) * w_ref[...]

    def rotary(x, width):
        if not rope:
            return x
        c, sa, sb = cos_ref[:, 0:width], sa_ref[:, 0:width], sb_ref[:, 0:width]
        return (x * c + pltpu.roll(x, width - ROPE_PAIRS, 1) * sa + pltpu.roll(x, ROPE_PAIRS, 1) * sb)

    ak_ref[0] = rotary(head_norm(xk, bdk_ref, kw_ref), ATT_KV).astype(BF16)
    aq = rotary(head_norm(xq, bdq_ref, qw_ref), ATT_Q)
    lane = lax.broadcasted_iota(jnp.int32, (tm, LANES), 1)
    for kv in range(ATT_KV_HEADS):
        keep = (lane >= kv * ATT_HEAD_DIM) & (lane < (kv + 1) * ATT_HEAD_DIM)
        for j in range(ATT_GROUP):
            hd = kv * ATT_GROUP + j
            blk = aq[:, (hd // 2) * LANES:(hd // 2 + 1) * LANES]
            if hd % 2 != kv:
                blk = pltpu.roll(blk, ATT_HEAD_DIM, 1)
            aq_ref[0, kv, j] = jnp.where(keep, blk, 0.0).astype(BF16)


def _inproj_call(h, mod2, g_norm, lw, tables, *, rope, tm):
    b, l, d = h.shape
    tm = min(tm, l)
    tok = lambda w: pl.BlockSpec((1, tm, w), lambda bi, i: (bi, i, 0))
    tab = pl.BlockSpec((tm, ATT_Q), lambda bi, i: (i, 0))
    wm = lw["w_main"]
    nt = l // tm
    tok_out = lambda w, dt: (tok(w), jax.ShapeDtypeStruct((b, l, w), dt))
    outs = [tok_out(2 * GLA_QK, F32), tok_out(GLA_V, F32), tok_out(GLA_V, F32), tok_out(2 * GLA_QK, F32),
            tok_out(S5_WIDTH, F32),
            (pl.BlockSpec((S5_SG, tm // S5_T, S5_UW), lambda bi, i: (0, bi * nt + i, 0)),
             jax.ShapeDtypeStruct((S5_SG, b * l // S5_T, S5_UW), BF16)),
            (pl.BlockSpec((1, ATT_KV_HEADS, ATT_GROUP, tm, LANES), lambda bi, i: (bi, 0, 0, i, 0)),
             jax.ShapeDtypeStruct((b, ATT_KV_HEADS, ATT_GROUP, l, LANES), BF16)),
            tok_out(ATT_KV, BF16), tok_out(ATT_KV, BF16)]
    return pl.pallas_call(
        functools.partial(_inproj_kernel, rope=rope),
        grid=(b, nt),
        in_specs=[tok(d), pl.BlockSpec((1, 2, d), lambda bi, i: (bi, 0, 0)), _const_spec((1, d)),
                  _const_spec(wm.shape), _const_spec((d, LANES)), _const_spec((LANES, 2 * GLA_QK)),
                  _const_spec((1, 2 * GLA_QK)), _const_spec((1, ATT_Q)), _const_spec((1, ATT_KV)),
                  _const_spec((ATT_Q, ATT_Q)), _const_spec((ATT_KV, ATT_KV)), tab, tab, tab],
        out_specs=[s for s, _ in outs],
        out_shape=[t for _, t in outs],
        scratch_shapes=[pltpu.VMEM((S5_SG, tm, LANES), F32)],
        compiler_params=_params("parallel", "parallel"),
        name="inproj",
    )(h, mod2, g_norm.reshape(1, d), wm, lw["w_gz"], lw["gate_w_bd"], lw["gate_b"], lw["q_norm_w"],
      lw["k_norm_w"], lw["bd_q"], lw["bd_k"], *tables)


def _gla_kernel(qk_ref, v_ref, lg_ref, s0_ref, tri_ref, of_ref, r_ref, nw_ref, o_ref, sfin_ref, s_scr,
                *, reverse, readout, n_chunks, n_batch):
    i = pl.program_id(0)

    @pl.when(i == 0)
    def _():
        s_scr[...] = s0_ref[...]

    tri = tri_ref[...]
    c, nh = GLA_CHUNK, GLA_HEADS
    row = lax.broadcasted_iota(jnp.int32, (nh * c, c), 0) % c
    col = lax.broadcasted_iota(jnp.int32, (nh * c, c), 1)
    keep = (col >= row) if reverse else (col <= row)
    head_of_lane = lax.broadcasted_iota(jnp.int32, (c, GLA_QK), 1) // GLA_DK

    def chunk(bi, ci):
        rows = slice(ci * c, (ci + 1) * c)
        g = lg_ref[bi, rows, :]
        g_hi = g.astype(BF16)
        b = _dot(tri, g_hi) + _dot(tri, (g - g_hi.astype(F32)).astype(BF16))
        b_last = b[0:1, :] if reverse else b[c - 1:c, :]
        q_in = qk_ref[bi, rows, 0:GLA_QK] * jnp.exp(b)
        k = qk_ref[bi, rows, GLA_QK:2 * GLA_QK]
        k_in = (k * jnp.exp(-b)).astype(BF16)
        k_out = (k * jnp.exp(b_last - b)).astype(BF16)
        dec = jnp.exp(jnp.broadcast_to(b_last, (LANES, GLA_QK)).T)
        v = v_ref[bi, rows, :].astype(BF16)
        s_old = s_scr[bi]
        q_st = jnp.concatenate([jnp.where(head_of_lane == hd, q_in, 0.0) for hd in range(nh)],
                               axis=0).astype(BF16)
        att = _dotg(q_st, k_in, (1, 1))
        att = jnp.where(keep, att, 0.0).astype(BF16)
        o_intra = _dot(att, v)
        o_inter = _dot(q_st, s_old.astype(BF16))
        ds = _dotg(k_out, v, (0, 0))
        for hd in range(nh):
            tok = slice(hd * c, (hd + 1) * c)
            vcols = slice(hd * GLA_DV, (hd + 1) * GLA_DV)
            o_h = o_intra[tok, vcols] + o_inter[tok]
            if readout:
                o_h = o_h + of_ref[bi, rows, vcols]
                ms = jnp.mean(o_h * o_h, axis=-1, keepdims=True)
                y = o_h * lax.rsqrt(ms + NORM_EPS) * nw_ref[...]
                rr = r_ref[bi, rows, vcols]
                o_h = y * (rr * jax.nn.sigmoid(rr))
            o_ref[bi, rows, vcols] = o_h
        d_state = jnp.concatenate([ds[hd * GLA_DK:(hd + 1) * GLA_DK, hd * GLA_DV:(hd + 1) * GLA_DV]
                                   for hd in range(nh)], axis=0)
        s_scr[bi] = dec * s_old + d_state

    for ci in (range(n_chunks - 1, -1, -1) if reverse else range(n_chunks)):
        for bi in range(n_batch):
            chunk(bi, ci)

    @pl.when(i == pl.num_programs(0) - 1)
    def _():
        sfin_ref[...] = s_scr[...]


def _gla_call(qk, v, lg, s0, o_f, r, norm_w, *, reverse, readout, tb):
    b, l, _ = v.shape
    tb = min(tb, l)
    nb = l // tb
    idx = (lambda i: (0, nb - 1 - i, 0)) if reverse else (lambda i: (0, i, 0))
    tok = lambda w: pl.BlockSpec((b, tb, w), idx)
    lg_off = 1 if reverse else 0
    lg_spec = pl.BlockSpec((b, tb, GLA_QK), (lambda i: (0, nb - 1 - i, lg_off)) if reverse
                           else (lambda i: (0, i, lg_off)))
    state = pl.BlockSpec((b, GLA_QK, GLA_DV), lambda i: (0, 0, 0))
    ones = np.triu(np.ones((GLA_CHUNK, GLA_CHUNK), np.float32)) if reverse else \
        np.tril(np.ones((GLA_CHUNK, GLA_CHUNK), np.float32))
    return pl.pallas_call(
        functools.partial(_gla_kernel, reverse=reverse, readout=readout, n_chunks=tb // GLA_CHUNK, n_batch=b),
        grid=(nb,),
        in_specs=[tok(2 * GLA_QK), tok(GLA_V), lg_spec, state, _const_spec((GLA_CHUNK, GLA_CHUNK)),
                  tok(GLA_V), tok(GLA_V), _const_spec((1, GLA_DV))],
        out_specs=[tok(GLA_V), state],
        out_shape=[jax.ShapeDtypeStruct((b, l, GLA_V), F32), jax.ShapeDtypeStruct((b, GLA_QK, GLA_DV), F32)],
        scratch_shapes=[pltpu.VMEM((b, GLA_QK, GLA_DV), F32)],
        compiler_params=_params("arbitrary"),
        name="gla_bwd" if reverse else "gla_fwd",
    )(qk, v, lg, s0, jnp.asarray(ones, BF16), o_f, r, norm_w.reshape(1, GLA_DV))


def _gla_bidir(qk, v, lg, r, norm_w, s0_f, s0_b, *, tb):
    o_f, s_f = _gla_call(qk, v, lg, s0_f, v, r, norm_w, reverse=False, readout=False, tb=tb)
    y, s_b = _gla_call(qk, v, lg, s0_b, o_f, r, norm_w, reverse=True, readout=True, tb=tb)
    return y, s_f, s_b


def _s5_operators(a_re, a_im, log_dt, b_re, b_im, c_re, c_im):
    hp = lax.Precision.HIGHEST
    t = S5_T
    cr, ci_ = c_re.astype(F32), c_im.astype(F32)
    ks = jnp.arange(t + 1, dtype=F32)[:, None, None]
    per_dir = []
    for d in range(2):
        dt = jnp.exp(log_dt[d].astype(F32))[:, None]
        ar, ai = a_re[d].astype(F32), a_im[d].astype(F32)
        mag = jnp.exp(ks * dt * ar)
        pw_re, pw_im = mag * jnp.cos(ks * dt * ai), mag * jnp.sin(ks * dt * ai)
        den = ar * ar + ai * ai
        xr, xi = pw_re[1] - 1.0, pw_im[1]
        coef_re = (xr * ar + xi * ai) / den
        coef_im = (xi * ar - xr * ai) / den
        br, bi = b_re.astype(F32), b_im.astype(F32)
        bb_re = coef_re[..., None] * br - coef_im[..., None] * bi
        bb_im = coef_re[..., None] * bi + coef_im[..., None] * br
        ca_re = cr[None] * pw_re[:, :, None, :] - ci_[None] * pw_im[:, :, None, :]
        ca_im = cr[None] * pw_im[:, :, None, :] + ci_[None] * pw_re[:, :, None, :]
        kk = (jnp.einsum("kgop,gpi->kgoi", ca_re, bb_re, precision=hp)
              - jnp.einsum("kgop,gpi->kgoi", ca_im, bb_im, precision=hp))
        s_idx = jnp.arange(t)[:, None]
        t_idx = jnp.arange(t)[None, :]
        lag = (t_idx - s_idx) if d == 0 else (s_idx - t_idx)
        toe = jnp.where((lag >= 0)[:, :, None, None, None], kk[jnp.clip(lag, 0, t)], 0.0)
        m = jnp.transpose(toe, (2, 0, 4, 1, 3)).reshape(S5_GROUPS, S5_CW, S5_CW)
        p_pow = (t - 1 - jnp.arange(t)) if d == 0 else jnp.arange(t)
        pr, pi = pw_re[p_pow], pw_im[p_pow]
        p_re = pr[:, :, :, None] * bb_re[None] - pi[:, :, :, None] * bb_im[None]
        p_im = pr[:, :, :, None] * bb_im[None] + pi[:, :, :, None] * bb_re[None]
        flat_p = lambda x: jnp.transpose(x, (1, 0, 3, 2)).reshape(S5_GROUPS, S5_CW, S5_STATE)
        q_pow = (jnp.arange(t) + 1) if d == 0 else (t - jnp.arange(t))
        q_re, q_im = ca_re[q_pow], -ca_im[q_pow]
        flat_q = lambda x: jnp.transpose(x, (1, 3, 0, 2)).reshape(S5_GROUPS, S5_STATE, S5_CW)
        per_dir.append(dict(m=m, p_re=flat_p(p_re), p_im=flat_p(p_im), q_re=flat_q(q_re), q_im=flat_q(q_im),
                            at_re=pw_re[t], at_im=pw_im[t]))
    f, bk = per_dir
    zq = jnp.zeros_like(f["q_re"])
    p_cat = jnp.concatenate([f["p_re"], bk["p_re"], f["p_im"], bk["p_im"]], axis=-1)
    q_f = jnp.concatenate([f["q_re"], zq, f["q_im"], zq], axis=1)
    q_b = jnp.concatenate([zq, bk["q_re"], zq, bk["q_im"]], axis=1)
    at_re = jnp.concatenate([f["at_re"], bk["at_re"]], axis=-1)
    at_im = jnp.concatenate([f["at_im"], bk["at_im"]], axis=-1)

    sg, gpb, t, gc = S5_SG, S5_GPB, S5_T, S5_GROUP
    idx = jnp.arange(S5_UW)
    gl_tgc = (idx // gc) % gpb
    gl_gk = idx // S5_CW
    src = np.arange(S5_UW)
    spread = np.zeros((S5_CW, S5_UW), np.float32)
    spread[(src // LANES) * gc + src % gc, src] = 1.0
    spread = jnp.asarray(spread, BF16)

    def rows_tgc(a):
        k = a.shape[-1]
        return jnp.transpose(a.astype(BF16).reshape(sg, gpb, t, gc, k), (0, 2, 1, 3, 4)).reshape(sg, S5_UW, k)

    def cols_tgc(a):
        return lax.dot_general(a, spread, (((2,), (0,)), ((), ())), preferred_element_type=BF16)

    def keep(rows_gl, cols_gl, a):
        return jnp.where((rows_gl[:, None] == cols_gl[None, :])[None], a, jnp.zeros_like(a))

    p_blk = keep(gl_tgc, gl_gk, jnp.tile(rows_tgc(p_cat), (1, 1, gpb)))
    m_blk = keep(gl_tgc, gl_tgc, cols_tgc(rows_tgc(f["m"] + bk["m"])))
    qf_blk = keep(gl_gk, gl_tgc, cols_tgc(q_f.astype(BF16).reshape(sg, S5_UW, S5_CW)))
    qb_blk = keep(gl_gk, gl_tgc, cols_tgc(q_b.astype(BF16).reshape(sg, S5_UW, S5_CW)))
    return p_blk, m_blk, qf_blk, qb_blk, at_re, at_im


def _s5_x_kernel(u_ref, p_ref, x_ref):
    x_ref[...] = _dot(u_ref[0], p_ref[0])


def _s5_x_call(uf, p_blk, *, rt):
    sg, r, uw = uf.shape
    rt = min(rt, r)
    return pl.pallas_call(
        _s5_x_kernel,
        grid=(sg, r // rt),
        in_specs=[pl.BlockSpec((1, rt, uw), lambda si, i: (si, i, 0)),
                  pl.BlockSpec((1, uw, uw), lambda si, i: (si, 0, 0), pipeline_mode=pl.Buffered(1))],
        out_specs=pl.BlockSpec((rt, uw), lambda si, i: (i, si)),
        out_shape=jax.ShapeDtypeStruct((r, sg * uw), F32),
        compiler_params=_params("parallel", "parallel"),
        name="s5_x",
    )(uf, p_blk)


def _s5_scan_kernel(xf_ref, xb_ref, are_ref, aim_ref, h0_ref, hf_ref, hb_ref, hfin_ref, hr_scr, hi_scr, *, nb):
    i = pl.program_id(0)

    @pl.when(i == 0)
    def _():
        hr_scr[...] = h0_ref[:, 0]
        hi_scr[...] = h0_ref[:, 1]

    ar, ai = are_ref[...][None], aim_ref[...][None]
    fwd = lax.broadcasted_iota(jnp.int32, (1, S5_GROUPS, LANES), 2) < S5_STATE

    def step(j, carry):
        hr, hi = carry
        jb = nb - 1 - j
        xr = jnp.where(fwd, xf_ref[:, j, :, 0:LANES], xb_ref[:, jb, :, 0:LANES])
        xi = jnp.where(fwd, xf_ref[:, j, :, LANES:2 * LANES], xb_ref[:, jb, :, LANES:2 * LANES])
        hf_ref[:, j, :, 0:LANES] = hr
        hf_ref[:, j, :, LANES:2 * LANES] = hi
        hb_ref[:, jb, :, 0:LANES] = hr
        hb_ref[:, jb, :, LANES:2 * LANES] = hi
        return ar * hr - ai * hi + xr, ar * hi + ai * hr + xi

    hr, hi = lax.fori_loop(0, nb, step, (hr_scr[...], hi_scr[...]))
    hr_scr[...] = hr
    hi_scr[...] = hi

    @pl.when(i == pl.num_programs(0) - 1)
    def _():
        hfin_ref[:, 0] = hr
        hfin_ref[:, 1] = hi


def _s5_scan_call(x4, at_re, at_im, h0, *, nb):
    b, n, g, w = x4.shape
    nb = min(nb, n)
    steps = n // nb
    blk = lambda rev: pl.BlockSpec((b, nb, g, w), (lambda i: (0, steps - 1 - i, 0, 0)) if rev
                                   else (lambda i: (0, i, 0, 0)))
    st = pl.BlockSpec((b, 2, g, LANES), lambda i: (0, 0, 0, 0))
    return pl.pallas_call(
        functools.partial(_s5_scan_kernel, nb=nb),
        grid=(steps,),
        in_specs=[blk(False), blk(True), _const_spec((g, LANES)), _const_spec((g, LANES)), st],
        out_specs=[blk(False), blk(True), st],
        out_shape=[jax.ShapeDtypeStruct(x4.shape, F32), jax.ShapeDtypeStruct(x4.shape, F32),
                   jax.ShapeDtypeStruct((b, 2, g, LANES), F32)],
        scratch_shapes=[pltpu.VMEM((b, g, LANES), F32), pltpu.VMEM((b, g, LANES), F32)],
        compiler_params=_params("arbitrary"),
        name="s5_scan",
    )(x4, x4, at_re, at_im, h0)


def _s5_y_kernel(u_ref, hf_ref, hb_ref, m_ref, qf_ref, qb_ref, y_ref):
    y = (_dot(u_ref[0], m_ref[0]) + _dot(hf_ref[...].astype(BF16), qf_ref[0])
         + _dot(hb_ref[...].astype(BF16), qb_ref[0]))
    rt = y.shape[0]
    for t in range(S5_T):
        y_ref[pl.ds(t, rt, stride=S5_T), :] = y[:, t * LANES:(t + 1) * LANES]


def _s5_y_call(uf, hf2, hb2, m_blk, qf_blk, qb_blk, *, rt):
    sg, r, uw = uf.shape
    rt = min(rt, r)
    wspec = pl.BlockSpec((1, uw, uw), lambda si, i: (si, 0, 0), pipeline_mode=pl.Buffered(1))
    hspec = pl.BlockSpec((rt, uw), lambda si, i: (i, si))
    return pl.pallas_call(
        _s5_y_kernel,
        grid=(sg, r // rt),
        in_specs=[pl.BlockSpec((1, rt, uw), lambda si, i: (si, i, 0)), hspec, hspec, wspec, wspec, wspec],
        out_specs=pl.BlockSpec((rt * S5_T, LANES), lambda si, i: (i, si)),
        out_shape=jax.ShapeDtypeStruct((r * S5_T, sg * LANES), F32),
        compiler_params=_params("parallel", "parallel"),
        name="s5_y",
    )(uf, hf2, hb2, m_blk, qf_blk, qb_blk)


def _s5_mix(uf, ops, h0, bsz, *, nb, rt):
    p_blk, m_blk, qf_blk, qb_blk, at_re, at_im = ops
    r = uf.shape[1]
    n = r // bsz
    x = _s5_x_call(uf, p_blk, rt=rt)
    hf, hb, hfin = _s5_scan_call(x.reshape(bsz, n, S5_GROUPS, S5_CW), at_re, at_im, h0, nb=nb)
    y = _s5_y_call(uf, hf.reshape(r, S5_GROUPS * S5_CW), hb.reshape(r, S5_GROUPS * S5_CW),
                   m_blk, qf_blk, qb_blk, rt=rt)
    return y.reshape(bsz, n * S5_T, S5_WIDTH), hfin


def _attn_kernel(q_ref, kt_ref, v_ref, o_ref, m_scr, acc_scr, s0_scr, s1_scr, *, tk, n_kv):
    rows = q_ref.shape[2] * q_ref.shape[3]
    q = q_ref[0, 0].reshape(rows, LANES)
    m_scr[...] = jnp.full(m_scr.shape, -jnp.inf, F32)
    acc_scr[...] = jnp.zeros(acc_scr.shape, F32)

    def scores(j, s_scr):
        off = pl.multiple_of(j * tk, tk)
        s_scr[...] = _dot(q, kt_ref[0, :, pl.ds(off, tk)])

    def softmax_pv(j, s_scr):
        off = pl.multiple_of(j * tk, tk)
        s = s_scr[...]
        m_prev = m_scr[...]
        m_new = jnp.maximum(m_prev, jnp.max(s, axis=1, keepdims=True))
        alpha = jnp.exp2(m_prev - m_new)
        p = jnp.exp2(s - jnp.tile(m_new, (1, tk // LANES))).astype(BF16)
        acc_scr[...] = acc_scr[...] * jnp.tile(alpha, (1, 2)) + _dot(p, v_ref[0, pl.ds(off, tk), :])
        m_scr[...] = m_new

    scores(0, s0_scr)

    def pair(t, carry):
        j = 2 * t
        scores(j + 1, s1_scr)
        softmax_pv(j, s0_scr)
        scores(j + 2, s0_scr)
        softmax_pv(j + 1, s1_scr)
        return carry

    lax.fori_loop(0, (n_kv - 1) // 2, pair, 0)
    if n_kv % 2 == 0:
        scores(n_kv - 1, s1_scr)
        softmax_pv(n_kv - 2, s0_scr)
        softmax_pv(n_kv - 1, s1_scr)
    else:
        softmax_pv(n_kv - 1, s0_scr)
    acc = acc_scr[...]
    out = acc[:, 0:LANES] / acc[:, LANES:2 * LANES]
    o_ref[0, 0] = out.reshape(q_ref.shape[2:]).astype(o_ref.dtype)


def _attn_call(qs, kt, v_ext, *, tq, tk):
    b, kvh, grp, l, _ = qs.shape
    lk = kt.shape[2]
    tq, tk = min(tq, l), min(tk, lk)
    rows = grp * tq
    blk = pl.BlockSpec((1, 1, grp, tq, LANES), lambda bi, ki, i: (bi, ki, 0, i, 0))
    return pl.pallas_call(
        functools.partial(_attn_kernel, tk=tk, n_kv=lk // tk),
        grid=(b, kvh, l // tq),
        in_specs=[blk,
                  pl.BlockSpec((1, LANES, lk), lambda bi, ki, i: (bi, 0, 0), pipeline_mode=pl.Buffered(1)),
                  pl.BlockSpec((1, lk, 2 * LANES), lambda bi, ki, i: (bi, 0, 0), pipeline_mode=pl.Buffered(1))],
        out_specs=blk,
        out_shape=jax.ShapeDtypeStruct(qs.shape, BF16),
        scratch_shapes=[pltpu.VMEM((rows, LANES), F32), pltpu.VMEM((rows, 2 * LANES), F32),
                        pltpu.VMEM((rows, tk), F32), pltpu.VMEM((rows, tk), F32)],
        compiler_params=_params("parallel", "parallel", "arbitrary"),
        name="attention",
    )(qs, kt, v_ext)


def _kv_layout(ak, av):
    kt = jnp.transpose(ak, (0, 2, 1))
    v_ext = jnp.concatenate([av, jnp.ones_like(av)], axis=-1)
    return kt, v_ext


def _merge_kernel(h_ref, mod_ref, gn_ref, yg_ref, ys_ref, u_ref, ya_ref, wbg_ref, bbg_ref, wp_ref, wpa_ref,
                  wo_ref, d_ref, gw_ref, gb_ref, o_ref):
    h = h_ref[0]
    d = h.shape[-1]
    n = _rms_mod(h, gn_ref[...], mod_ref[0, 0:1], mod_ref[0, 1:2]).astype(BF16)
    g = jax.nn.sigmoid(_dot(n, wbg_ref[...]) + bbg_ref[...])
    y = jax.nn.gelu(ys_ref[0] + d_ref[...] * u_ref[0])
    y_s5 = y * jax.nn.sigmoid(_dot(y.astype(BF16), gw_ref[...]) + gb_ref[...])
    ya = jnp.concatenate([ya_ref[0, kv, j] for kv in range(ATT_KV_HEADS) for j in range(ATT_GROUP)], axis=-1)
    m = (g[:, 0:d] * _dot(yg_ref[0].astype(BF16), wp_ref[0])
         + g[:, d:2 * d] * _dot(y_s5.astype(BF16), wp_ref[1])
         + g[:, 2 * d:3 * d] * _dot(ya, wpa_ref[...]))
    o_ref[0] = h + mod_ref[0, 2:3] * _dot(m.astype(BF16), wo_ref[...])


def _merge_call(h, mod3, g_norm, y_gla, y_s5raw, u, y_att, lw, *, tm):
    b, l, d = h.shape
    tm = min(tm, l)
    tok = lambda w: pl.BlockSpec((1, tm, w), lambda bi, i: (bi, i, 0))
    bw = y_gla.shape[-1]
    att = pl.BlockSpec((1, ATT_KV_HEADS, ATT_GROUP, tm, LANES), lambda bi, i: (bi, 0, 0, i, 0))
    return pl.pallas_call(
        _merge_kernel,
        grid=(b, l // tm),
        in_specs=[tok(d), pl.BlockSpec((1, 3, d), lambda bi, i: (bi, 0, 0)), _const_spec((1, d)),
                  tok(bw), tok(bw), tok(bw), att,
                  _const_spec((d, 3 * d)), _const_spec((1, 3 * d)), _const_spec((2, bw, d)),
                  _const_spec((ATT_HEADS * LANES, d)), _const_spec((d, d)),
                  _const_spec((1, bw)), _const_spec((bw, bw)), _const_spec((1, bw))],
        out_specs=tok(d),
        out_shape=jax.ShapeDtypeStruct((b, l, d), F32),
        compiler_params=_params("parallel", "parallel"),
        name="merge",
    )(h, mod3, g_norm.reshape(1, d), y_gla, y_s5raw, u, y_att, lw["w_bgate"], lw["b_bgate"], lw["w_bproj"],
      lw["w_aproj"], lw["w_out"], lw["s5_d"], lw["glu_w"], lw["glu_b"])


def _rope_tables(n_tokens):
    rows = n_tokens // GRID_W
    row = jnp.repeat(jnp.arange(rows, dtype=F32), GRID_W)
    col = jnp.tile(jnp.arange(GRID_W, dtype=F32), rows)
    inv = ROPE_THETA ** (-jnp.arange(ROPE_PAIRS, dtype=F32) / ROPE_PAIRS)
    ang_r, ang_c = row[:, None] * inv, col[:, None] * inv
    zero = jnp.zeros_like(ang_r)
    cos = jnp.concatenate([jnp.cos(ang_r), jnp.cos(ang_r), jnp.cos(ang_c), jnp.cos(ang_c)], axis=-1)
    sin_a = jnp.concatenate([-jnp.sin(ang_r), zero, -jnp.sin(ang_c), zero], axis=-1)
    sin_b = jnp.concatenate([zero, jnp.sin(ang_r), zero, jnp.sin(ang_c)], axis=-1)
    tile = lambda t: jnp.tile(t, (1, ATT_HEADS))
    return tile(cos), tile(sin_a), tile(sin_b)


def _block_diag_ones(width, seg):
    idx = np.arange(width) // seg
    return jnp.asarray((idx[:, None] == idx[None, :]).astype(np.float32), BF16)


def _layer_weights(i, w_in, gla_gate_w, gla_gate_b, attn_q_norm_w, attn_k_norm_w, w_branch_gate,
                   b_branch_gate, w_branch_proj, w_out, s5_d, s5_glu_w, s5_glu_b):
    offs = np.concatenate([[0], np.cumsum(IN_WIDTHS)])
    col = lambda k: w_in[i][:, offs[k]:offs[k + 1]]
    w_main = jnp.concatenate([col(0), col(1), col(2), col(5), col(6), col(7), col(8), col(9)], axis=1)
    d = w_in.shape[1]
    rk = GLA_GATE_RANK
    w_gz = jnp.zeros((d, LANES), F32).at[:, 0:rk].set(col(3)).at[:, rk:2 * rk].set(col(4))
    gw = jnp.zeros((LANES, 2 * GLA_QK), F32)
    gw = gw.at[0:rk, 0:GLA_QK].set(gla_gate_w[i, 0]).at[rk:2 * rk, GLA_QK:].set(gla_gate_w[i, 1])
    wa = w_branch_proj[i, 2].reshape(ATT_KV_HEADS, ATT_GROUP, ATT_HEAD_DIM, d)
    za = jnp.zeros_like(wa[0])
    w_aproj = jnp.concatenate([jnp.concatenate([wa[0], za], axis=1), jnp.concatenate([za, wa[1]], axis=1)],
                              axis=0).reshape(ATT_HEADS * LANES, d)
    return dict(
        w_aproj=w_aproj.astype(BF16),
        w_main=w_main.astype(BF16), w_gz=w_gz.astype(BF16), gate_w_bd=gw.astype(BF16),
        gate_b=gla_gate_b[i].reshape(1, 2 * GLA_QK),
        q_norm_w=(jnp.tile(attn_q_norm_w[i], ATT_HEADS) * (ATT_HEAD_DIM ** -0.5 * LOG2_E)).reshape(1, ATT_Q),
        k_norm_w=jnp.tile(attn_k_norm_w[i], ATT_KV_HEADS).reshape(1, ATT_KV),
        bd_q=_block_diag_ones(ATT_Q, ATT_HEAD_DIM), bd_k=_block_diag_ones(ATT_KV, ATT_HEAD_DIM),
        w_bgate=w_branch_gate[i].astype(BF16), b_bgate=b_branch_gate[i].reshape(1, -1),
        w_bproj=w_branch_proj[i, 0:2].astype(BF16), w_out=w_out[i].astype(BF16),
        s5_d=s5_d[i].reshape(1, -1), glu_w=s5_glu_w[i].astype(BF16), glu_b=s5_glu_b[i].reshape(1, -1),
    )


def _mixer_inputs(h, mod, g_norm, lw, tables, *, rope, tm):
    mod2 = mod[:, 3:5]
    return _inproj_call(h, mod2, g_norm, lw, tables, rope=rope, tm=tm)


def kernel(x, c, ctx, c_ctx, w_ada, b_ada, norm_w, w_ffn_gate, w_ffn_up, w_ffn_down, w_in, gla_gate_w,
           gla_gate_b, gla_norm_w, s5_a_re, s5_a_im, s5_log_dt, s5_b_re, s5_b_im, s5_c_re, s5_c_im, s5_d,
           s5_glu_w, s5_glu_b, attn_q_norm_w, attn_k_norm_w, w_branch_gate, b_branch_gate, w_branch_proj,
           w_out, final_norm_w):
    bsz, seq, d = x.shape
    depth = w_ada.shape[0]
    tables = _rope_tables(seq)
    ctx_tables = tuple(t[0:ctx.shape[1]] for t in tables)
    cond = jnp.zeros((8, d), F32).at[0:bsz].set(c).at[bsz].set(c_ctx)
    tm_ffn, tm_in, tm_merge, tb_gla, nb_s5, rt_s5, tq, tk = (
        TILES[k] for k in ("ffn", "inproj", "merge", "gla", "s5_scan", "s5_rows", "att_q", "att_k"))

    ffn_w = (w_ffn_gate.astype(BF16), w_ffn_up.astype(BF16), w_ffn_down.astype(BF16))
    h_lat, h_ctx = x, ctx
    for i in range(depth):
        last = i == depth - 1
        mods = _ada_call(cond, w_ada, b_ada, i)
        mod_l = mods[0:bsz].reshape(bsz, N_MOD, d)
        mod_c = jnp.broadcast_to(mods[bsz].reshape(1, N_MOD, d), (bsz, N_MOD, d))
        lw = _layer_weights(i, w_in, gla_gate_w, gla_gate_b, attn_q_norm_w, attn_k_norm_w, w_branch_gate,
                            b_branch_gate, w_branch_proj, w_out, s5_d, s5_glu_w, s5_glu_b)

        def ffn(h, mod, j, fin=False):
            return _ffn_call(h, mod[:, 6 * j:6 * j + 3], norm_w[i, 2 * j], *ffn_w, final_norm_w,
                             layer=i, sub=j, final_norm=fin, tm=tm_ffn)

        h_lat = ffn(h_lat, mod_l, 0)
        h_ctx = ffn(h_ctx, mod_c, 0)

        qk_c, v_c, r_c, lg_c, u_c, uf_c, aq_c, ak_c, av_c = _mixer_inputs(h_ctx, mod_c, norm_w[i, 1], lw,
                                                                          ctx_tables, rope=False, tm=tm_in)
        qk_l, v_l, r_l, lg_l, u_l, uf_l, aq_l, ak_l, av_l = _mixer_inputs(h_lat, mod_l, norm_w[i, 1], lw,
                                                                          tables, rope=True, tm=tm_in)
        s0 = jnp.zeros((bsz, GLA_QK, GLA_DV), F32)
        yg_c, sf_c, sb_c = _gla_bidir(qk_c, v_c, lg_c, r_c, gla_norm_w[i], s0, s0, tb=tb_gla)
        yg_l, _, _ = _gla_bidir(qk_l, v_l, lg_l, r_l, gla_norm_w[i], sf_c, sb_c, tb=tb_gla)
        ops = _s5_operators(s5_a_re[i], s5_a_im[i], s5_log_dt[i], s5_b_re[i], s5_b_im[i], s5_c_re[i], s5_c_im[i])
        h0 = jnp.zeros((bsz, 2, S5_GROUPS, LANES), F32)
        ys_c, hfin_c = _s5_mix(uf_c, ops, h0, bsz, nb=nb_s5, rt=rt_s5)
        ys_l, _ = _s5_mix(uf_l, ops, hfin_c, bsz, nb=nb_s5, rt=rt_s5)
        kt, v_ext = _kv_layout(jnp.concatenate([ak_c, ak_l], axis=1), jnp.concatenate([av_c, av_l], axis=1))
        ya_l = _attn_call(aq_l, kt, v_ext, tq=tq, tk=tk)
        h_lat = _merge_call(h_lat, mod_l[:, 3:6], norm_w[i, 1], yg_l, ys_l, u_l, ya_l, lw, tm=tm_merge)
        h_lat = ffn(h_lat, mod_l, 1, last)
        if not last:
            kt_c, v_ext_c = _kv_layout(ak_c, av_c)
            ya_c = _attn_call(aq_c, kt_c, v_ext_c, tq=tq, tk=tk)
            h_ctx = _merge_call(h_ctx, mod_c[:, 3:6], norm_w[i, 1], yg_c, ys_c, u_c, ya_c, lw, tm=tm_merge)
            h_ctx = ffn(h_ctx, mod_c, 1)
    return h_lat
```

```python
import functools

import jax
import jax.numpy as jnp
import numpy as np
from jax import lax
from jax.experimental import pallas as pl
from jax.experimental.pallas import tpu as pltpu

F32 = jnp.float32
BF16 = jnp.bfloat16

N_MOD = 9
MACARON_WEIGHT = 0.5
NORM_EPS = 1e-6
GRID_W = 64
GLA_HEADS = 4
GLA_DK = 64
GLA_DV = 128
GLA_QK = GLA_HEADS * GLA_DK
GLA_V = GLA_HEADS * GLA_DV
GLA_GATE_RANK = 16
GLA_TAU = 16.0
GLA_CHUNK = 64
S5_WIDTH = 512
S5_GROUP = 16
S5_GROUPS = S5_WIDTH // S5_GROUP
S5_STATE = 64
ATT_HEADS = 8
ATT_KV_HEADS = 2
ATT_GROUP = ATT_HEADS // ATT_KV_HEADS
ATT_HEAD_DIM = 64
ATT_Q = ATT_HEADS * ATT_HEAD_DIM
ATT_KV = ATT_KV_HEADS * ATT_HEAD_DIM
ROPE_PAIRS = ATT_HEAD_DIM // 4
ROPE_THETA = 10000.0
IN_WIDTHS = (GLA_QK, GLA_QK, GLA_V, GLA_GATE_RANK, GLA_GATE_RANK, GLA_V, S5_WIDTH, ATT_Q, ATT_KV, ATT_KV)

LANES = 128
MXU_DIM = 256
VMEM_LIMIT = 56 * 1024 * 1024

LOG2_E = 1.4426950408889634

S5_T = MXU_DIM // S5_GROUP
S5_CW = S5_T * S5_GROUP
S5_GPB = LANES // S5_GROUP
S5_SG = S5_GROUPS // S5_GPB
S5_UW = S5_T * LANES

TILES = dict(ffn=512, inproj=512, merge=512, gla=512, s5_scan=32, s5_rows=256, att_q=128, att_k=1280,
             att_streams=2)


def _params(*sem):
    return pltpu.CompilerParams(dimension_semantics=sem, vmem_limit_bytes=VMEM_LIMIT)


def _const_spec(shape):
    nd = len(shape)
    return pl.BlockSpec(shape, lambda *_: (0,) * nd, pipeline_mode=pl.Buffered(1))


def _dotg(a, b, contract):
    return lax.dot_general(a, b, (((contract[0],), (contract[1],)), ((), ())),
                           precision=lax.Precision.DEFAULT, preferred_element_type=F32)


def _dot(a, b):
    return _dotg(a, b, (1, 0))


def _dot_hi_lo(x, w):
    hi = x.astype(BF16)
    lo = (x - hi.astype(F32)).astype(BF16)
    return _dot(hi, w) + _dot(lo, w)


def _rms_mod(h, g_norm, shift, scale):
    ms = jnp.mean(h * h, axis=-1, keepdims=True)
    y = h * lax.rsqrt(ms + NORM_EPS) * g_norm
    return y * (1.0 + scale) + shift


def _log_sigmoid(x):
    return jnp.minimum(x, 0.0) - jnp.log(1.0 + jnp.exp(-jnp.abs(x)))


def _ada_kernel(c_ref, w_ref, b_ref, o_ref):
    c = c_ref[...]
    sc = (c * jax.nn.sigmoid(c)).astype(BF16)
    o_ref[...] = _dot(sc, w_ref[...].astype(BF16)) + b_ref[...]


def _ada_call(cond, w_all, b_all, layer):
    rows, d = cond.shape
    depth, _, n = w_all.shape
    tn = 9 * LANES
    return pl.pallas_call(
        _ada_kernel,
        grid=(n // tn,),
        in_specs=[pl.BlockSpec((rows, d), lambda j: (0, 0)),
                  pl.BlockSpec((None, d, tn), lambda j: (layer, 0, j)),
                  pl.BlockSpec((None, 1, tn), lambda j: (layer, 0, j))],
        out_specs=pl.BlockSpec((rows, tn), lambda j: (0, j)),
        out_shape=jax.ShapeDtypeStruct((rows, n), F32),
        compiler_params=_params("arbitrary"),
        name="adaln",
    )(cond, w_all, b_all.reshape(depth, 1, n))


def _ffn_kernel(h_ref, mod_ref, gn_ref, wg_ref, wu_ref, wd_ref, fn_ref, o_ref, *, final_norm):
    h = h_ref[0]
    shift, scale, gate = mod_ref[0, 0:1], mod_ref[0, 1:2], mod_ref[0, 2:3]
    n = _rms_mod(h, gn_ref[...], shift, scale).astype(BF16)
    g = _dot(n, wg_ref[...])
    u = _dot(n, wu_ref[...])
    a = (g * jax.nn.sigmoid(g) * u).astype(BF16)
    y = h + (MACARON_WEIGHT * gate) * _dot(a, wd_ref[...])
    if final_norm:
        ms = jnp.mean(y * y, axis=-1, keepdims=True)
        y = y * lax.rsqrt(ms + NORM_EPS) * fn_ref[...]
    o_ref[0] = y


def _ffn_call(h, mod3, g_norm, wg, wu, wd, final_w, *, layer, sub, final_norm, tm):
    b, l, d = h.shape
    f = wg.shape[-1]
    tm = min(tm, l)
    wspec = lambda r, c: pl.BlockSpec((None, None, r, c), lambda bi, i: (layer, sub, 0, 0),
                                      pipeline_mode=pl.Buffered(1))
    return pl.pallas_call(
        functools.partial(_ffn_kernel, final_norm=final_norm),
        grid=(b, l // tm),
        in_specs=[pl.BlockSpec((1, tm, d), lambda bi, i: (bi, i, 0)),
                  pl.BlockSpec((1, 3, d), lambda bi, i: (bi, 0, 0)),
                  _const_spec((1, d)), wspec(d, f), wspec(d, f), wspec(f, d),
                  _const_spec((1, d))],
        out_specs=pl.BlockSpec((1, tm, d), lambda bi, i: (bi, i, 0)),
        out_shape=jax.ShapeDtypeStruct((b, l, d), F32),
        compiler_params=_params("parallel", "parallel"),
        name="ffn",
    )(h, mod3, g_norm.reshape(1, d), wg, wu, wd, final_w.reshape(1, d))


def _inproj_kernel(h_ref, mod_ref, gn_ref, wm_ref, wgz_ref, gw_ref, gb_ref, qw_ref, kw_ref,
                   bdq_ref, bdk_ref, cos_ref, sa_ref, sb_ref,
                   qk_ref, v_ref, r_ref, lg_ref, u_ref, uf_ref, aq_ref, ak_ref, av_ref, us_scr, *, rope):
    h = h_ref[0]
    tm = h.shape[0]
    n = _rms_mod(h, gn_ref[...], mod_ref[0, 0:1], mod_ref[0, 1:2]).astype(BF16)
    z = _dot(n, wm_ref[...])
    o = 0
    q = z[:, o:o + GLA_QK] * (GLA_DK ** -0.5); o += GLA_QK
    k = z[:, o:o + GLA_QK]; o += GLA_QK
    qk_ref[0, :, 0:GLA_QK] = q
    qk_ref[0, :, GLA_QK:2 * GLA_QK] = k
    v_ref[0] = z[:, o:o + GLA_V]; o += GLA_V
    r_ref[0] = z[:, o:o + GLA_V]; o += GLA_V
    u_ref[0] = z[:, o:o + S5_WIDTH]; o += S5_WIDTH
    for sg in range(S5_SG):
        us_scr[sg] = z[:, o - S5_WIDTH + sg * LANES:o - S5_WIDTH + (sg + 1) * LANES]
        for t in range(S5_T):
            piece = us_scr[sg, pl.ds(t, tm // S5_T, stride=S5_T), :]
            uf_ref[sg, :, t * LANES:(t + 1) * LANES] = piece.astype(BF16)
    xq = z[:, o:o + ATT_Q]; o += ATT_Q
    xk = z[:, o:o + ATT_KV]; o += ATT_KV
    av_ref[0] = z[:, o:o + ATT_KV].astype(BF16)

    zg = _dot(n, wgz_ref[...]).astype(BF16)
    lg_ref[0] = _log_sigmoid(_dot(zg, gw_ref[...]) + gb_ref[...]) * (1.0 / GLA_TAU)

    def head_norm(x, bd_ref, w_ref):
        ss = _dot_hi_lo(x * x, bd_ref[...])
        return x * lax.rsqrt(ss * (1.0 / ATT_HEAD_DIM) + NORM_EPS) * w_ref[...]

    def rotary(x, width):
        if not rope:
            return x
        c, sa, sb = cos_ref[:, 0:width], sa_ref[:, 0:width], sb_ref[:, 0:width]
        return (x * c + pltpu.roll(x, width - ROPE_PAIRS, 1) * sa + pltpu.roll(x, ROPE_PAIRS, 1) * sb)

    ak_ref[0] = rotary(head_norm(xk, bdk_ref, kw_ref), ATT_KV).astype(BF16)
    aq = rotary(head_norm(xq, bdq_ref, qw_ref), ATT_Q)
    lane = lax.broadcasted_iota(jnp.int32, (tm, LANES), 1)
    for kv in range(ATT_KV_HEADS):
        keep = (lane >= kv * ATT_HEAD_DIM) & (lane < (kv + 1) * ATT_HEAD_DIM)
        for j in range(ATT_GROUP):
            hd = kv * ATT_GROUP + j
            blk = aq[:, (hd // 2) * LANES:(hd // 2 + 1) * LANES]
            if hd % 2 != kv:
                blk = pltpu.roll(blk, ATT_HEAD_DIM, 1)
            aq_ref[0, kv, j] = jnp.where(keep, blk, 0.0).astype(BF16)


def _inproj_call(h, mod2, g_norm, lw, tables, *, rope, tm):
    b, l, d = h.shape
    tm = min(tm, l)
    tok = lambda w: pl.BlockSpec((1, tm, w), lambda bi, i: (bi, i, 0))
    tab = pl.BlockSpec((tm, ATT_Q), lambda bi, i: (i, 0))
    wm = lw["w_main"]
    nt = l // tm
    tok_out = lambda w, dt: (tok(w), jax.ShapeDtypeStruct((b, l, w), dt))
    outs = [tok_out(2 * GLA_QK, F32), tok_out(GLA_V, F32), tok_out(GLA_V, F32), tok_out(2 * GLA_QK, F32),
            tok_out(S5_WIDTH, F32),
            (pl.BlockSpec((S5_SG, tm // S5_T, S5_UW), lambda bi, i: (0, bi * nt + i, 0)),
             jax.ShapeDtypeStruct((S5_SG, b * l // S5_T, S5_UW), BF16)),
            (pl.BlockSpec((1, ATT_KV_HEADS, ATT_GROUP, tm, LANES), lambda bi, i: (bi, 0, 0, i, 0)),
             jax.ShapeDtypeStruct((b, ATT_KV_HEADS, ATT_GROUP, l, LANES), BF16)),
            tok_out(ATT_KV, BF16), tok_out(ATT_KV, BF16)]
    return pl.pallas_call(
        functools.partial(_inproj_kernel, rope=rope),
        grid=(b, nt),
        in_specs=[tok(d), pl.BlockSpec((1, 2, d), lambda bi, i: (bi, 0, 0)), _const_spec((1, d)),
                  _const_spec(wm.shape), _const_spec((d, LANES)), _const_spec((LANES, 2 * GLA_QK)),
                  _const_spec((1, 2 * GLA_QK)), _const_spec((1, ATT_Q)), _const_spec((1, ATT_KV)),
                  _const_spec((ATT_Q, ATT_Q)), _const_spec((ATT_KV, ATT_KV)), tab, tab, tab],
        out_specs=[s for s, _ in outs],
        out_shape=[t for _, t in outs],
        scratch_shapes=[pltpu.VMEM((S5_SG, tm, LANES), F32)],
        compiler_params=_params("parallel", "parallel"),
        name="inproj",
    )(h, mod2, g_norm.reshape(1, d), wm, lw["w_gz"], lw["gate_w_bd"], lw["gate_b"], lw["q_norm_w"],
      lw["k_norm_w"], lw["bd_q"], lw["bd_k"], *tables)


def _gla_kernel(qk_ref, v_ref, lg_ref, s0_ref, tri_ref, of_ref, r_ref, nw_ref, o_ref, sfin_ref, s_scr,
                *, reverse, readout, n_chunks, n_batch):
    i = pl.program_id(0)

    @pl.when(i == 0)
    def _():
        s_scr[...] = s0_ref[...]

    tri = tri_ref[...]
    c, nh = GLA_CHUNK, GLA_HEADS
    row = lax.broadcasted_iota(jnp.int32, (nh * c, c), 0) % c
    col = lax.broadcasted_iota(jnp.int32, (nh * c, c), 1)
    keep = (col >= row) if reverse else (col <= row)
    head_of_lane = lax.broadcasted_iota(jnp.int32, (c, GLA_QK), 1) // GLA_DK

    def chunk(bi, ci):
        rows = slice(ci * c, (ci + 1) * c)
        g = lg_ref[bi, rows, :]
        g_hi = g.astype(BF16)
        b = _dot(tri, g_hi) + _dot(tri, (g - g_hi.astype(F32)).astype(BF16))
        b_last = b[0:1, :] if reverse else b[c - 1:c, :]
        q_in = qk_ref[bi, rows, 0:GLA_QK] * jnp.exp(b)
        k = qk_ref[bi, rows, GLA_QK:2 * GLA_QK]
        k_in = (k * jnp.exp(-b)).astype(BF16)
        k_out = (k * jnp.exp(b_last - b)).astype(BF16)
        dec = jnp.exp(jnp.broadcast_to(b_last, (LANES, GLA_QK)).T)
        v = v_ref[bi, rows, :].astype(BF16)
        s_old = s_scr[bi]
        q_st = jnp.concatenate([jnp.where(head_of_lane == hd, q_in, 0.0) for hd in range(nh)],
                               axis=0).astype(BF16)
        att = _dotg(q_st, k_in, (1, 1))
        att = jnp.where(keep, att, 0.0).astype(BF16)
        o_intra = _dot(att, v)
        o_inter = _dot(q_st, s_old.astype(BF16))
        ds = _dotg(k_out, v, (0, 0))
        for hd in range(nh):
            tok = slice(hd * c, (hd + 1) * c)
            vcols = slice(hd * GLA_DV, (hd + 1) * GLA_DV)
            o_h = o_intra[tok, vcols] + o_inter[tok]
            if readout:
                o_h = o_h + of_ref[bi, rows, vcols]
                ms = jnp.mean(o_h * o_h, axis=-1, keepdims=True)
                y = o_h * lax.rsqrt(ms + NORM_EPS) * nw_ref[...]
                rr = r_ref[bi, rows, vcols]
                o_h = y * (rr * jax.nn.sigmoid(rr))
            o_ref[bi, rows, vcols] = o_h
        d_state = jnp.concatenate([ds[hd * GLA_DK:(hd + 1) * GLA_DK, hd * GLA_DV:(hd + 1) * GLA_DV]
                                   for hd in range(nh)], axis=0)
        s_scr[bi] = dec * s_old + d_state

    for ci in (range(n_chunks - 1, -1, -1) if reverse else range(n_chunks)):
        for bi in range(n_batch):
            chunk(bi, ci)

    @pl.when(i == pl.num_programs(0) - 1)
    def _():
        sfin_ref[...] = s_scr[...]


def _gla_call(qk, v, lg, s0, o_f, r, norm_w, *, reverse, readout, tb):
    b, l, _ = v.shape
    tb = min(tb, l)
    nb = l // tb
    idx = (lambda i: (0, nb - 1 - i, 0)) if reverse else (lambda i: (0, i, 0))
    tok = lambda w: pl.BlockSpec((b, tb, w), idx)
    lg_off = 1 if reverse else 0
    lg_spec = pl.BlockSpec((b, tb, GLA_QK), (lambda i: (0, nb - 1 - i, lg_off)) if reverse
                           else (lambda i: (0, i, lg_off)))
    state = pl.BlockSpec((b, GLA_QK, GLA_DV), lambda i: (0, 0, 0))
    ones = np.triu(np.ones((GLA_CHUNK, GLA_CHUNK), np.float32)) if reverse else \
        np.tril(np.ones((GLA_CHUNK, GLA_CHUNK), np.float32))
    return pl.pallas_call(
        functools.partial(_gla_kernel, reverse=reverse, readout=readout, n_chunks=tb // GLA_CHUNK, n_batch=b),
        grid=(nb,),
        in_specs=[tok(2 * GLA_QK), tok(GLA_V), lg_spec, state, _const_spec((GLA_CHUNK, GLA_CHUNK)),
                  tok(GLA_V), tok(GLA_V), _const_spec((1, GLA_DV))],
        out_specs=[tok(GLA_V), state],
        out_shape=[jax.ShapeDtypeStruct((b, l, GLA_V), F32), jax.ShapeDtypeStruct((b, GLA_QK, GLA_DV), F32)],
        scratch_shapes=[pltpu.VMEM((b, GLA_QK, GLA_DV), F32)],
        compiler_params=_params("arbitrary"),
        name="gla_bwd" if reverse else "gla_fwd",
    )(qk, v, lg, s0, jnp.asarray(ones, BF16), o_f, r, norm_w.reshape(1, GLA_DV))


def _gla_bidir(qk, v, lg, r, norm_w, s0_f, s0_b, *, tb):
    o_f, s_f = _gla_call(qk, v, lg, s0_f, v, r, norm_w, reverse=False, readout=False, tb=tb)
    y, s_b = _gla_call(qk, v, lg, s0_b, o_f, r, norm_w, reverse=True, readout=True, tb=tb)
    return y, s_f, s_b


def _s5_operators(a_re, a_im, log_dt, b_re, b_im, c_re, c_im):
    hp = lax.Precision.HIGHEST
    t = S5_T
    cr, ci_ = c_re.astype(F32), c_im.astype(F32)
    ks = jnp.arange(t + 1, dtype=F32)[:, None, None]
    per_dir = []
    for d in range(2):
        dt = jnp.exp(log_dt[d].astype(F32))[:, None]
        ar, ai = a_re[d].astype(F32), a_im[d].astype(F32)
        mag = jnp.exp(ks * dt * ar)
        pw_re, pw_im = mag * jnp.cos(ks * dt * ai), mag * jnp.sin(ks * dt * ai)
        den = ar * ar + ai * ai
        xr, xi = pw_re[1] - 1.0, pw_im[1]
        coef_re = (xr * ar + xi * ai) / den
        coef_im = (xi * ar - xr * ai) / den
        br, bi = b_re.astype(F32), b_im.astype(F32)
        bb_re = coef_re[..., None] * br - coef_im[..., None] * bi
        bb_im = coef_re[..., None] * bi + coef_im[..., None] * br
        ca_re = cr[None] * pw_re[:, :, None, :] - ci_[None] * pw_im[:, :, None, :]
        ca_im = cr[None] * pw_im[:, :, None, :] + ci_[None] * pw_re[:, :, None, :]
        kk = (jnp.einsum("kgop,gpi->kgoi", ca_re, bb_re, precision=hp)
              - jnp.einsum("kgop,gpi->kgoi", ca_im, bb_im, precision=hp))
        s_idx = jnp.arange(t)[:, None]
        t_idx = jnp.arange(t)[None, :]
        lag = (t_idx - s_idx) if d == 0 else (s_idx - t_idx)
        toe = jnp.where((lag >= 0)[:, :, None, None, None], kk[jnp.clip(lag, 0, t)], 0.0)
        m = jnp.transpose(toe, (2, 0, 4, 1, 3)).reshape(S5_GROUPS, S5_CW, S5_CW)
        p_pow = (t - 1 - jnp.arange(t)) if d == 0 else jnp.arange(t)
        pr, pi = pw_re[p_pow], pw_im[p_pow]
        p_re = pr[:, :, :, None] * bb_re[None] - pi[:, :, :, None] * bb_im[None]
        p_im = pr[:, :, :, None] * bb_im[None] + pi[:, :, :, None] * bb_re[None]
        flat_p = lambda x: jnp.transpose(x, (1, 0, 3, 2)).reshape(S5_GROUPS, S5_CW, S5_STATE)
        q_pow = (jnp.arange(t) + 1) if d == 0 else (t - jnp.arange(t))
        q_re, q_im = ca_re[q_pow], -ca_im[q_pow]
        flat_q = lambda x: jnp.transpose(x, (1, 3, 0, 2)).reshape(S5_GROUPS, S5_STATE, S5_CW)
        per_dir.append(dict(m=m, p_re=flat_p(p_re), p_im=flat_p(p_im), q_re=flat_q(q_re), q_im=flat_q(q_im),
                            at_re=pw_re[t], at_im=pw_im[t]))
    f, bk = per_dir
    zq = jnp.zeros_like(f["q_re"])
    p_cat = jnp.concatenate([f["p_re"], bk["p_re"], f["p_im"], bk["p_im"]], axis=-1)
    q_f = jnp.concatenate([f["q_re"], zq, f["q_im"], zq], axis=1)
    q_b = jnp.concatenate([zq, bk["q_re"], zq, bk["q_im"]], axis=1)
    at_re = jnp.concatenate([f["at_re"], bk["at_re"]], axis=-1)
    at_im = jnp.concatenate([f["at_im"], bk["at_im"]], axis=-1)

    sg, gpb, t, gc = S5_SG, S5_GPB, S5_T, S5_GROUP
    idx = jnp.arange(S5_UW)
    gl_tgc = (idx // gc) % gpb
    gl_gk = idx // S5_CW
    src = np.arange(S5_UW)
    spread = np.zeros((S5_CW, S5_UW), np.float32)
    spread[(src // LANES) * gc + src % gc, src] = 1.0
    spread = jnp.asarray(spread, BF16)

    def rows_tgc(a):
        k = a.shape[-1]
        return jnp.transpose(a.astype(BF16).reshape(sg, gpb, t, gc, k), (0, 2, 1, 3, 4)).reshape(sg, S5_UW, k)

    def cols_tgc(a):
        return lax.dot_general(a, spread, (((2,), (0,)), ((), ())), preferred_element_type=BF16)

    def keep(rows_gl, cols_gl, a):
        return jnp.where((rows_gl[:, None] == cols_gl[None, :])[None], a, jnp.zeros_like(a))

    p_blk = keep(gl_tgc, gl_gk, jnp.tile(rows_tgc(p_cat), (1, 1, gpb)))
    m_blk = keep(gl_tgc, gl_tgc, cols_tgc(rows_tgc(f["m"] + bk["m"])))
    qf_blk = keep(gl_gk, gl_tgc, cols_tgc(q_f.astype(BF16).reshape(sg, S5_UW, S5_CW)))
    qb_blk = keep(gl_gk, gl_tgc, cols_tgc(q_b.astype(BF16).reshape(sg, S5_UW, S5_CW)))
    return p_blk, m_blk, qf_blk, qb_blk, at_re, at_im


def _s5_x_kernel(u_ref, p_ref, x_ref):
    x_ref[...] = _dot(u_ref[0], p_ref[0])


def _s5_x_call(uf, p_blk, *, rt):
    sg, r, uw = uf.shape
    rt = min(rt, r)
    return pl.pallas_call(
        _s5_x_kernel,
        grid=(sg, r // rt),
        in_specs=[pl.BlockSpec((1, rt, uw), lambda si, i: (si, i, 0)),
                  pl.BlockSpec((1, uw, uw), lambda si, i: (si, 0, 0), pipeline_mode=pl.Buffered(1))],
        out_specs=pl.BlockSpec((rt, uw), lambda si, i: (i, si)),
        out_shape=jax.ShapeDtypeStruct((r, sg * uw), F32),
        compiler_params=_params("parallel", "parallel"),
        name="s5_x",
    )(uf, p_blk)


def _s5_scan_kernel(xf_ref, xb_ref, are_ref, aim_ref, h0_ref, hf_ref, hb_ref, hfin_ref, hr_scr, hi_scr, *, nb):
    i = pl.program_id(0)

    @pl.when(i == 0)
    def _():
        hr_scr[...] = h0_ref[:, 0]
        hi_scr[...] = h0_ref[:, 1]

    ar, ai = are_ref[...][None], aim_ref[...][None]
    fwd = lax.broadcasted_iota(jnp.int32, (1, S5_GROUPS, LANES), 2) < S5_STATE

    def step(j, carry):
        hr, hi = carry
        jb = nb - 1 - j
        xr = jnp.where(fwd, xf_ref[:, j, :, 0:LANES], xb_ref[:, jb, :, 0:LANES])
        xi = jnp.where(fwd, xf_ref[:, j, :, LANES:2 * LANES], xb_ref[:, jb, :, LANES:2 * LANES])
        hf_ref[:, j, :, 0:LANES] = hr
        hf_ref[:, j, :, LANES:2 * LANES] = hi
        hb_ref[:, jb, :, 0:LANES] = hr
        hb_ref[:, jb, :, LANES:2 * LANES] = hi
        return ar * hr - ai * hi + xr, ar * hi + ai * hr + xi

    hr, hi = lax.fori_loop(0, nb, step, (hr_scr[...], hi_scr[...]))
    hr_scr[...] = hr
    hi_scr[...] = hi

    @pl.when(i == pl.num_programs(0) - 1)
    def _():
        hfin_ref[:, 0] = hr
        hfin_ref[:, 1] = hi


def _s5_scan_call(x4, at_re, at_im, h0, *, nb):
    b, n, g, w = x4.shape
    nb = min(nb, n)
    steps = n // nb
    blk = lambda rev: pl.BlockSpec((b, nb, g, w), (lambda i: (0, steps - 1 - i, 0, 0)) if rev
                                   else (lambda i: (0, i, 0, 0)))
    st = pl.BlockSpec((b, 2, g, LANES), lambda i: (0, 0, 0, 0))
    return pl.pallas_call(
        functools.partial(_s5_scan_kernel, nb=nb),
        grid=(steps,),
        in_specs=[blk(False), blk(True), _const_spec((g, LANES)), _const_spec((g, LANES)), st],
        out_specs=[blk(False), blk(True), st],
        out_shape=[jax.ShapeDtypeStruct(x4.shape, F32), jax.ShapeDtypeStruct(x4.shape, F32),
                   jax.ShapeDtypeStruct((b, 2, g, LANES), F32)],
        scratch_shapes=[pltpu.VMEM((b, g, LANES), F32), pltpu.VMEM((b, g, LANES), F32)],
        compiler_params=_params("arbitrary"),
        name="s5_scan",
    )(x4, x4, at_re, at_im, h0)


def _s5_y_kernel(u_ref, hf_ref, hb_ref, m_ref, qf_ref, qb_ref, y_ref):
    y = (_dot(u_ref[0], m_ref[0]) + _dot(hf_ref[...].astype(BF16), qf_ref[0])
         + _dot(hb_ref[...].astype(BF16), qb_ref[0]))
    rt = y.shape[0]
    for t in range(S5_T):
        y_ref[pl.ds(t, rt, stride=S5_T), :] = y[:, t * LANES:(t + 1) * LANES]


def _s5_y_call(uf, hf2, hb2, m_blk, qf_blk, qb_blk, *, rt):
    sg, r, uw = uf.shape
    rt = min(rt, r)
    wspec = pl.BlockSpec((1, uw, uw), lambda si, i: (si, 0, 0), pipeline_mode=pl.Buffered(1))
    hspec = pl.BlockSpec((rt, uw), lambda si, i: (i, si))
    return pl.pallas_call(
        _s5_y_kernel,
        grid=(sg, r // rt),
        in_specs=[pl.BlockSpec((1, rt, uw), lambda si, i: (si, i, 0)), hspec, hspec, wspec, wspec, wspec],
        out_specs=pl.BlockSpec((rt * S5_T, LANES), lambda si, i: (i, si)),
        out_shape=jax.ShapeDtypeStruct((r * S5_T, sg * LANES), F32),
        compiler_params=_params("parallel", "parallel"),
        name="s5_y",
    )(uf, hf2, hb2, m_blk, qf_blk, qb_blk)


def _s5_mix(uf, ops, h0, bsz, *, nb, rt):
    p_blk, m_blk, qf_blk, qb_blk, at_re, at_im = ops
    r = uf.shape[1]
    n = r // bsz
    x = _s5_x_call(uf, p_blk, rt=rt)
    hf, hb, hfin = _s5_scan_call(x.reshape(bsz, n, S5_GROUPS, S5_CW), at_re, at_im, h0, nb=nb)
    y = _s5_y_call(uf, hf.reshape(r, S5_GROUPS * S5_CW), hb.reshape(r, S5_GROUPS * S5_CW),
                   m_blk, qf_blk, qb_blk, rt=rt)
    return y.reshape(bsz, n * S5_T, S5_WIDTH), hfin


def _attn_kernel(q_ref, k_ref, vt_ref, o_ref, m_scr, acc_scr, s_scr, mx_scr, *, tk, n_kv, n_str):
    grp, tq = q_ref.shape[2], q_ref.shape[3] // n_str
    cols = grp * tq
    q = [q_ref[0, 0, :, st * tq:(st + 1) * tq, :].reshape(cols, LANES) for st in range(n_str)]
    m_scr[...] = jnp.full(m_scr.shape, -jnp.inf, F32)
    acc_scr[...] = jnp.zeros(acc_scr.shape, F32)

    def scores(j, st, slot):
        off = pl.multiple_of(j * tk, tk)
        s = _dotg(k_ref[0, pl.ds(off, tk), :], q[st], (1, 1))
        s_scr[st, slot] = s
        mx_scr[st, slot] = jnp.broadcast_to(jnp.max(s, axis=0, keepdims=True), (8, cols))

    def softmax_pv(j, st, slot):
        off = pl.multiple_of(j * tk, tk)
        m_prev = m_scr[st]
        m_new = jnp.maximum(m_prev, mx_scr[st, slot])
        alpha = jnp.exp2(m_prev - m_new)
        p = jnp.exp2(s_scr[st, slot] - m_new[0:1]).astype(BF16)
        acc_scr[st] = acc_scr[st] * alpha[0:1] + _dot(vt_ref[0, 0, :, pl.ds(off, tk)], p)
        m_scr[st] = m_new

    def step(j_next, j_cur, slot_next, slot_cur):
        for st in range(n_str):
            if j_next is not None:
                scores(j_next, st, slot_next)
            if j_cur is not None:
                softmax_pv(j_cur, st, slot_cur)

    step(0, None, 0, None)

    def pair(t, carry):
        j = 2 * t
        step(j + 1, j, 1, 0)
        step(j + 2, j + 1, 0, 1)
        return carry

    lax.fori_loop(0, (n_kv - 1) // 2, pair, 0)
    if n_kv % 2 == 0:
        step(n_kv - 1, n_kv - 2, 1, 0)
        step(None, n_kv - 1, None, 1)
    else:
        step(None, n_kv - 1, None, 0)
    for st in range(n_str):
        acc = acc_scr[st]
        out_t = acc / acc[ATT_HEAD_DIM:ATT_HEAD_DIM + 1]
        for j in range(grp):
            o_ref[0, 0, j, st * tq:(st + 1) * tq, :] = out_t[:, j * tq:(j + 1) * tq].T.astype(o_ref.dtype)


def _attn_call(qs, k, vt_ext, *, tq, tk, n_str):
    b, kvh, grp, l, _ = qs.shape
    lk = k.shape[1]
    tq, tk = min(tq, l // n_str), min(tk, lk)
    cols = grp * tq
    blk = pl.BlockSpec((1, 1, grp, n_str * tq, LANES), lambda bi, ki, i: (bi, ki, 0, i, 0))
    return pl.pallas_call(
        functools.partial(_attn_kernel, tk=tk, n_kv=lk // tk, n_str=n_str),
        grid=(b, kvh, l // (n_str * tq)),
        in_specs=[blk,
                  pl.BlockSpec((1, lk, LANES), lambda bi, ki, i: (bi, 0, 0), pipeline_mode=pl.Buffered(1)),
                  pl.BlockSpec((1, 1, LANES, lk), lambda bi, ki, i: (bi, ki, 0, 0), pipeline_mode=pl.Buffered(1))],
        out_specs=blk,
        out_shape=jax.ShapeDtypeStruct(qs.shape, BF16),
        scratch_shapes=[pltpu.VMEM((n_str, 8, cols), F32), pltpu.VMEM((n_str, LANES, cols), F32),
                        pltpu.VMEM((n_str, 2, tk, cols), F32), pltpu.VMEM((n_str, 2, 8, cols), F32)],
        compiler_params=_params("parallel", "parallel", "arbitrary"),
        name="attention",
    )(qs, k, vt_ext)


def _kv_layout(ak, av):
    b, lk, _ = av.shape
    vt = jnp.transpose(av.reshape(b, lk, ATT_KV_HEADS, ATT_HEAD_DIM), (0, 2, 3, 1))
    return ak, jnp.concatenate([vt, jnp.ones_like(vt)], axis=2)


def _merge_kernel(h_ref, mod_ref, gn_ref, yg_ref, ys_ref, u_ref, ya_ref, wbg_ref, bbg_ref, wp_ref, wpa_ref,
                  wo_ref, d_ref, gw_ref, gb_ref, o_ref):
    h = h_ref[0]
    d = h.shape[-1]
    n = _rms_mod(h, gn_ref[...], mod_ref[0, 0:1], mod_ref[0, 1:2]).astype(BF16)
    g = jax.nn.sigmoid(_dot(n, wbg_ref[...]) + bbg_ref[...])
    y = jax.nn.gelu(ys_ref[0] + d_ref[...] * u_ref[0])
    y_s5 = y * jax.nn.sigmoid(_dot(y.astype(BF16), gw_ref[...]) + gb_ref[...])
    ya = jnp.concatenate([ya_ref[0, kv, j] for kv in range(ATT_KV_HEADS) for j in range(ATT_GROUP)], axis=-1)
    m = (g[:, 0:d] * _dot(yg_ref[0].astype(BF16), wp_ref[0])
         + g[:, d:2 * d] * _dot(y_s5.astype(BF16), wp_ref[1])
         + g[:, 2 * d:3 * d] * _dot(ya, wpa_ref[...]))
    o_ref[0] = h + mod_ref[0, 2:3] * _dot(m.astype(BF16), wo_ref[...])


def _merge_call(h, mod3, g_norm, y_gla, y_s5raw, u, y_att, lw, *, tm):
    b, l, d = h.shape
    tm = min(tm, l)
    tok = lambda w: pl.BlockSpec((1, tm, w), lambda bi, i: (bi, i, 0))
    bw = y_gla.shape[-1]
    att = pl.BlockSpec((1, ATT_KV_HEADS, ATT_GROUP, tm, LANES), lambda bi, i: (bi, 0, 0, i, 0))
    return pl.pallas_call(
        _merge_kernel,
        grid=(b, l // tm),
        in_specs=[tok(d), pl.BlockSpec((1, 3, d), lambda bi, i: (bi, 0, 0)), _const_spec((1, d)),
                  tok(bw), tok(bw), tok(bw), att,
                  _const_spec((d, 3 * d)), _const_spec((1, 3 * d)), _const_spec((2, bw, d)),
                  _const_spec((ATT_HEADS * LANES, d)), _const_spec((d, d)),
                  _const_spec((1, bw)), _const_spec((bw, bw)), _const_spec((1, bw))],
        out_specs=tok(d),
        out_shape=jax.ShapeDtypeStruct((b, l, d), F32),
        compiler_params=_params("parallel", "parallel"),
        name="merge",
    )(h, mod3, g_norm.reshape(1, d), y_gla, y_s5raw, u, y_att, lw["w_bgate"], lw["b_bgate"], lw["w_bproj"],
      lw["w_aproj"], lw["w_out"], lw["s5_d"], lw["glu_w"], lw["glu_b"])


def _rope_tables(n_tokens):
    rows = n_tokens // GRID_W
    row = jnp.repeat(jnp.arange(rows, dtype=F32), GRID_W)
    col = jnp.tile(jnp.arange(GRID_W, dtype=F32), rows)
    inv = ROPE_THETA ** (-jnp.arange(ROPE_PAIRS, dtype=F32) / ROPE_PAIRS)
    ang_r, ang_c = row[:, None] * inv, col[:, None] * inv
    zero = jnp.zeros_like(ang_r)
    cos = jnp.concatenate([jnp.cos(ang_r), jnp.cos(ang_r), jnp.cos(ang_c), jnp.cos(ang_c)], axis=-1)
    sin_a = jnp.concatenate([-jnp.sin(ang_r), zero, -jnp.sin(ang_c), zero], axis=-1)
    sin_b = jnp.concatenate([zero, jnp.sin(ang_r), zero, jnp.sin(ang_c)], axis=-1)
    tile = lambda t: jnp.tile(t, (1, ATT_HEADS))
    return tile(cos), tile(sin_a), tile(sin_b)


def _block_diag_ones(width, seg):
    idx = np.arange(width) // seg
    return jnp.asarray((idx[:, None] == idx[None, :]).astype(np.float32), BF16)


def _layer_weights(i, w_in, gla_gate_w, gla_gate_b, attn_q_norm_w, attn_k_norm_w, w_branch_gate,
                   b_branch_gate, w_branch_proj, w_out, s5_d, s5_glu_w, s5_glu_b):
    offs = np.concatenate([[0], np.cumsum(IN_WIDTHS)])
    col = lambda k: w_in[i][:, offs[k]:offs[k + 1]]
    w_main = jnp.concatenate([col(0), col(1), col(2), col(5), col(6), col(7), col(8), col(9)], axis=1)
    d = w_in.shape[1]
    rk = GLA_GATE_RANK
    w_gz = jnp.zeros((d, LANES), F32).at[:, 0:rk].set(col(3)).at[:, rk:2 * rk].set(col(4))
    gw = jnp.zeros((LANES, 2 * GLA_QK), F32)
    gw = gw.at[0:rk, 0:GLA_QK].set(gla_gate_w[i, 0]).at[rk:2 * rk, GLA_QK:].set(gla_gate_w[i, 1])
    wa = w_branch_proj[i, 2].reshape(ATT_HEADS, ATT_HEAD_DIM, d)
    w_aproj = jnp.concatenate([wa, jnp.zeros_like(wa)], axis=1).reshape(ATT_HEADS * LANES, d)
    return dict(
        w_aproj=w_aproj.astype(BF16),
        w_main=w_main.astype(BF16), w_gz=w_gz.astype(BF16), gate_w_bd=gw.astype(BF16),
        gate_b=gla_gate_b[i].reshape(1, 2 * GLA_QK),
        q_norm_w=(jnp.tile(attn_q_norm_w[i], ATT_HEADS) * (ATT_HEAD_DIM ** -0.5 * LOG2_E)).reshape(1, ATT_Q),
        k_norm_w=jnp.tile(attn_k_norm_w[i], ATT_KV_HEADS).reshape(1, ATT_KV),
        bd_q=_block_diag_ones(ATT_Q, ATT_HEAD_DIM), bd_k=_block_diag_ones(ATT_KV, ATT_HEAD_DIM),
        w_bgate=w_branch_gate[i].astype(BF16), b_bgate=b_branch_gate[i].reshape(1, -1),
        w_bproj=w_branch_proj[i, 0:2].astype(BF16), w_out=w_out[i].astype(BF16),
        s5_d=s5_d[i].reshape(1, -1), glu_w=s5_glu_w[i].astype(BF16), glu_b=s5_glu_b[i].reshape(1, -1),
    )


def _mixer_inputs(h, mod, g_norm, lw, tables, *, rope, tm):
    mod2 = mod[:, 3:5]
    return _inproj_call(h, mod2, g_norm, lw, tables, rope=rope, tm=tm)


def kernel(x, c, ctx, c_ctx, w_ada, b_ada, norm_w, w_ffn_gate, w_ffn_up, w_ffn_down, w_in, gla_gate_w,
           gla_gate_b, gla_norm_w, s5_a_re, s5_a_im, s5_log_dt, s5_b_re, s5_b_im, s5_c_re, s5_c_im, s5_d,
           s5_glu_w, s5_glu_b, attn_q_norm_w, attn_k_norm_w, w_branch_gate, b_branch_gate, w_branch_proj,
           w_out, final_norm_w):
    bsz, seq, d = x.shape
    depth = w_ada.shape[0]
    tables = _rope_tables(seq)
    ctx_tables = tuple(t[0:ctx.shape[1]] for t in tables)
    cond = jnp.zeros((8, d), F32).at[0:bsz].set(c).at[bsz].set(c_ctx)
    tm_ffn, tm_in, tm_merge, tb_gla, nb_s5, rt_s5, tq, tk = (
        TILES[k] for k in ("ffn", "inproj", "merge", "gla", "s5_scan", "s5_rows", "att_q", "att_k"))

    ffn_w = (w_ffn_gate.astype(BF16), w_ffn_up.astype(BF16), w_ffn_down.astype(BF16))
    h_lat, h_ctx = x, ctx
    for i in range(depth):
        last = i == depth - 1
        mods = _ada_call(cond, w_ada, b_ada, i)
        mod_l = mods[0:bsz].reshape(bsz, N_MOD, d)
        mod_c = jnp.broadcast_to(mods[bsz].reshape(1, N_MOD, d), (bsz, N_MOD, d))
        lw = _layer_weights(i, w_in, gla_gate_w, gla_gate_b, attn_q_norm_w, attn_k_norm_w, w_branch_gate,
                            b_branch_gate, w_branch_proj, w_out, s5_d, s5_glu_w, s5_glu_b)

        def ffn(h, mod, j, fin=False):
            return _ffn_call(h, mod[:, 6 * j:6 * j + 3], norm_w[i, 2 * j], *ffn_w, final_norm_w,
                             layer=i, sub=j, final_norm=fin, tm=tm_ffn)

        h_lat = ffn(h_lat, mod_l, 0)
        h_ctx = ffn(h_ctx, mod_c, 0)

        qk_c, v_c, r_c, lg_c, u_c, uf_c, aq_c, ak_c, av_c = _mixer_inputs(h_ctx, mod_c, norm_w[i, 1], lw,
                                                                          ctx_tables, rope=False, tm=tm_in)
        qk_l, v_l, r_l, lg_l, u_l, uf_l, aq_l, ak_l, av_l = _mixer_inputs(h_lat, mod_l, norm_w[i, 1], lw,
                                                                          tables, rope=True, tm=tm_in)
        s0 = jnp.zeros((bsz, GLA_QK, GLA_DV), F32)
        yg_c, sf_c, sb_c = _gla_bidir(qk_c, v_c, lg_c, r_c, gla_norm_w[i], s0, s0, tb=tb_gla)
        yg_l, _, _ = _gla_bidir(qk_l, v_l, lg_l, r_l, gla_norm_w[i], sf_c, sb_c, tb=tb_gla)
        ops = _s5_operators(s5_a_re[i], s5_a_im[i], s5_log_dt[i], s5_b_re[i], s5_b_im[i], s5_c_re[i], s5_c_im[i])
        h0 = jnp.zeros((bsz, 2, S5_GROUPS, LANES), F32)
        ys_c, hfin_c = _s5_mix(uf_c, ops, h0, bsz, nb=nb_s5, rt=rt_s5)
        ys_l, _ = _s5_mix(uf_l, ops, hfin_c, bsz, nb=nb_s5, rt=rt_s5)
        kt, v_ext = _kv_layout(jnp.concatenate([ak_c, ak_l], axis=1), jnp.concatenate([av_c, av_l], axis=1))
        ya_l = _attn_call(aq_l, kt, v_ext, tq=tq, tk=tk, n_str=TILES["att_streams"])
        h_lat = _merge_call(h_lat, mod_l[:, 3:6], norm_w[i, 1], yg_l, ys_l, u_l, ya_l, lw, tm=tm_merge)
        h_lat = ffn(h_lat, mod_l, 1, last)
        if not last:
            kt_c, v_ext_c = _kv_layout(ak_c, av_c)
            ya_c = _attn_call(aq_c, kt_c, v_ext_c, tq=tq, tk=tk, n_str=TILES["att_streams"])
            h_ctx = _merge_call(h_ctx, mod_c[:, 3:6], norm_w[i, 1], yg_c, ys_c, u_c, ya_c, lw, tm=tm_merge)
            h_ctx = ffn(h_ctx, mod_c, 1)
    return h_lat
```

```python
import functools

import jax
import jax.numpy as jnp
import numpy as np
from jax import lax
from jax.experimental import pallas as pl
from jax.experimental.pallas import tpu as pltpu

F32 = jnp.float32
BF16 = jnp.bfloat16

N_MOD = 9
MACARON_WEIGHT = 0.5
NORM_EPS = 1e-6
GRID_W = 64
GLA_HEADS = 4
GLA_DK = 64
GLA_DV = 128
GLA_QK = GLA_HEADS * GLA_DK
GLA_V = GLA_HEADS * GLA_DV
GLA_GATE_RANK = 16
GLA_TAU = 16.0
GLA_CHUNK = 64
S5_WIDTH = 512
S5_GROUP = 16
S5_GROUPS = S5_WIDTH // S5_GROUP
S5_STATE = 64
ATT_HEADS = 8
ATT_KV_HEADS = 2
ATT_GROUP = ATT_HEADS // ATT_KV_HEADS
ATT_HEAD_DIM = 64
ATT_Q = ATT_HEADS * ATT_HEAD_DIM
ATT_KV = ATT_KV_HEADS * ATT_HEAD_DIM
ROPE_PAIRS = ATT_HEAD_DIM // 4
ROPE_THETA = 10000.0
IN_WIDTHS = (GLA_QK, GLA_QK, GLA_V, GLA_GATE_RANK, GLA_GATE_RANK, GLA_V, S5_WIDTH, ATT_Q, ATT_KV, ATT_KV)

LANES = 128
MXU_DIM = 256
VMEM_LIMIT = 56 * 1024 * 1024

LOG2_E = 1.4426950408889634

S5_T = MXU_DIM // S5_GROUP
S5_CW = S5_T * S5_GROUP
S5_GPB = LANES // S5_GROUP
S5_SG = S5_GROUPS // S5_GPB
S5_UW = S5_T * LANES

TILES = dict(ffn=512, inproj=512, merge=512, gla=512, s5_scan=32, s5_rows=256, att_q=128, att_k=1280,
             att_streams=4)


def _params(*sem):
    return pltpu.CompilerParams(dimension_semantics=sem, vmem_limit_bytes=VMEM_LIMIT)


def _const_spec(shape):
    nd = len(shape)
    return pl.BlockSpec(shape, lambda *_: (0,) * nd, pipeline_mode=pl.Buffered(1))


def _dotg(a, b, contract):
    return lax.dot_general(a, b, (((contract[0],), (contract[1],)), ((), ())),
                           precision=lax.Precision.DEFAULT, preferred_element_type=F32)


def _dot(a, b):
    return _dotg(a, b, (1, 0))


def _dot_hi_lo(x, w):
    hi = x.astype(BF16)
    lo = (x - hi.astype(F32)).astype(BF16)
    return _dot(hi, w) + _dot(lo, w)


def _rms_mod(h, g_norm, shift, scale):
    ms = jnp.mean(h * h, axis=-1, keepdims=True)
    y = h * lax.rsqrt(ms + NORM_EPS) * g_norm
    return y * (1.0 + scale) + shift


def _log_sigmoid(x):
    return jnp.minimum(x, 0.0) - jnp.log(1.0 + jnp.exp(-jnp.abs(x)))


def _ada_kernel(c_ref, w_ref, b_ref, o_ref):
    c = c_ref[...]
    sc = (c * jax.nn.sigmoid(c)).astype(BF16)
    o_ref[...] = _dot(sc, w_ref[...].astype(BF16)) + b_ref[...]


def _ada_call(cond, w_all, b_all, layer):
    rows, d = cond.shape
    depth, _, n = w_all.shape
    tn = 9 * LANES
    return pl.pallas_call(
        _ada_kernel,
        grid=(n // tn,),
        in_specs=[pl.BlockSpec((rows, d), lambda j: (0, 0)),
                  pl.BlockSpec((None, d, tn), lambda j: (layer, 0, j)),
                  pl.BlockSpec((None, 1, tn), lambda j: (layer, 0, j))],
        out_specs=pl.BlockSpec((rows, tn), lambda j: (0, j)),
        out_shape=jax.ShapeDtypeStruct((rows, n), F32),
        compiler_params=_params("arbitrary"),
        name="adaln",
    )(cond, w_all, b_all.reshape(depth, 1, n))


def _ffn_kernel(h_ref, mod_ref, gn_ref, wg_ref, wu_ref, wd_ref, fn_ref, o_ref, *, final_norm):
    h = h_ref[0]
    shift, scale, gate = mod_ref[0, 0:1], mod_ref[0, 1:2], mod_ref[0, 2:3]
    n = _rms_mod(h, gn_ref[...], shift, scale).astype(BF16)
    g = _dot(n, wg_ref[...])
    u = _dot(n, wu_ref[...])
    a = (g * jax.nn.sigmoid(g) * u).astype(BF16)
    y = h + (MACARON_WEIGHT * gate) * _dot(a, wd_ref[...])
    if final_norm:
        ms = jnp.mean(y * y, axis=-1, keepdims=True)
        y = y * lax.rsqrt(ms + NORM_EPS) * fn_ref[...]
    o_ref[0] = y


def _ffn_call(h, mod3, g_norm, wg, wu, wd, final_w, *, layer, sub, final_norm, tm):
    b, l, d = h.shape
    f = wg.shape[-1]
    tm = min(tm, l)
    wspec = lambda r, c: pl.BlockSpec((None, None, r, c), lambda bi, i: (layer, sub, 0, 0),
                                      pipeline_mode=pl.Buffered(1))
    return pl.pallas_call(
        functools.partial(_ffn_kernel, final_norm=final_norm),
        grid=(b, l // tm),
        in_specs=[pl.BlockSpec((1, tm, d), lambda bi, i: (bi, i, 0)),
                  pl.BlockSpec((1, 3, d), lambda bi, i: (bi, 0, 0)),
                  _const_spec((1, d)), wspec(d, f), wspec(d, f), wspec(f, d),
                  _const_spec((1, d))],
        out_specs=pl.BlockSpec((1, tm, d), lambda bi, i: (bi, i, 0)),
        out_shape=jax.ShapeDtypeStruct((b, l, d), F32),
        compiler_params=_params("parallel", "parallel"),
        name="ffn",
    )(h, mod3, g_norm.reshape(1, d), wg, wu, wd, final_w.reshape(1, d))


def _inproj_kernel(h_ref, mod_ref, gn_ref, wm_ref, wgz_ref, gw_ref, gb_ref, qw_ref, kw_ref,
                   bdq_ref, bdk_ref, cos_ref, sa_ref, sb_ref,
                   qk_ref, v_ref, r_ref, lg_ref, u_ref, uf_ref, aq_ref, ak_ref, av_ref, us_scr, *, rope):
    h = h_ref[0]
    tm = h.shape[0]
    n = _rms_mod(h, gn_ref[...], mod_ref[0, 0:1], mod_ref[0, 1:2]).astype(BF16)
    z = _dot(n, wm_ref[...])
    o = 0
    q = z[:, o:o + GLA_QK] * (GLA_DK ** -0.5); o += GLA_QK
    k = z[:, o:o + GLA_QK]; o += GLA_QK
    qk_ref[0, :, 0:GLA_QK] = q
    qk_ref[0, :, GLA_QK:2 * GLA_QK] = k
    v_ref[0] = z[:, o:o + GLA_V]; o += GLA_V
    r_ref[0] = z[:, o:o + GLA_V]; o += GLA_V
    u_ref[0] = z[:, o:o + S5_WIDTH]; o += S5_WIDTH
    for sg in range(S5_SG):
        us_scr[sg] = z[:, o - S5_WIDTH + sg * LANES:o - S5_WIDTH + (sg + 1) * LANES]
        for t in range(S5_T):
            piece = us_scr[sg, pl.ds(t, tm // S5_T, stride=S5_T), :]
            uf_ref[sg, :, t * LANES:(t + 1) * LANES] = piece.astype(BF16)
    xq = z[:, o:o + ATT_Q]; o += ATT_Q
    xk = z[:, o:o + ATT_KV]; o += ATT_KV
    av_ref[0] = z[:, o:o + ATT_KV].astype(BF16)

    zg = _dot(n, wgz_ref[...]).astype(BF16)
    lg_ref[0] = _log_sigmoid(_dot(zg, gw_ref[...]) + gb_ref[...]) * (1.0 / GLA_TAU)

    def head_norm(x, bd_ref, w_ref):
        ss = _dot_hi_lo(x * x, bd_ref[...])
        return x * lax.rsqrt(ss * (1.0 / ATT_HEAD_DIM) + NORM_EPS) * w_ref[...]

    def rotary(x, width):
        if not rope:
            return x
        c, sa, sb = cos_ref[:, 0:width], sa_ref[:, 0:width], sb_ref[:, 0:width]
        return (x * c + pltpu.roll(x, width - ROPE_PAIRS, 1) * sa + pltpu.roll(x, ROPE_PAIRS, 1) * sb)

    ak_ref[0] = rotary(head_norm(xk, bdk_ref, kw_ref), ATT_KV).astype(BF16)
    aq = rotary(head_norm(xq, bdq_ref, qw_ref), ATT_Q)
    lane = lax.broadcasted_iota(jnp.int32, (tm, LANES), 1)
    for kv in range(ATT_KV_HEADS):
        keep = (lane >= kv * ATT_HEAD_DIM) & (lane < (kv + 1) * ATT_HEAD_DIM)
        for j in range(ATT_GROUP):
            hd = kv * ATT_GROUP + j
            blk = aq[:, (hd // 2) * LANES:(hd // 2 + 1) * LANES]
            if hd % 2 != kv:
                blk = pltpu.roll(blk, ATT_HEAD_DIM, 1)
            aq_ref[0, kv, j] = jnp.where(keep, blk, 0.0).astype(BF16)


def _inproj_call(h, mod2, g_norm, lw, tables, *, rope, tm):
    b, l, d = h.shape
    tm = min(tm, l)
    tok = lambda w: pl.BlockSpec((1, tm, w), lambda bi, i: (bi, i, 0))
    tab = pl.BlockSpec((tm, ATT_Q), lambda bi, i: (i, 0))
    wm = lw["w_main"]
    nt = l // tm
    tok_out = lambda w, dt: (tok(w), jax.ShapeDtypeStruct((b, l, w), dt))
    outs = [tok_out(2 * GLA_QK, F32), tok_out(GLA_V, F32), tok_out(GLA_V, F32), tok_out(2 * GLA_QK, F32),
            tok_out(S5_WIDTH, F32),
            (pl.BlockSpec((S5_SG, tm // S5_T, S5_UW), lambda bi, i: (0, bi * nt + i, 0)),
             jax.ShapeDtypeStruct((S5_SG, b * l // S5_T, S5_UW), BF16)),
            (pl.BlockSpec((1, ATT_KV_HEADS, ATT_GROUP, tm, LANES), lambda bi, i: (bi, 0, 0, i, 0)),
             jax.ShapeDtypeStruct((b, ATT_KV_HEADS, ATT_GROUP, l, LANES), BF16)),
            tok_out(ATT_KV, BF16), tok_out(ATT_KV, BF16)]
    return pl.pallas_call(
        functools.partial(_inproj_kernel, rope=rope),
        grid=(b, nt),
        in_specs=[tok(d), pl.BlockSpec((1, 2, d), lambda bi, i: (bi, 0, 0)), _const_spec((1, d)),
                  _const_spec(wm.shape), _const_spec((d, LANES)), _const_spec((LANES, 2 * GLA_QK)),
                  _const_spec((1, 2 * GLA_QK)), _const_spec((1, ATT_Q)), _const_spec((1, ATT_KV)),
                  _const_spec((ATT_Q, ATT_Q)), _const_spec((ATT_KV, ATT_KV)), tab, tab, tab],
        out_specs=[s for s, _ in outs],
        out_shape=[t for _, t in outs],
        scratch_shapes=[pltpu.VMEM((S5_SG, tm, LANES), F32)],
        compiler_params=_params("parallel", "parallel"),
        name="inproj",
    )(h, mod2, g_norm.reshape(1, d), wm, lw["w_gz"], lw["gate_w_bd"], lw["gate_b"], lw["q_norm_w"],
      lw["k_norm_w"], lw["bd_q"], lw["bd_k"], *tables)


def _gla_kernel(qk_ref, v_ref, lg_ref, s0_ref, tri_ref, of_ref, r_ref, nw_ref, o_ref, sfin_ref, s_scr,
                *, reverse, readout, n_chunks, n_batch):
    i = pl.program_id(0)

    @pl.when(i == 0)
    def _():
        s_scr[...] = s0_ref[...]

    tri = tri_ref[...]
    c, nh = GLA_CHUNK, GLA_HEADS
    row = lax.broadcasted_iota(jnp.int32, (nh * c, c), 0) % c
    col = lax.broadcasted_iota(jnp.int32, (nh * c, c), 1)
    keep = (col >= row) if reverse else (col <= row)
    head_of_lane = lax.broadcasted_iota(jnp.int32, (c, GLA_QK), 1) // GLA_DK

    def chunk(bi, ci):
        rows = slice(ci * c, (ci + 1) * c)
        g = lg_ref[bi, rows, :]
        g_hi = g.astype(BF16)
        b = _dot(tri, g_hi) + _dot(tri, (g - g_hi.astype(F32)).astype(BF16))
        b_last = b[0:1, :] if reverse else b[c - 1:c, :]
        q_in = qk_ref[bi, rows, 0:GLA_QK] * jnp.exp(b)
        k = qk_ref[bi, rows, GLA_QK:2 * GLA_QK]
        k_in = (k * jnp.exp(-b)).astype(BF16)
        k_out = (k * jnp.exp(b_last - b)).astype(BF16)
        dec = jnp.exp(jnp.broadcast_to(b_last, (LANES, GLA_QK)).T)
        v = v_ref[bi, rows, :].astype(BF16)
        s_old = s_scr[bi]
        q_st = jnp.concatenate([jnp.where(head_of_lane == hd, q_in, 0.0) for hd in range(nh)],
                               axis=0).astype(BF16)
        att = _dotg(q_st, k_in, (1, 1))
        att = jnp.where(keep, att, 0.0).astype(BF16)
        o_intra = _dot(att, v)
        o_inter = _dot(q_st, s_old.astype(BF16))
        ds = _dotg(k_out, v, (0, 0))
        for hd in range(nh):
            tok = slice(hd * c, (hd + 1) * c)
            vcols = slice(hd * GLA_DV, (hd + 1) * GLA_DV)
            o_h = o_intra[tok, vcols] + o_inter[tok]
            if readout:
                o_h = o_h + of_ref[bi, rows, vcols]
                ms = jnp.mean(o_h * o_h, axis=-1, keepdims=True)
                y = o_h * lax.rsqrt(ms + NORM_EPS) * nw_ref[...]
                rr = r_ref[bi, rows, vcols]
                o_h = y * (rr * jax.nn.sigmoid(rr))
            o_ref[bi, rows, vcols] = o_h
        d_state = jnp.concatenate([ds[hd * GLA_DK:(hd + 1) * GLA_DK, hd * GLA_DV:(hd + 1) * GLA_DV]
                                   for hd in range(nh)], axis=0)
        s_scr[bi] = dec * s_old + d_state

    for ci in (range(n_chunks - 1, -1, -1) if reverse else range(n_chunks)):
        for bi in range(n_batch):
            chunk(bi, ci)

    @pl.when(i == pl.num_programs(0) - 1)
    def _():
        sfin_ref[...] = s_scr[...]


def _gla_call(qk, v, lg, s0, o_f, r, norm_w, *, reverse, readout, tb):
    b, l, _ = v.shape
    tb = min(tb, l)
    nb = l // tb
    idx = (lambda i: (0, nb - 1 - i, 0)) if reverse else (lambda i: (0, i, 0))
    tok = lambda w: pl.BlockSpec((b, tb, w), idx)
    lg_off = 1 if reverse else 0
    lg_spec = pl.BlockSpec((b, tb, GLA_QK), (lambda i: (0, nb - 1 - i, lg_off)) if reverse
                           else (lambda i: (0, i, lg_off)))
    state = pl.BlockSpec((b, GLA_QK, GLA_DV), lambda i: (0, 0, 0))
    ones = np.triu(np.ones((GLA_CHUNK, GLA_CHUNK), np.float32)) if reverse else \
        np.tril(np.ones((GLA_CHUNK, GLA_CHUNK), np.float32))
    return pl.pallas_call(
        functools.partial(_gla_kernel, reverse=reverse, readout=readout, n_chunks=tb // GLA_CHUNK, n_batch=b),
        grid=(nb,),
        in_specs=[tok(2 * GLA_QK), tok(GLA_V), lg_spec, state, _const_spec((GLA_CHUNK, GLA_CHUNK)),
                  tok(GLA_V), tok(GLA_V), _const_spec((1, GLA_DV))],
        out_specs=[tok(GLA_V), state],
        out_shape=[jax.ShapeDtypeStruct((b, l, GLA_V), F32), jax.ShapeDtypeStruct((b, GLA_QK, GLA_DV), F32)],
        scratch_shapes=[pltpu.VMEM((b, GLA_QK, GLA_DV), F32)],
        compiler_params=_params("arbitrary"),
        name="gla_bwd" if reverse else "gla_fwd",
    )(qk, v, lg, s0, jnp.asarray(ones, BF16), o_f, r, norm_w.reshape(1, GLA_DV))


def _gla_bidir(qk, v, lg, r, norm_w, s0_f, s0_b, *, tb):
    o_f, s_f = _gla_call(qk, v, lg, s0_f, v, r, norm_w, reverse=False, readout=False, tb=tb)
    y, s_b = _gla_call(qk, v, lg, s0_b, o_f, r, norm_w, reverse=True, readout=True, tb=tb)
    return y, s_f, s_b


def _s5_operators(a_re, a_im, log_dt, b_re, b_im, c_re, c_im):
    hp = lax.Precision.HIGHEST
    t = S5_T
    cr, ci_ = c_re.astype(F32), c_im.astype(F32)
    ks = jnp.arange(t + 1, dtype=F32)[:, None, None]
    per_dir = []
    for d in range(2):
        dt = jnp.exp(log_dt[d].astype(F32))[:, None]
        ar, ai = a_re[d].astype(F32), a_im[d].astype(F32)
        mag = jnp.exp(ks * dt * ar)
        pw_re, pw_im = mag * jnp.cos(ks * dt * ai), mag * jnp.sin(ks * dt * ai)
        den = ar * ar + ai * ai
        xr, xi = pw_re[1] - 1.0, pw_im[1]
        coef_re = (xr * ar + xi * ai) / den
        coef_im = (xi * ar - xr * ai) / den
        br, bi = b_re.astype(F32), b_im.astype(F32)
        bb_re = coef_re[..., None] * br - coef_im[..., None] * bi
        bb_im = coef_re[..., None] * bi + coef_im[..., None] * br
        ca_re = cr[None] * pw_re[:, :, None, :] - ci_[None] * pw_im[:, :, None, :]
        ca_im = cr[None] * pw_im[:, :, None, :] + ci_[None] * pw_re[:, :, None, :]
        kk = (jnp.einsum("kgop,gpi->kgoi", ca_re, bb_re, precision=hp)
              - jnp.einsum("kgop,gpi->kgoi", ca_im, bb_im, precision=hp))
        s_idx = jnp.arange(t)[:, None]
        t_idx = jnp.arange(t)[None, :]
        lag = (t_idx - s_idx) if d == 0 else (s_idx - t_idx)
        toe = jnp.where((lag >= 0)[:, :, None, None, None], kk[jnp.clip(lag, 0, t)], 0.0)
        m = jnp.transpose(toe, (2, 0, 4, 1, 3)).reshape(S5_GROUPS, S5_CW, S5_CW)
        p_pow = (t - 1 - jnp.arange(t)) if d == 0 else jnp.arange(t)
        pr, pi = pw_re[p_pow], pw_im[p_pow]
        p_re = pr[:, :, :, None] * bb_re[None] - pi[:, :, :, None] * bb_im[None]
        p_im = pr[:, :, :, None] * bb_im[None] + pi[:, :, :, None] * bb_re[None]
        flat_p = lambda x: jnp.transpose(x, (1, 0, 3, 2)).reshape(S5_GROUPS, S5_CW, S5_STATE)
        q_pow = (jnp.arange(t) + 1) if d == 0 else (t - jnp.arange(t))
        q_re, q_im = ca_re[q_pow], -ca_im[q_pow]
        flat_q = lambda x: jnp.transpose(x, (1, 3, 0, 2)).reshape(S5_GROUPS, S5_STATE, S5_CW)
        per_dir.append(dict(m=m, p_re=flat_p(p_re), p_im=flat_p(p_im), q_re=flat_q(q_re), q_im=flat_q(q_im),
                            at_re=pw_re[t], at_im=pw_im[t]))
    f, bk = per_dir
    zq = jnp.zeros_like(f["q_re"])
    p_cat = jnp.concatenate([f["p_re"], bk["p_re"], f["p_im"], bk["p_im"]], axis=-1)
    q_f = jnp.concatenate([f["q_re"], zq, f["q_im"], zq], axis=1)
    q_b = jnp.concatenate([zq, bk["q_re"], zq, bk["q_im"]], axis=1)
    at_re = jnp.concatenate([f["at_re"], bk["at_re"]], axis=-1)
    at_im = jnp.concatenate([f["at_im"], bk["at_im"]], axis=-1)

    sg, gpb, t, gc = S5_SG, S5_GPB, S5_T, S5_GROUP
    idx = jnp.arange(S5_UW)
    gl_tgc = (idx // gc) % gpb
    gl_gk = idx // S5_CW
    src = np.arange(S5_UW)
    spread = np.zeros((S5_CW, S5_UW), np.float32)
    spread[(src // LANES) * gc + src % gc, src] = 1.0
    spread = jnp.asarray(spread, BF16)

    def rows_tgc(a):
        k = a.shape[-1]
        return jnp.transpose(a.astype(BF16).reshape(sg, gpb, t, gc, k), (0, 2, 1, 3, 4)).reshape(sg, S5_UW, k)

    def cols_tgc(a):
        return lax.dot_general(a, spread, (((2,), (0,)), ((), ())), preferred_element_type=BF16)

    def keep(rows_gl, cols_gl, a):
        return jnp.where((rows_gl[:, None] == cols_gl[None, :])[None], a, jnp.zeros_like(a))

    p_blk = keep(gl_tgc, gl_gk, jnp.tile(rows_tgc(p_cat), (1, 1, gpb)))
    m_blk = keep(gl_tgc, gl_tgc, cols_tgc(rows_tgc(f["m"] + bk["m"])))
    qf_blk = keep(gl_gk, gl_tgc, cols_tgc(q_f.astype(BF16).reshape(sg, S5_UW, S5_CW)))
    qb_blk = keep(gl_gk, gl_tgc, cols_tgc(q_b.astype(BF16).reshape(sg, S5_UW, S5_CW)))
    return p_blk, m_blk, qf_blk, qb_blk, at_re, at_im


def _s5_x_kernel(u_ref, p_ref, x_ref):
    x_ref[...] = _dot(u_ref[0], p_ref[0])


def _s5_x_call(uf, p_blk, layer, *, rt):
    sg, r, uw = uf.shape
    rt = min(rt, r)
    return pl.pallas_call(
        _s5_x_kernel,
        grid=(sg, r // rt),
        in_specs=[pl.BlockSpec((1, rt, uw), lambda si, i: (si, i, 0)),
                  pl.BlockSpec((None, 1, uw, uw), lambda si, i: (layer, si, 0, 0), pipeline_mode=pl.Buffered(1))],
        out_specs=pl.BlockSpec((rt, uw), lambda si, i: (i, si)),
        out_shape=jax.ShapeDtypeStruct((r, sg * uw), F32),
        compiler_params=_params("parallel", "parallel"),
        name="s5_x",
    )(uf, p_blk)


def _s5_scan_kernel(xf_ref, xb_ref, are_ref, aim_ref, h0_ref, hf_ref, hb_ref, hfin_ref, hr_scr, hi_scr, *, nb):
    i = pl.program_id(0)

    @pl.when(i == 0)
    def _():
        hr_scr[...] = h0_ref[:, 0]
        hi_scr[...] = h0_ref[:, 1]

    ar, ai = are_ref[...][None], aim_ref[...][None]
    fwd = lax.broadcasted_iota(jnp.int32, (1, S5_GROUPS, LANES), 2) < S5_STATE

    def step(j, carry):
        hr, hi = carry
        jb = nb - 1 - j
        xr = jnp.where(fwd, xf_ref[:, j, :, 0:LANES], xb_ref[:, jb, :, 0:LANES])
        xi = jnp.where(fwd, xf_ref[:, j, :, LANES:2 * LANES], xb_ref[:, jb, :, LANES:2 * LANES])
        hf_ref[:, j, :, 0:LANES] = hr
        hf_ref[:, j, :, LANES:2 * LANES] = hi
        hb_ref[:, jb, :, 0:LANES] = hr
        hb_ref[:, jb, :, LANES:2 * LANES] = hi
        return ar * hr - ai * hi + xr, ar * hi + ai * hr + xi

    hr, hi = lax.fori_loop(0, nb, step, (hr_scr[...], hi_scr[...]))
    hr_scr[...] = hr
    hi_scr[...] = hi

    @pl.when(i == pl.num_programs(0) - 1)
    def _():
        hfin_ref[:, 0] = hr
        hfin_ref[:, 1] = hi


def _s5_scan_call(x4, at_re, at_im, h0, *, nb):
    b, n, g, w = x4.shape
    nb = min(nb, n)
    steps = n // nb
    blk = lambda rev: pl.BlockSpec((b, nb, g, w), (lambda i: (0, steps - 1 - i, 0, 0)) if rev
                                   else (lambda i: (0, i, 0, 0)))
    st = pl.BlockSpec((b, 2, g, LANES), lambda i: (0, 0, 0, 0))
    return pl.pallas_call(
        functools.partial(_s5_scan_kernel, nb=nb),
        grid=(steps,),
        in_specs=[blk(False), blk(True), _const_spec((g, LANES)), _const_spec((g, LANES)), st],
        out_specs=[blk(False), blk(True), st],
        out_shape=[jax.ShapeDtypeStruct(x4.shape, F32), jax.ShapeDtypeStruct(x4.shape, F32),
                   jax.ShapeDtypeStruct((b, 2, g, LANES), F32)],
        scratch_shapes=[pltpu.VMEM((b, g, LANES), F32), pltpu.VMEM((b, g, LANES), F32)],
        compiler_params=_params("arbitrary"),
        name="s5_scan",
    )(x4, x4, at_re, at_im, h0)


def _s5_y_kernel(u_ref, hf_ref, hb_ref, m_ref, qf_ref, qb_ref, y_ref):
    y = (_dot(u_ref[0], m_ref[0]) + _dot(hf_ref[...].astype(BF16), qf_ref[0])
         + _dot(hb_ref[...].astype(BF16), qb_ref[0]))
    rt = y.shape[0]
    for t in range(S5_T):
        y_ref[pl.ds(t, rt, stride=S5_T), :] = y[:, t * LANES:(t + 1) * LANES]


def _s5_y_call(uf, hf2, hb2, m_blk, qf_blk, qb_blk, layer, *, rt):
    sg, r, uw = uf.shape
    rt = min(rt, r)
    wspec = pl.BlockSpec((None, 1, uw, uw), lambda si, i: (layer, si, 0, 0), pipeline_mode=pl.Buffered(1))
    hspec = pl.BlockSpec((rt, uw), lambda si, i: (i, si))
    return pl.pallas_call(
        _s5_y_kernel,
        grid=(sg, r // rt),
        in_specs=[pl.BlockSpec((1, rt, uw), lambda si, i: (si, i, 0)), hspec, hspec, wspec, wspec, wspec],
        out_specs=pl.BlockSpec((rt * S5_T, LANES), lambda si, i: (i, si)),
        out_shape=jax.ShapeDtypeStruct((r * S5_T, sg * LANES), F32),
        compiler_params=_params("parallel", "parallel"),
        name="s5_y",
    )(uf, hf2, hb2, m_blk, qf_blk, qb_blk)


def _s5_mix(uf, ops, layer, h0, bsz, *, nb, rt):
    p_blk, m_blk, qf_blk, qb_blk, at_re, at_im = ops
    r = uf.shape[1]
    n = r // bsz
    x = _s5_x_call(uf, p_blk, layer, rt=rt)
    hf, hb, hfin = _s5_scan_call(x.reshape(bsz, n, S5_GROUPS, S5_CW), at_re[layer], at_im[layer], h0, nb=nb)
    y = _s5_y_call(uf, hf.reshape(r, S5_GROUPS * S5_CW), hb.reshape(r, S5_GROUPS * S5_CW),
                   m_blk, qf_blk, qb_blk, layer, rt=rt)
    return y.reshape(bsz, n * S5_T, S5_WIDTH), hfin


def _attn_kernel(q_ref, k_ref, vt_ref, o_ref, m_scr, acc_scr, s_scr, mx_scr, *, tk, n_kv, n_str):
    grp, tq = q_ref.shape[2], q_ref.shape[3] // n_str
    cols = grp * tq
    q = [q_ref[0, 0, :, st * tq:(st + 1) * tq, :].reshape(cols, LANES) for st in range(n_str)]
    m_scr[...] = jnp.full(m_scr.shape, -jnp.inf, F32)
    acc_scr[...] = jnp.zeros(acc_scr.shape, F32)

    def scores(j, st, slot):
        off = pl.multiple_of(j * tk, tk)
        s = _dotg(k_ref[0, pl.ds(off, tk), :], q[st], (1, 1))
        s_scr[st, slot] = s
        mx_scr[st, slot] = jnp.broadcast_to(jnp.max(s, axis=0, keepdims=True), (8, cols))

    def softmax_pv(j, st, slot):
        off = pl.multiple_of(j * tk, tk)
        m_prev = m_scr[st]
        m_new = jnp.maximum(m_prev, mx_scr[st, slot])
        alpha = jnp.exp2(m_prev - m_new)
        p = jnp.exp2(s_scr[st, slot] - m_new[0:1]).astype(BF16)
        acc_scr[st] = acc_scr[st] * alpha[0:1] + _dot(vt_ref[0, 0, :, pl.ds(off, tk)], p)
        m_scr[st] = m_new

    def step(j_next, j_cur, slot_next, slot_cur):
        for st in range(n_str):
            if j_next is not None:
                scores(j_next, st, slot_next)
            if j_cur is not None:
                softmax_pv(j_cur, st, slot_cur)

    step(0, None, 0, None)

    def pair(t, carry):
        j = 2 * t
        step(j + 1, j, 1, 0)
        step(j + 2, j + 1, 0, 1)
        return carry

    lax.fori_loop(0, (n_kv - 1) // 2, pair, 0)
    if n_kv % 2 == 0:
        step(n_kv - 1, n_kv - 2, 1, 0)
        step(None, n_kv - 1, None, 1)
    else:
        step(None, n_kv - 1, None, 0)
    for st in range(n_str):
        acc = acc_scr[st]
        out_t = acc / acc[ATT_HEAD_DIM:ATT_HEAD_DIM + 1]
        for j in range(grp):
            o_ref[0, 0, j, st * tq:(st + 1) * tq, :] = out_t[:, j * tq:(j + 1) * tq].T.astype(o_ref.dtype)


def _attn_call(qs, k, vt_ext, *, tq, tk, n_str):
    b, kvh, grp, l, _ = qs.shape
    lk = k.shape[1]
    tq, tk = min(tq, l // n_str), min(tk, lk)
    cols = grp * tq
    blk = pl.BlockSpec((1, 1, grp, n_str * tq, LANES), lambda bi, ki, i: (bi, ki, 0, i, 0))
    return pl.pallas_call(
        functools.partial(_attn_kernel, tk=tk, n_kv=lk // tk, n_str=n_str),
        grid=(b, kvh, l // (n_str * tq)),
        in_specs=[blk,
                  pl.BlockSpec((1, lk, LANES), lambda bi, ki, i: (bi, 0, 0), pipeline_mode=pl.Buffered(1)),
                  pl.BlockSpec((1, 1, LANES, lk), lambda bi, ki, i: (bi, ki, 0, 0), pipeline_mode=pl.Buffered(1))],
        out_specs=blk,
        out_shape=jax.ShapeDtypeStruct(qs.shape, BF16),
        scratch_shapes=[pltpu.VMEM((n_str, 8, cols), F32), pltpu.VMEM((n_str, LANES, cols), F32),
                        pltpu.VMEM((n_str, 2, tk, cols), F32), pltpu.VMEM((n_str, 2, 8, cols), F32)],
        compiler_params=_params("parallel", "parallel", "arbitrary"),
        name="attention",
    )(qs, k, vt_ext)


def _kv_layout(ak, av):
    b, lk, _ = av.shape
    vt = jnp.transpose(av.reshape(b, lk, ATT_KV_HEADS, ATT_HEAD_DIM), (0, 2, 3, 1))
    return ak, jnp.concatenate([vt, jnp.ones_like(vt)], axis=2)


def _merge_kernel(h_ref, mod_ref, gn_ref, yg_ref, ys_ref, u_ref, ya_ref, wbg_ref, bbg_ref, wp_ref, wpa_ref,
                  wo_ref, d_ref, gw_ref, gb_ref, o_ref):
    h = h_ref[0]
    d = h.shape[-1]
    n = _rms_mod(h, gn_ref[...], mod_ref[0, 0:1], mod_ref[0, 1:2]).astype(BF16)
    g = jax.nn.sigmoid(_dot(n, wbg_ref[...]) + bbg_ref[...])
    y = jax.nn.gelu(ys_ref[0] + d_ref[...] * u_ref[0])
    y_s5 = y * jax.nn.sigmoid(_dot(y.astype(BF16), gw_ref[...]) + gb_ref[...])
    ya = jnp.concatenate([ya_ref[0, kv, j] for kv in range(ATT_KV_HEADS) for j in range(ATT_GROUP)], axis=-1)
    m = (g[:, 0:d] * _dot(yg_ref[0].astype(BF16), wp_ref[0])
         + g[:, d:2 * d] * _dot(y_s5.astype(BF16), wp_ref[1])
         + g[:, 2 * d:3 * d] * _dot(ya, wpa_ref[...]))
    o_ref[0] = h + mod_ref[0, 2:3] * _dot(m.astype(BF16), wo_ref[...])


def _merge_call(h, mod3, g_norm, y_gla, y_s5raw, u, y_att, lw, *, tm):
    b, l, d = h.shape
    tm = min(tm, l)
    tok = lambda w: pl.BlockSpec((1, tm, w), lambda bi, i: (bi, i, 0))
    bw = y_gla.shape[-1]
    att = pl.BlockSpec((1, ATT_KV_HEADS, ATT_GROUP, tm, LANES), lambda bi, i: (bi, 0, 0, i, 0))
    return pl.pallas_call(
        _merge_kernel,
        grid=(b, l // tm),
        in_specs=[tok(d), pl.BlockSpec((1, 3, d), lambda bi, i: (bi, 0, 0)), _const_spec((1, d)),
                  tok(bw), tok(bw), tok(bw), att,
                  _const_spec((d, 3 * d)), _const_spec((1, 3 * d)), _const_spec((2, bw, d)),
                  _const_spec((ATT_HEADS * LANES, d)), _const_spec((d, d)),
                  _const_spec((1, bw)), _const_spec((bw, bw)), _const_spec((1, bw))],
        out_specs=tok(d),
        out_shape=jax.ShapeDtypeStruct((b, l, d), F32),
        compiler_params=_params("parallel", "parallel"),
        name="merge",
    )(h, mod3, g_norm.reshape(1, d), y_gla, y_s5raw, u, y_att, lw["w_bgate"], lw["b_bgate"], lw["w_bproj"],
      lw["w_aproj"], lw["w_out"], lw["s5_d"], lw["glu_w"], lw["glu_b"])


def _rope_tables(n_tokens):
    rows = n_tokens // GRID_W
    row = jnp.repeat(jnp.arange(rows, dtype=F32), GRID_W)
    col = jnp.tile(jnp.arange(GRID_W, dtype=F32), rows)
    inv = ROPE_THETA ** (-jnp.arange(ROPE_PAIRS, dtype=F32) / ROPE_PAIRS)
    ang_r, ang_c = row[:, None] * inv, col[:, None] * inv
    zero = jnp.zeros_like(ang_r)
    cos = jnp.concatenate([jnp.cos(ang_r), jnp.cos(ang_r), jnp.cos(ang_c), jnp.cos(ang_c)], axis=-1)
    sin_a = jnp.concatenate([-jnp.sin(ang_r), zero, -jnp.sin(ang_c), zero], axis=-1)
    sin_b = jnp.concatenate([zero, jnp.sin(ang_r), zero, jnp.sin(ang_c)], axis=-1)
    tile = lambda t: jnp.tile(t, (1, ATT_HEADS))
    return tile(cos), tile(sin_a), tile(sin_b)


def _block_diag_ones(width, seg):
    idx = np.arange(width) // seg
    return jnp.asarray((idx[:, None] == idx[None, :]).astype(np.float32), BF16)


def _layer_weights(i, w_in, gla_gate_w, gla_gate_b, attn_q_norm_w, attn_k_norm_w, w_branch_gate,
                   b_branch_gate, w_branch_proj, w_out, s5_d, s5_glu_w, s5_glu_b):
    offs = np.concatenate([[0], np.cumsum(IN_WIDTHS)])
    col = lambda k: w_in[i][:, offs[k]:offs[k + 1]]
    w_main = jnp.concatenate([col(0), col(1), col(2), col(5), col(6), col(7), col(8), col(9)], axis=1)
    d = w_in.shape[1]
    rk = GLA_GATE_RANK
    w_gz = jnp.zeros((d, LANES), F32).at[:, 0:rk].set(col(3)).at[:, rk:2 * rk].set(col(4))
    gw = jnp.zeros((LANES, 2 * GLA_QK), F32)
    gw = gw.at[0:rk, 0:GLA_QK].set(gla_gate_w[i, 0]).at[rk:2 * rk, GLA_QK:].set(gla_gate_w[i, 1])
    wa = w_branch_proj[i, 2].reshape(ATT_HEADS, ATT_HEAD_DIM, d)
    w_aproj = jnp.concatenate([wa, jnp.zeros_like(wa)], axis=1).reshape(ATT_HEADS * LANES, d)
    return dict(
        w_aproj=w_aproj.astype(BF16),
        w_main=w_main.astype(BF16), w_gz=w_gz.astype(BF16), gate_w_bd=gw.astype(BF16),
        gate_b=gla_gate_b[i].reshape(1, 2 * GLA_QK),
        q_norm_w=(jnp.tile(attn_q_norm_w[i], ATT_HEADS) * (ATT_HEAD_DIM ** -0.5 * LOG2_E)).reshape(1, ATT_Q),
        k_norm_w=jnp.tile(attn_k_norm_w[i], ATT_KV_HEADS).reshape(1, ATT_KV),
        bd_q=_block_diag_ones(ATT_Q, ATT_HEAD_DIM), bd_k=_block_diag_ones(ATT_KV, ATT_HEAD_DIM),
        w_bgate=w_branch_gate[i].astype(BF16), b_bgate=b_branch_gate[i].reshape(1, -1),
        w_bproj=w_branch_proj[i, 0:2].astype(BF16), w_out=w_out[i].astype(BF16),
        s5_d=s5_d[i].reshape(1, -1), glu_w=s5_glu_w[i].astype(BF16), glu_b=s5_glu_b[i].reshape(1, -1),
    )


def _mixer_inputs(h, mod, g_norm, lw, tables, *, rope, tm):
    mod2 = mod[:, 3:5]
    return _inproj_call(h, mod2, g_norm, lw, tables, rope=rope, tm=tm)


def kernel(x, c, ctx, c_ctx, w_ada, b_ada, norm_w, w_ffn_gate, w_ffn_up, w_ffn_down, w_in, gla_gate_w,
           gla_gate_b, gla_norm_w, s5_a_re, s5_a_im, s5_log_dt, s5_b_re, s5_b_im, s5_c_re, s5_c_im, s5_d,
           s5_glu_w, s5_glu_b, attn_q_norm_w, attn_k_norm_w, w_branch_gate, b_branch_gate, w_branch_proj,
           w_out, final_norm_w):
    bsz, seq, d = x.shape
    depth = w_ada.shape[0]
    tables = _rope_tables(seq)
    ctx_tables = tuple(t[0:ctx.shape[1]] for t in tables)
    cond = jnp.zeros((8, d), F32).at[0:bsz].set(c).at[bsz].set(c_ctx)
    tm_ffn, tm_in, tm_merge, tb_gla, nb_s5, rt_s5, tq, tk = (
        TILES[k] for k in ("ffn", "inproj", "merge", "gla", "s5_scan", "s5_rows", "att_q", "att_k"))

    ffn_w = (w_ffn_gate.astype(BF16), w_ffn_up.astype(BF16), w_ffn_down.astype(BF16))
    s5_ops = jax.vmap(_s5_operators)(s5_a_re, s5_a_im, s5_log_dt, s5_b_re, s5_b_im, s5_c_re, s5_c_im)
    h_lat, h_ctx = x, ctx
    for i in range(depth):
        last = i == depth - 1
        mods = _ada_call(cond, w_ada, b_ada, i)
        mod_l = mods[0:bsz].reshape(bsz, N_MOD, d)
        mod_c = jnp.broadcast_to(mods[bsz].reshape(1, N_MOD, d), (bsz, N_MOD, d))
        lw = _layer_weights(i, w_in, gla_gate_w, gla_gate_b, attn_q_norm_w, attn_k_norm_w, w_branch_gate,
                            b_branch_gate, w_branch_proj, w_out, s5_d, s5_glu_w, s5_glu_b)

        def ffn(h, mod, j, fin=False):
            return _ffn_call(h, mod[:, 6 * j:6 * j + 3], norm_w[i, 2 * j], *ffn_w, final_norm_w,
                             layer=i, sub=j, final_norm=fin, tm=tm_ffn)

        h_lat = ffn(h_lat, mod_l, 0)
        h_ctx = ffn(h_ctx, mod_c, 0)

        qk_c, v_c, r_c, lg_c, u_c, uf_c, aq_c, ak_c, av_c = _mixer_inputs(h_ctx, mod_c, norm_w[i, 1], lw,
                                                                          ctx_tables, rope=False, tm=tm_in)
        qk_l, v_l, r_l, lg_l, u_l, uf_l, aq_l, ak_l, av_l = _mixer_inputs(h_lat, mod_l, norm_w[i, 1], lw,
                                                                          tables, rope=True, tm=tm_in)
        s0 = jnp.zeros((bsz, GLA_QK, GLA_DV), F32)
        yg_c, sf_c, sb_c = _gla_bidir(qk_c, v_c, lg_c, r_c, gla_norm_w[i], s0, s0, tb=tb_gla)
        yg_l, _, _ = _gla_bidir(qk_l, v_l, lg_l, r_l, gla_norm_w[i], sf_c, sb_c, tb=tb_gla)
        h0 = jnp.zeros((bsz, 2, S5_GROUPS, LANES), F32)
        ys_c, hfin_c = _s5_mix(uf_c, s5_ops, i, h0, bsz, nb=nb_s5, rt=rt_s5)
        ys_l, _ = _s5_mix(uf_l, s5_ops, i, hfin_c, bsz, nb=nb_s5, rt=rt_s5)
        kt, v_ext = _kv_layout(jnp.concatenate([ak_c, ak_l], axis=1), jnp.concatenate([av_c, av_l], axis=1))
        ya_l = _attn_call(aq_l, kt, v_ext, tq=tq, tk=tk, n_str=TILES["att_streams"])
        h_lat = _merge_call(h_lat, mod_l[:, 3:6], norm_w[i, 1], yg_l, ys_l, u_l, ya_l, lw, tm=tm_merge)
        h_lat = ffn(h_lat, mod_l, 1, last)
        if not last:
            kt_c, v_ext_c = _kv_layout(ak_c, av_c)
            ya_c = _attn_call(aq_c, kt_c, v_ext_c, tq=tq, tk=tk, n_str=TILES["att_streams"])
            h_ctx = _merge_call(h_ctx, mod_c[:, 3:6], norm_w[i, 1], yg_c, ys_c, u_c, ya_c, lw, tm=tm_merge)
            h_ctx = ffn(h_ctx, mod_c, 1)
    return h_lat
```

```python
import functools

import jax
import jax.numpy as jnp
import numpy as np
from jax import lax
from jax.experimental import pallas as pl
from jax.experimental.pallas import tpu as pltpu

F32 = jnp.float32
BF16 = jnp.bfloat16

N_MOD = 9
MACARON_WEIGHT = 0.5
NORM_EPS = 1e-6
GRID_W = 64
GLA_HEADS = 4
GLA_DK = 64
GLA_DV = 128
GLA_QK = GLA_HEADS * GLA_DK
GLA_V = GLA_HEADS * GLA_DV
GLA_GATE_RANK = 16
GLA_TAU = 16.0
GLA_CHUNK = 64
S5_WIDTH = 512
S5_GROUP = 16
S5_GROUPS = S5_WIDTH // S5_GROUP
S5_STATE = 64
ATT_HEADS = 8
ATT_KV_HEADS = 2
ATT_GROUP = ATT_HEADS // ATT_KV_HEADS
ATT_HEAD_DIM = 64
ATT_Q = ATT_HEADS * ATT_HEAD_DIM
ATT_KV = ATT_KV_HEADS * ATT_HEAD_DIM
ROPE_PAIRS = ATT_HEAD_DIM // 4
ROPE_THETA = 10000.0
IN_WIDTHS = (GLA_QK, GLA_QK, GLA_V, GLA_GATE_RANK, GLA_GATE_RANK, GLA_V, S5_WIDTH, ATT_Q, ATT_KV, ATT_KV)

LANES = 128
MXU_DIM = 256
VMEM_LIMIT = 56 * 1024 * 1024

LOG2_E = 1.4426950408889634

S5_T = MXU_DIM // S5_GROUP
S5_CW = S5_T * S5_GROUP
S5_GPB = LANES // S5_GROUP
S5_SG = S5_GROUPS // S5_GPB
S5_UW = S5_T * LANES

TILES = dict(ffn=512, inproj=512, merge=512, gla=512, s5_scan=32, s5_rows=256, att_q=128, att_k=1280,
             att_streams=4)


def _params(*sem):
    return pltpu.CompilerParams(dimension_semantics=sem, vmem_limit_bytes=VMEM_LIMIT)


def _const_spec(shape):
    nd = len(shape)
    return pl.BlockSpec(shape, lambda *_: (0,) * nd, pipeline_mode=pl.Buffered(1))


def _dotg(a, b, contract):
    return lax.dot_general(a, b, (((contract[0],), (contract[1],)), ((), ())),
                           precision=lax.Precision.DEFAULT, preferred_element_type=F32)


def _dot(a, b):
    return _dotg(a, b, (1, 0))


def _dot_hi_lo(x, w):
    hi = x.astype(BF16)
    lo = (x - hi.astype(F32)).astype(BF16)
    return _dot(hi, w) + _dot(lo, w)


def _rms_mod(h, g_norm, shift, scale):
    ms = jnp.mean(h * h, axis=-1, keepdims=True)
    y = h * lax.rsqrt(ms + NORM_EPS) * g_norm
    return y * (1.0 + scale) + shift


def _log_sigmoid(x):
    return jnp.minimum(x, 0.0) - jnp.log(1.0 + jnp.exp(-jnp.abs(x)))


def _ada_kernel(c_ref, w_ref, b_ref, o_ref):
    c = c_ref[...]
    sc = (c * jax.nn.sigmoid(c)).astype(BF16)
    o_ref[...] = _dot(sc, w_ref[...].astype(BF16)) + b_ref[...]


def _ada_call(cond, w_all, b_all, layer):
    rows, d = cond.shape
    depth, _, n = w_all.shape
    tn = 9 * LANES
    return pl.pallas_call(
        _ada_kernel,
        grid=(n // tn,),
        in_specs=[pl.BlockSpec((rows, d), lambda j: (0, 0)),
                  pl.BlockSpec((None, d, tn), lambda j: (layer, 0, j)),
                  pl.BlockSpec((None, 1, tn), lambda j: (layer, 0, j))],
        out_specs=pl.BlockSpec((rows, tn), lambda j: (0, j)),
        out_shape=jax.ShapeDtypeStruct((rows, n), F32),
        compiler_params=_params("arbitrary"),
        name="adaln",
    )(cond, w_all, b_all.reshape(depth, 1, n))


def _ffn_kernel(h_ref, mod_ref, gn_ref, wg_ref, wu_ref, wd_ref, fn_ref, o_ref, *, final_norm):
    h = h_ref[0]
    shift, scale, gate = mod_ref[0, 0:1], mod_ref[0, 1:2], mod_ref[0, 2:3]
    n = _rms_mod(h, gn_ref[...], shift, scale).astype(BF16)
    g = _dot(n, wg_ref[...])
    u = _dot(n, wu_ref[...])
    a = (g * jax.nn.sigmoid(g) * u).astype(BF16)
    y = h + (MACARON_WEIGHT * gate) * _dot(a, wd_ref[...])
    if final_norm:
        ms = jnp.mean(y * y, axis=-1, keepdims=True)
        y = y * lax.rsqrt(ms + NORM_EPS) * fn_ref[...]
    o_ref[0] = y


def _ffn_call(h, mod3, g_norm, wg, wu, wd, final_w, *, layer, sub, final_norm, tm):
    b, l, d = h.shape
    f = wg.shape[-1]
    tm = min(tm, l)
    wspec = lambda r, c: pl.BlockSpec((None, None, r, c), lambda bi, i: (layer, sub, 0, 0),
                                      pipeline_mode=pl.Buffered(1))
    return pl.pallas_call(
        functools.partial(_ffn_kernel, final_norm=final_norm),
        grid=(b, l // tm),
        in_specs=[pl.BlockSpec((1, tm, d), lambda bi, i: (bi, i, 0)),
                  pl.BlockSpec((1, 3, d), lambda bi, i: (bi, 0, 0)),
                  _const_spec((1, d)), wspec(d, f), wspec(d, f), wspec(f, d),
                  _const_spec((1, d))],
        out_specs=pl.BlockSpec((1, tm, d), lambda bi, i: (bi, i, 0)),
        out_shape=jax.ShapeDtypeStruct((b, l, d), F32),
        compiler_params=_params("parallel", "parallel"),
        name="ffn",
    )(h, mod3, g_norm.reshape(1, d), wg, wu, wd, final_w.reshape(1, d))


def _inproj_kernel(h_ref, mod_ref, gn_ref, wm_ref, wgz_ref, gw_ref, gb_ref, qw_ref, kw_ref,
                   bdq_ref, bdk_ref, cos_ref, sa_ref, sb_ref,
                   qk_ref, v_ref, r_ref, lg_ref, u_ref, uf_ref, aq_ref, ak_ref, av_ref, us_scr, *, rope):
    h = h_ref[0]
    tm = h.shape[0]
    n = _rms_mod(h, gn_ref[...], mod_ref[0, 0:1], mod_ref[0, 1:2]).astype(BF16)
    z = _dot(n, wm_ref[...])
    o = 0
    q = z[:, o:o + GLA_QK] * (GLA_DK ** -0.5); o += GLA_QK
    k = z[:, o:o + GLA_QK]; o += GLA_QK
    qk_ref[0, :, 0:GLA_QK] = q
    qk_ref[0, :, GLA_QK:2 * GLA_QK] = k
    v_ref[0] = z[:, o:o + GLA_V]; o += GLA_V
    r_ref[0] = z[:, o:o + GLA_V]; o += GLA_V
    u_ref[0] = z[:, o:o + S5_WIDTH]; o += S5_WIDTH
    for sg in range(S5_SG):
        us_scr[sg] = z[:, o - S5_WIDTH + sg * LANES:o - S5_WIDTH + (sg + 1) * LANES]
        for t in range(S5_T):
            piece = us_scr[sg, pl.ds(t, tm // S5_T, stride=S5_T), :]
            uf_ref[sg, :, t * LANES:(t + 1) * LANES] = piece.astype(BF16)
    xq = z[:, o:o + ATT_Q]; o += ATT_Q
    xk = z[:, o:o + ATT_KV]; o += ATT_KV
    av_ref[0] = z[:, o:o + ATT_KV].astype(BF16)

    zg = _dot(n, wgz_ref[...]).astype(BF16)
    lg_ref[0] = _log_sigmoid(_dot(zg, gw_ref[...]) + gb_ref[...]) * (1.0 / GLA_TAU)

    def head_norm(x, bd_ref, w_ref):
        ss = _dot_hi_lo(x * x, bd_ref[...])
        return x * lax.rsqrt(ss * (1.0 / ATT_HEAD_DIM) + NORM_EPS) * w_ref[...]

    def rotary(x, width):
        if not rope:
            return x
        c, sa, sb = cos_ref[:, 0:width], sa_ref[:, 0:width], sb_ref[:, 0:width]
        return (x * c + pltpu.roll(x, width - ROPE_PAIRS, 1) * sa + pltpu.roll(x, ROPE_PAIRS, 1) * sb)

    ak_ref[0] = rotary(head_norm(xk, bdk_ref, kw_ref), ATT_KV).astype(BF16)
    aq = rotary(head_norm(xq, bdq_ref, qw_ref), ATT_Q)
    lane = lax.broadcasted_iota(jnp.int32, (tm, LANES), 1)
    for kv in range(ATT_KV_HEADS):
        keep = (lane >= kv * ATT_HEAD_DIM) & (lane < (kv + 1) * ATT_HEAD_DIM)
        for j in range(ATT_GROUP):
            hd = kv * ATT_GROUP + j
            blk = aq[:, (hd // 2) * LANES:(hd // 2 + 1) * LANES]
            if hd % 2 != kv:
                blk = pltpu.roll(blk, ATT_HEAD_DIM, 1)
            aq_ref[0, kv, j] = jnp.where(keep, blk, 0.0).astype(BF16)


def _inproj_call(h, mod2, g_norm, lw, tables, *, rope, tm):
    b, l, d = h.shape
    tm = min(tm, l)
    tok = lambda w: pl.BlockSpec((1, tm, w), lambda bi, i: (bi, i, 0))
    tab = pl.BlockSpec((tm, ATT_Q), lambda bi, i: (i, 0))
    wm = lw["w_main"]
    nt = l // tm
    tok_out = lambda w, dt: (tok(w), jax.ShapeDtypeStruct((b, l, w), dt))
    outs = [tok_out(2 * GLA_QK, F32), tok_out(GLA_V, F32), tok_out(GLA_V, F32), tok_out(2 * GLA_QK, F32),
            tok_out(S5_WIDTH, F32),
            (pl.BlockSpec((S5_SG, tm // S5_T, S5_UW), lambda bi, i: (0, bi * nt + i, 0)),
             jax.ShapeDtypeStruct((S5_SG, b * l // S5_T, S5_UW), BF16)),
            (pl.BlockSpec((1, ATT_KV_HEADS, ATT_GROUP, tm, LANES), lambda bi, i: (bi, 0, 0, i, 0)),
             jax.ShapeDtypeStruct((b, ATT_KV_HEADS, ATT_GROUP, l, LANES), BF16)),
            tok_out(ATT_KV, BF16), tok_out(ATT_KV, BF16)]
    return pl.pallas_call(
        functools.partial(_inproj_kernel, rope=rope),
        grid=(b, nt),
        in_specs=[tok(d), pl.BlockSpec((1, 2, d), lambda bi, i: (bi, 0, 0)), _const_spec((1, d)),
                  _const_spec(wm.shape), _const_spec((d, LANES)), _const_spec((LANES, 2 * GLA_QK)),
                  _const_spec((1, 2 * GLA_QK)), _const_spec((1, ATT_Q)), _const_spec((1, ATT_KV)),
                  _const_spec((ATT_Q, ATT_Q)), _const_spec((ATT_KV, ATT_KV)), tab, tab, tab],
        out_specs=[s for s, _ in outs],
        out_shape=[t for _, t in outs],
        scratch_shapes=[pltpu.VMEM((S5_SG, tm, LANES), F32)],
        compiler_params=_params("parallel", "parallel"),
        name="inproj",
    )(h, mod2, g_norm.reshape(1, d), wm, lw["w_gz"], lw["gate_w_bd"], lw["gate_b"], lw["q_norm_w"],
      lw["k_norm_w"], lw["bd_q"], lw["bd_k"], *tables)


def _gla_kernel(qk_ref, v_ref, lg_ref, s0_ref, tri_ref, of_ref, r_ref, nw_ref, o_ref, sfin_ref, s_scr,
                *, reverse, readout, n_chunks, n_batch):
    i = pl.program_id(0)

    @pl.when(i == 0)
    def _():
        s_scr[...] = s0_ref[...]

    tri = tri_ref[...]
    c, nh = GLA_CHUNK, GLA_HEADS
    row = lax.broadcasted_iota(jnp.int32, (nh * c, c), 0) % c
    col = lax.broadcasted_iota(jnp.int32, (nh * c, c), 1)
    keep = (col >= row) if reverse else (col <= row)
    head_of_lane = lax.broadcasted_iota(jnp.int32, (c, GLA_QK), 1) // GLA_DK

    def chunk(bi, ci):
        rows = slice(ci * c, (ci + 1) * c)
        g = lg_ref[bi, rows, :]
        g_hi = g.astype(BF16)
        b = _dot(tri, g_hi) + _dot(tri, (g - g_hi.astype(F32)).astype(BF16))
        b_last = b[0:1, :] if reverse else b[c - 1:c, :]
        q_in = qk_ref[bi, rows, 0:GLA_QK] * jnp.exp(b)
        k = qk_ref[bi, rows, GLA_QK:2 * GLA_QK]
        k_in = (k * jnp.exp(-b)).astype(BF16)
        k_out = (k * jnp.exp(b_last - b)).astype(BF16)
        dec = jnp.exp(jnp.broadcast_to(b_last, (LANES, GLA_QK)).T)
        v = v_ref[bi, rows, :].astype(BF16)
        s_old = s_scr[bi]
        q_st = jnp.concatenate([jnp.where(head_of_lane == hd, q_in, 0.0) for hd in range(nh)],
                               axis=0).astype(BF16)
        att = _dotg(q_st, k_in, (1, 1))
        att = jnp.where(keep, att, 0.0).astype(BF16)
        o_intra = _dot(att, v)
        o_inter = _dot(q_st, s_old.astype(BF16))
        ds = _dotg(k_out, v, (0, 0))
        for hd in range(nh):
            tok = slice(hd * c, (hd + 1) * c)
            vcols = slice(hd * GLA_DV, (hd + 1) * GLA_DV)
            o_h = o_intra[tok, vcols] + o_inter[tok]
            if readout:
                o_h = o_h + of_ref[bi, rows, vcols]
                ms = jnp.mean(o_h * o_h, axis=-1, keepdims=True)
                y = o_h * lax.rsqrt(ms + NORM_EPS) * nw_ref[...]
                rr = r_ref[bi, rows, vcols]
                o_h = y * (rr * jax.nn.sigmoid(rr))
            o_ref[bi, rows, vcols] = o_h
        d_state = jnp.concatenate([ds[hd * GLA_DK:(hd + 1) * GLA_DK, hd * GLA_DV:(hd + 1) * GLA_DV]
                                   for hd in range(nh)], axis=0)
        s_scr[bi] = dec * s_old + d_state

    for ci in (range(n_chunks - 1, -1, -1) if reverse else range(n_chunks)):
        for bi in range(n_batch):
            chunk(bi, ci)

    @pl.when(i == pl.num_programs(0) - 1)
    def _():
        sfin_ref[...] = s_scr[...]


def _gla_call(qk, v, lg, s0, o_f, r, norm_w, *, reverse, readout, tb):
    b, l, _ = v.shape
    tb = min(tb, l)
    nb = l // tb
    idx = (lambda i: (0, nb - 1 - i, 0)) if reverse else (lambda i: (0, i, 0))
    tok = lambda w: pl.BlockSpec((b, tb, w), idx)
    lg_off = 1 if reverse else 0
    lg_spec = pl.BlockSpec((b, tb, GLA_QK), (lambda i: (0, nb - 1 - i, lg_off)) if reverse
                           else (lambda i: (0, i, lg_off)))
    state = pl.BlockSpec((b, GLA_QK, GLA_DV), lambda i: (0, 0, 0))
    ones = np.triu(np.ones((GLA_CHUNK, GLA_CHUNK), np.float32)) if reverse else \
        np.tril(np.ones((GLA_CHUNK, GLA_CHUNK), np.float32))
    return pl.pallas_call(
        functools.partial(_gla_kernel, reverse=reverse, readout=readout, n_chunks=tb // GLA_CHUNK, n_batch=b),
        grid=(nb,),
        in_specs=[tok(2 * GLA_QK), tok(GLA_V), lg_spec, state, _const_spec((GLA_CHUNK, GLA_CHUNK)),
                  tok(GLA_V), tok(GLA_V), _const_spec((1, GLA_DV))],
        out_specs=[tok(GLA_V), state],
        out_shape=[jax.ShapeDtypeStruct((b, l, GLA_V), F32), jax.ShapeDtypeStruct((b, GLA_QK, GLA_DV), F32)],
        scratch_shapes=[pltpu.VMEM((b, GLA_QK, GLA_DV), F32)],
        compiler_params=_params("arbitrary"),
        name="gla_bwd" if reverse else "gla_fwd",
    )(qk, v, lg, s0, jnp.asarray(ones, BF16), o_f, r, norm_w.reshape(1, GLA_DV))


def _gla_bidir(qk, v, lg, r, norm_w, s0_f, s0_b, *, tb):
    o_f, s_f = _gla_call(qk, v, lg, s0_f, v, r, norm_w, reverse=False, readout=False, tb=tb)
    y, s_b = _gla_call(qk, v, lg, s0_b, o_f, r, norm_w, reverse=True, readout=True, tb=tb)
    return y, s_f, s_b


def _s5_operators(a_re, a_im, log_dt, b_re, b_im, c_re, c_im):
    hp = lax.Precision.HIGHEST
    t = S5_T
    cr, ci_ = c_re.astype(F32), c_im.astype(F32)
    ks = jnp.arange(t + 1, dtype=F32)[:, None, None]
    per_dir = []
    for d in range(2):
        dt = jnp.exp(log_dt[d].astype(F32))[:, None]
        ar, ai = a_re[d].astype(F32), a_im[d].astype(F32)
        mag = jnp.exp(ks * dt * ar)
        pw_re, pw_im = mag * jnp.cos(ks * dt * ai), mag * jnp.sin(ks * dt * ai)
        den = ar * ar + ai * ai
        xr, xi = pw_re[1] - 1.0, pw_im[1]
        coef_re = (xr * ar + xi * ai) / den
        coef_im = (xi * ar - xr * ai) / den
        br, bi = b_re.astype(F32), b_im.astype(F32)
        bb_re = coef_re[..., None] * br - coef_im[..., None] * bi
        bb_im = coef_re[..., None] * bi + coef_im[..., None] * br
        ca_re = cr[None] * pw_re[:, :, None, :] - ci_[None] * pw_im[:, :, None, :]
        ca_im = cr[None] * pw_im[:, :, None, :] + ci_[None] * pw_re[:, :, None, :]
        kk = (jnp.einsum("kgop,gpi->kgoi", ca_re, bb_re, precision=hp)
              - jnp.einsum("kgop,gpi->kgoi", ca_im, bb_im, precision=hp))
        s_idx = jnp.arange(t)[:, None]
        t_idx = jnp.arange(t)[None, :]
        lag = (t_idx - s_idx) if d == 0 else (s_idx - t_idx)
        toe = jnp.where((lag >= 0)[:, :, None, None, None], kk[jnp.clip(lag, 0, t)], 0.0)
        m = jnp.transpose(toe, (2, 0, 4, 1, 3)).reshape(S5_GROUPS, S5_CW, S5_CW)
        p_pow = (t - 1 - jnp.arange(t)) if d == 0 else jnp.arange(t)
        pr, pi = pw_re[p_pow], pw_im[p_pow]
        p_re = pr[:, :, :, None] * bb_re[None] - pi[:, :, :, None] * bb_im[None]
        p_im = pr[:, :, :, None] * bb_im[None] + pi[:, :, :, None] * bb_re[None]
        flat_p = lambda x: jnp.transpose(x, (1, 0, 3, 2)).reshape(S5_GROUPS, S5_CW, S5_STATE)
        q_pow = (jnp.arange(t) + 1) if d == 0 else (t - jnp.arange(t))
        q_re, q_im = ca_re[q_pow], -ca_im[q_pow]
        flat_q = lambda x: jnp.transpose(x, (1, 3, 0, 2)).reshape(S5_GROUPS, S5_STATE, S5_CW)
        per_dir.append(dict(m=m, p_re=flat_p(p_re), p_im=flat_p(p_im), q_re=flat_q(q_re), q_im=flat_q(q_im),
                            at_re=pw_re[t], at_im=pw_im[t]))
    f, bk = per_dir
    zq = jnp.zeros_like(f["q_re"])
    p_cat = jnp.concatenate([f["p_re"], bk["p_re"], f["p_im"], bk["p_im"]], axis=-1)
    q_f = jnp.concatenate([f["q_re"], zq, f["q_im"], zq], axis=1)
    q_b = jnp.concatenate([zq, bk["q_re"], zq, bk["q_im"]], axis=1)
    at_re = jnp.concatenate([f["at_re"], bk["at_re"]], axis=-1)
    at_im = jnp.concatenate([f["at_im"], bk["at_im"]], axis=-1)

    sg, gpb, t, gc = S5_SG, S5_GPB, S5_T, S5_GROUP
    idx = jnp.arange(S5_UW)
    gl_tgc = (idx // gc) % gpb
    gl_gk = idx // S5_CW
    src = np.arange(S5_UW)
    spread = np.zeros((S5_CW, S5_UW), np.float32)
    spread[(src // LANES) * gc + src % gc, src] = 1.0
    spread = jnp.asarray(spread, BF16)

    def rows_tgc(a):
        k = a.shape[-1]
        return jnp.transpose(a.astype(BF16).reshape(sg, gpb, t, gc, k), (0, 2, 1, 3, 4)).reshape(sg, S5_UW, k)

    def cols_tgc(a):
        return lax.dot_general(a, spread, (((2,), (0,)), ((), ())), preferred_element_type=BF16)

    def keep(rows_gl, cols_gl, a):
        return jnp.where((rows_gl[:, None] == cols_gl[None, :])[None], a, jnp.zeros_like(a))

    p_rows = rows_tgc(p_cat)
    own = (gl_tgc[:, None] == jnp.arange(gpb)[None, :])[None, :, :, None]
    p_blk = jnp.where(own, p_rows[:, :, None, :], jnp.zeros((), BF16)).reshape(sg, S5_UW, S5_UW)
    m_blk = keep(gl_tgc, gl_tgc, cols_tgc(rows_tgc(f["m"] + bk["m"])))
    q_blk = keep(gl_gk, gl_tgc, cols_tgc((q_f + q_b).astype(BF16).reshape(sg, S5_UW, S5_CW)))
    return p_blk, m_blk, q_blk, at_re, at_im


def _s5_x_kernel(u_ref, p_ref, x_ref):
    x_ref[...] = _dot(u_ref[0], p_ref[0])


def _s5_x_call(uf, p_blk, layer, *, rt):
    sg, r, uw = uf.shape
    rt = min(rt, r)
    return pl.pallas_call(
        _s5_x_kernel,
        grid=(sg, r // rt),
        in_specs=[pl.BlockSpec((1, rt, uw), lambda si, i: (si, i, 0)),
                  pl.BlockSpec((None, 1, uw, uw), lambda si, i: (layer, si, 0, 0), pipeline_mode=pl.Buffered(1))],
        out_specs=pl.BlockSpec((rt, uw), lambda si, i: (i, si)),
        out_shape=jax.ShapeDtypeStruct((r, sg * uw), F32),
        compiler_params=_params("parallel", "parallel"),
        name="s5_x",
    )(uf, p_blk)


def _s5_scan_kernel(xf_ref, xb_ref, are_ref, aim_ref, h0_ref, hf_ref, hb_ref, hfin_ref, hr_scr, hi_scr, *, nb):
    i = pl.program_id(0)

    @pl.when(i == 0)
    def _():
        hr_scr[...] = h0_ref[:, 0]
        hi_scr[...] = h0_ref[:, 1]

    ar, ai = are_ref[...][None], aim_ref[...][None]
    fwd = lax.broadcasted_iota(jnp.int32, (1, S5_GROUPS, LANES), 2) < S5_STATE

    def step(j, carry):
        hr, hi = carry
        jb = nb - 1 - j
        xr = jnp.where(fwd, xf_ref[:, j, :, 0:LANES], xb_ref[:, jb, :, 0:LANES])
        xi = jnp.where(fwd, xf_ref[:, j, :, LANES:2 * LANES], xb_ref[:, jb, :, LANES:2 * LANES])
        hf_ref[:, j, :, 0:LANES] = hr
        hf_ref[:, j, :, LANES:2 * LANES] = hi
        hb_ref[:, jb, :, 0:LANES] = hr
        hb_ref[:, jb, :, LANES:2 * LANES] = hi
        return ar * hr - ai * hi + xr, ar * hi + ai * hr + xi

    hr, hi = lax.fori_loop(0, nb, step, (hr_scr[...], hi_scr[...]))
    hr_scr[...] = hr
    hi_scr[...] = hi

    @pl.when(i == pl.num_programs(0) - 1)
    def _():
        hfin_ref[:, 0] = hr
        hfin_ref[:, 1] = hi


def _s5_scan_call(x4, at_re, at_im, h0, *, nb):
    b, n, g, w = x4.shape
    nb = min(nb, n)
    steps = n // nb
    blk = lambda rev: pl.BlockSpec((b, nb, g, w), (lambda i: (0, steps - 1 - i, 0, 0)) if rev
                                   else (lambda i: (0, i, 0, 0)))
    st = pl.BlockSpec((b, 2, g, LANES), lambda i: (0, 0, 0, 0))
    return pl.pallas_call(
        functools.partial(_s5_scan_kernel, nb=nb),
        grid=(steps,),
        in_specs=[blk(False), blk(True), _const_spec((g, LANES)), _const_spec((g, LANES)), st],
        out_specs=[blk(False), blk(True), st],
        out_shape=[jax.ShapeDtypeStruct(x4.shape, F32), jax.ShapeDtypeStruct(x4.shape, F32),
                   jax.ShapeDtypeStruct((b, 2, g, LANES), F32)],
        scratch_shapes=[pltpu.VMEM((b, g, LANES), F32), pltpu.VMEM((b, g, LANES), F32)],
        compiler_params=_params("arbitrary"),
        name="s5_scan",
    )(x4, x4, at_re, at_im, h0)


def _s5_y_kernel(u_ref, hf_ref, hb_ref, m_ref, q_ref, y_ref):
    rt = hf_ref.shape[0]
    lane = lax.broadcasted_iota(jnp.int32, (rt, S5_UW), 1)
    h = jnp.where(lane % LANES < S5_STATE, hf_ref[...], hb_ref[...]).astype(BF16)
    y = _dot(u_ref[0], m_ref[0]) + _dot(h, q_ref[0])
    for t in range(S5_T):
        y_ref[pl.ds(t, rt, stride=S5_T), :] = y[:, t * LANES:(t + 1) * LANES]


def _s5_y_call(uf, hf2, hb2, m_blk, q_blk, layer, *, rt):
    sg, r, uw = uf.shape
    rt = min(rt, r)
    wspec = pl.BlockSpec((None, 1, uw, uw), lambda si, i: (layer, si, 0, 0), pipeline_mode=pl.Buffered(1))
    hspec = pl.BlockSpec((rt, uw), lambda si, i: (i, si))
    return pl.pallas_call(
        _s5_y_kernel,
        grid=(sg, r // rt),
        in_specs=[pl.BlockSpec((1, rt, uw), lambda si, i: (si, i, 0)), hspec, hspec, wspec, wspec],
        out_specs=pl.BlockSpec((rt * S5_T, LANES), lambda si, i: (i, si)),
        out_shape=jax.ShapeDtypeStruct((r * S5_T, sg * LANES), F32),
        compiler_params=_params("parallel", "parallel"),
        name="s5_y",
    )(uf, hf2, hb2, m_blk, q_blk)


def _s5_mix(uf, ops, layer, h0, bsz, *, nb, rt):
    p_blk, m_blk, q_blk, at_re, at_im = ops
    r = uf.shape[1]
    n = r // bsz
    x = _s5_x_call(uf, p_blk, layer, rt=rt)
    hf, hb, hfin = _s5_scan_call(x.reshape(bsz, n, S5_GROUPS, S5_CW), at_re[layer], at_im[layer], h0, nb=nb)
    y = _s5_y_call(uf, hf.reshape(r, S5_GROUPS * S5_CW), hb.reshape(r, S5_GROUPS * S5_CW),
                   m_blk, q_blk, layer, rt=rt)
    return y.reshape(bsz, n * S5_T, S5_WIDTH), hfin


def _attn_kernel(q_ref, k_ref, vt_ref, o_ref, m_scr, acc_scr, s_scr, mx_scr, *, tk, n_kv, n_str):
    grp, tq = q_ref.shape[2], q_ref.shape[3] // n_str
    cols = grp * tq
    q = [q_ref[0, 0, :, st * tq:(st + 1) * tq, :].reshape(cols, LANES) for st in range(n_str)]
    m_scr[...] = jnp.full(m_scr.shape, -jnp.inf, F32)
    acc_scr[...] = jnp.zeros(acc_scr.shape, F32)

    def scores(j, st, slot):
        off = pl.multiple_of(j * tk, tk)
        s = _dotg(k_ref[0, pl.ds(off, tk), :], q[st], (1, 1))
        s_scr[st, slot] = s
        mx_scr[st, slot] = jnp.broadcast_to(jnp.max(s, axis=0, keepdims=True), (8, cols))

    def softmax_pv(j, st, slot):
        off = pl.multiple_of(j * tk, tk)
        m_prev = m_scr[st]
        m_new = jnp.maximum(m_prev, mx_scr[st, slot])
        alpha = jnp.exp2(m_prev - m_new)
        p = jnp.exp2(s_scr[st, slot] - m_new[0:1]).astype(BF16)
        acc_scr[st] = acc_scr[st] * alpha[0:1] + _dot(vt_ref[0, 0, :, pl.ds(off, tk)], p)
        m_scr[st] = m_new

    def step(j_next, j_cur, slot_next, slot_cur):
        for st in range(n_str):
            if j_next is not None:
                scores(j_next, st, slot_next)
            if j_cur is not None:
                softmax_pv(j_cur, st, slot_cur)

    step(0, None, 0, None)

    def pair(t, carry):
        j = 2 * t
        step(j + 1, j, 1, 0)
        step(j + 2, j + 1, 0, 1)
        return carry

    lax.fori_loop(0, (n_kv - 1) // 2, pair, 0)
    if n_kv % 2 == 0:
        step(n_kv - 1, n_kv - 2, 1, 0)
        step(None, n_kv - 1, None, 1)
    else:
        step(None, n_kv - 1, None, 0)
    for st in range(n_str):
        acc = acc_scr[st]
        out_t = acc / acc[ATT_HEAD_DIM:ATT_HEAD_DIM + 1]
        for j in range(grp):
            o_ref[0, 0, j, st * tq:(st + 1) * tq, :] = out_t[:, j * tq:(j + 1) * tq].T.astype(o_ref.dtype)


def _attn_call(qs, k, vt_ext, *, tq, tk, n_str):
    b, kvh, grp, l, _ = qs.shape
    lk = k.shape[1]
    tq, tk = min(tq, l // n_str), min(tk, lk)
    cols = grp * tq
    blk = pl.BlockSpec((1, 1, grp, n_str * tq, LANES), lambda bi, ki, i: (bi, ki, 0, i, 0))
    return pl.pallas_call(
        functools.partial(_attn_kernel, tk=tk, n_kv=lk // tk, n_str=n_str),
        grid=(b, kvh, l // (n_str * tq)),
        in_specs=[blk,
                  pl.BlockSpec((1, lk, LANES), lambda bi, ki, i: (bi, 0, 0), pipeline_mode=pl.Buffered(1)),
                  pl.BlockSpec((1, 1, LANES, lk), lambda bi, ki, i: (bi, ki, 0, 0), pipeline_mode=pl.Buffered(1))],
        out_specs=blk,
        out_shape=jax.ShapeDtypeStruct(qs.shape, BF16),
        scratch_shapes=[pltpu.VMEM((n_str, 8, cols), F32), pltpu.VMEM((n_str, LANES, cols), F32),
                        pltpu.VMEM((n_str, 2, tk, cols), F32), pltpu.VMEM((n_str, 2, 8, cols), F32)],
        compiler_params=_params("parallel", "parallel", "arbitrary"),
        name="attention",
    )(qs, k, vt_ext)


def _kv_layout(ak, av):
    b, lk, _ = av.shape
    vt = jnp.transpose(av.reshape(b, lk, ATT_KV_HEADS, ATT_HEAD_DIM), (0, 2, 3, 1))
    return ak, jnp.concatenate([vt, jnp.ones_like(vt)], axis=2)


def _merge_kernel(h_ref, mod_ref, gn_ref, yg_ref, ys_ref, u_ref, ya_ref, wbg_ref, bbg_ref, wp_ref, wpa_ref,
                  wo_ref, d_ref, gw_ref, gb_ref, o_ref):
    h = h_ref[0]
    d = h.shape[-1]
    n = _rms_mod(h, gn_ref[...], mod_ref[0, 0:1], mod_ref[0, 1:2]).astype(BF16)
    g = jax.nn.sigmoid(_dot(n, wbg_ref[...]) + bbg_ref[...])
    y = jax.nn.gelu(ys_ref[0] + d_ref[...] * u_ref[0])
    y_s5 = y * jax.nn.sigmoid(_dot(y.astype(BF16), gw_ref[...]) + gb_ref[...])
    ya = jnp.concatenate([ya_ref[0, kv, j] for kv in range(ATT_KV_HEADS) for j in range(ATT_GROUP)], axis=-1)
    m = (g[:, 0:d] * _dot(yg_ref[0].astype(BF16), wp_ref[0])
         + g[:, d:2 * d] * _dot(y_s5.astype(BF16), wp_ref[1])
         + g[:, 2 * d:3 * d] * _dot(ya, wpa_ref[...]))
    o_ref[0] = h + mod_ref[0, 2:3] * _dot(m.astype(BF16), wo_ref[...])


def _merge_call(h, mod3, g_norm, y_gla, y_s5raw, u, y_att, lw, *, tm):
    b, l, d = h.shape
    tm = min(tm, l)
    tok = lambda w: pl.BlockSpec((1, tm, w), lambda bi, i: (bi, i, 0))
    bw = y_gla.shape[-1]
    att = pl.BlockSpec((1, ATT_KV_HEADS, ATT_GROUP, tm, LANES), lambda bi, i: (bi, 0, 0, i, 0))
    return pl.pallas_call(
        _merge_kernel,
        grid=(b, l // tm),
        in_specs=[tok(d), pl.BlockSpec((1, 3, d), lambda bi, i: (bi, 0, 0)), _const_spec((1, d)),
                  tok(bw), tok(bw), tok(bw), att,
                  _const_spec((d, 3 * d)), _const_spec((1, 3 * d)), _const_spec((2, bw, d)),
                  _const_spec((ATT_HEADS * LANES, d)), _const_spec((d, d)),
                  _const_spec((1, bw)), _const_spec((bw, bw)), _const_spec((1, bw))],
        out_specs=tok(d),
        out_shape=jax.ShapeDtypeStruct((b, l, d), F32),
        compiler_params=_params("parallel", "parallel"),
        name="merge",
    )(h, mod3, g_norm.reshape(1, d), y_gla, y_s5raw, u, y_att, lw["w_bgate"], lw["b_bgate"], lw["w_bproj"],
      lw["w_aproj"], lw["w_out"], lw["s5_d"], lw["glu_w"], lw["glu_b"])


def _rope_tables(n_tokens):
    rows = n_tokens // GRID_W
    row = jnp.repeat(jnp.arange(rows, dtype=F32), GRID_W)
    col = jnp.tile(jnp.arange(GRID_W, dtype=F32), rows)
    inv = ROPE_THETA ** (-jnp.arange(ROPE_PAIRS, dtype=F32) / ROPE_PAIRS)
    ang_r, ang_c = row[:, None] * inv, col[:, None] * inv
    zero = jnp.zeros_like(ang_r)
    cos = jnp.concatenate([jnp.cos(ang_r), jnp.cos(ang_r), jnp.cos(ang_c), jnp.cos(ang_c)], axis=-1)
    sin_a = jnp.concatenate([-jnp.sin(ang_r), zero, -jnp.sin(ang_c), zero], axis=-1)
    sin_b = jnp.concatenate([zero, jnp.sin(ang_r), zero, jnp.sin(ang_c)], axis=-1)
    tile = lambda t: jnp.tile(t, (1, ATT_HEADS))
    return tile(cos), tile(sin_a), tile(sin_b)


def _block_diag_ones(width, seg):
    idx = np.arange(width) // seg
    return jnp.asarray((idx[:, None] == idx[None, :]).astype(np.float32), BF16)


def _layer_weights(i, w_in, gla_gate_w, gla_gate_b, attn_q_norm_w, attn_k_norm_w, w_branch_gate,
                   b_branch_gate, w_branch_proj, w_out, s5_d, s5_glu_w, s5_glu_b):
    offs = np.concatenate([[0], np.cumsum(IN_WIDTHS)])
    col = lambda k: w_in[i][:, offs[k]:offs[k + 1]]
    w_main = jnp.concatenate([col(0), col(1), col(2), col(5), col(6), col(7), col(8), col(9)], axis=1)
    d = w_in.shape[1]
    rk = GLA_GATE_RANK
    w_gz = jnp.zeros((d, LANES), F32).at[:, 0:rk].set(col(3)).at[:, rk:2 * rk].set(col(4))
    gw = jnp.zeros((LANES, 2 * GLA_QK), F32)
    gw = gw.at[0:rk, 0:GLA_QK].set(gla_gate_w[i, 0]).at[rk:2 * rk, GLA_QK:].set(gla_gate_w[i, 1])
    wa = w_branch_proj[i, 2].reshape(ATT_HEADS, ATT_HEAD_DIM, d)
    w_aproj = jnp.concatenate([wa, jnp.zeros_like(wa)], axis=1).reshape(ATT_HEADS * LANES, d)
    return dict(
        w_aproj=w_aproj.astype(BF16),
        w_main=w_main.astype(BF16), w_gz=w_gz.astype(BF16), gate_w_bd=gw.astype(BF16),
        gate_b=gla_gate_b[i].reshape(1, 2 * GLA_QK),
        q_norm_w=(jnp.tile(attn_q_norm_w[i], ATT_HEADS) * (ATT_HEAD_DIM ** -0.5 * LOG2_E)).reshape(1, ATT_Q),
        k_norm_w=jnp.tile(attn_k_norm_w[i], ATT_KV_HEADS).reshape(1, ATT_KV),
        bd_q=_block_diag_ones(ATT_Q, ATT_HEAD_DIM), bd_k=_block_diag_ones(ATT_KV, ATT_HEAD_DIM),
        w_bgate=w_branch_gate[i].astype(BF16), b_bgate=b_branch_gate[i].reshape(1, -1),
        w_bproj=w_branch_proj[i, 0:2].astype(BF16), w_out=w_out[i].astype(BF16),
        s5_d=s5_d[i].reshape(1, -1), glu_w=s5_glu_w[i].astype(BF16), glu_b=s5_glu_b[i].reshape(1, -1),
    )


def _mixer_inputs(h, mod, g_norm, lw, tables, *, rope, tm):
    mod2 = mod[:, 3:5]
    return _inproj_call(h, mod2, g_norm, lw, tables, rope=rope, tm=tm)


def kernel(x, c, ctx, c_ctx, w_ada, b_ada, norm_w, w_ffn_gate, w_ffn_up, w_ffn_down, w_in, gla_gate_w,
           gla_gate_b, gla_norm_w, s5_a_re, s5_a_im, s5_log_dt, s5_b_re, s5_b_im, s5_c_re, s5_c_im, s5_d,
           s5_glu_w, s5_glu_b, attn_q_norm_w, attn_k_norm_w, w_branch_gate, b_branch_gate, w_branch_proj,
           w_out, final_norm_w):
    bsz, seq, d = x.shape
    depth = w_ada.shape[0]
    tables = _rope_tables(seq)
    ctx_tables = tuple(t[0:ctx.shape[1]] for t in tables)
    cond = jnp.zeros((8, d), F32).at[0:bsz].set(c).at[bsz].set(c_ctx)
    tm_ffn, tm_in, tm_merge, tb_gla, nb_s5, rt_s5, tq, tk = (
        TILES[k] for k in ("ffn", "inproj", "merge", "gla", "s5_scan", "s5_rows", "att_q", "att_k"))

    ffn_w = (w_ffn_gate.astype(BF16), w_ffn_up.astype(BF16), w_ffn_down.astype(BF16))
    s5_ops = jax.vmap(_s5_operators)(s5_a_re, s5_a_im, s5_log_dt, s5_b_re, s5_b_im, s5_c_re, s5_c_im)
    h_lat, h_ctx = x, ctx
    for i in range(depth):
        last = i == depth - 1
        mods = _ada_call(cond, w_ada, b_ada, i)
        mod_l = mods[0:bsz].reshape(bsz, N_MOD, d)
        mod_c = jnp.broadcast_to(mods[bsz].reshape(1, N_MOD, d), (bsz, N_MOD, d))
        lw = _layer_weights(i, w_in, gla_gate_w, gla_gate_b, attn_q_norm_w, attn_k_norm_w, w_branch_gate,
                            b_branch_gate, w_branch_proj, w_out, s5_d, s5_glu_w, s5_glu_b)

        def ffn(h, mod, j, fin=False):
            return _ffn_call(h, mod[:, 6 * j:6 * j + 3], norm_w[i, 2 * j], *ffn_w, final_norm_w,
                             layer=i, sub=j, final_norm=fin, tm=tm_ffn)

        h_lat = ffn(h_lat, mod_l, 0)
        h_ctx = ffn(h_ctx, mod_c, 0)

        qk_c, v_c, r_c, lg_c, u_c, uf_c, aq_c, ak_c, av_c = _mixer_inputs(h_ctx, mod_c, norm_w[i, 1], lw,
                                                                          ctx_tables, rope=False, tm=tm_in)
        qk_l, v_l, r_l, lg_l, u_l, uf_l, aq_l, ak_l, av_l = _mixer_inputs(h_lat, mod_l, norm_w[i, 1], lw,
                                                                          tables, rope=True, tm=tm_in)
        s0 = jnp.zeros((bsz, GLA_QK, GLA_DV), F32)
        yg_c, sf_c, sb_c = _gla_bidir(qk_c, v_c, lg_c, r_c, gla_norm_w[i], s0, s0, tb=tb_gla)
        yg_l, _, _ = _gla_bidir(qk_l, v_l, lg_l, r_l, gla_norm_w[i], sf_c, sb_c, tb=tb_gla)
        h0 = jnp.zeros((bsz, 2, S5_GROUPS, LANES), F32)
        ys_c, hfin_c = _s5_mix(uf_c, s5_ops, i, h0, bsz, nb=nb_s5, rt=rt_s5)
        ys_l, _ = _s5_mix(uf_l, s5_ops, i, hfin_c, bsz, nb=nb_s5, rt=rt_s5)
        kt, v_ext = _kv_layout(jnp.concatenate([ak_c, ak_l], axis=1), jnp.concatenate([av_c, av_l], axis=1))
        ya_l = _attn_call(aq_l, kt, v_ext, tq=tq, tk=tk, n_str=TILES["att_streams"])
        h_lat = _merge_call(h_lat, mod_l[:, 3:6], norm_w[i, 1], yg_l, ys_l, u_l, ya_l, lw, tm=tm_merge)
        h_lat = ffn(h_lat, mod_l, 1, last)
        if not last:
            kt_c, v_ext_c = _kv_layout(ak_c, av_c)
            ya_c = _attn_call(aq_c, kt_c, v_ext_c, tq=tq, tk=tk, n_str=TILES["att_streams"])
            h_ctx = _merge_call(h_ctx, mod_c[:, 3:6], norm_w[i, 1], yg_c, ys_c, u_c, ya_c, lw, tm=tm_merge)
            h_ctx = ffn(h_ctx, mod_c, 1)
    return h_lat
```

```python
import functools

import jax
import jax.numpy as jnp
import numpy as np
from jax import lax
from jax.experimental import pallas as pl
from jax.experimental.pallas import tpu as pltpu

F32 = jnp.float32
BF16 = jnp.bfloat16

N_MOD = 9
MIX_MOD = 3
MACARON_WEIGHT = 0.5
NORM_EPS = 1e-6
GRID_W = 64
GLA_HEADS = 4
GLA_DK = 64
GLA_DV = 128
GLA_QK = GLA_HEADS * GLA_DK
GLA_V = GLA_HEADS * GLA_DV
GLA_GATE_RANK = 16
GLA_TAU = 16.0
GLA_CHUNK = 64
S5_WIDTH = 512
S5_GROUP = 16
S5_GROUPS = S5_WIDTH // S5_GROUP
S5_STATE = 64
ATT_HEADS = 8
ATT_KV_HEADS = 2
ATT_GROUP = ATT_HEADS // ATT_KV_HEADS
ATT_HEAD_DIM = 64
ATT_Q = ATT_HEADS * ATT_HEAD_DIM
ATT_KV = ATT_KV_HEADS * ATT_HEAD_DIM
ROPE_PAIRS = ATT_HEAD_DIM // 4
ROPE_THETA = 10000.0
IN_WIDTHS = (GLA_QK, GLA_QK, GLA_V, GLA_GATE_RANK, GLA_GATE_RANK, GLA_V, S5_WIDTH, ATT_Q, ATT_KV, ATT_KV)

LANES = 128
SUBLANES = 8
MXU_DIM = 256
VMEM_LIMIT = 56 * 1024 * 1024

LOG2_E = 1.4426950408889634

S5_T = MXU_DIM // S5_GROUP
S5_CW = S5_T * S5_GROUP
S5_GPB = LANES // S5_GROUP
S5_SG = S5_GROUPS // S5_GPB
S5_UW = S5_T * LANES

TILES = dict(ffn=512, inproj=512, merge=512, gla=512, s5_scan=32, s5_rows=256, att_q=128, att_k=1280,
             att_streams=4)


def _params(*sem):
    return pltpu.CompilerParams(dimension_semantics=sem, vmem_limit_bytes=VMEM_LIMIT)


def _const_spec(shape):
    nd = len(shape)
    return pl.BlockSpec(shape, lambda *_: (0,) * nd, pipeline_mode=pl.Buffered(1))


def _mod_spec(d, mod_row):
    if mod_row is None:
        return pl.BlockSpec((1, N_MOD, d), lambda bi, i: (bi, 0, 0))
    return pl.BlockSpec((1, N_MOD, d), lambda bi, i: (mod_row, 0, 0))


def _norm_spec(d, layer):
    return pl.BlockSpec((None, 3, d), lambda bi, i: (layer, 0, 0), pipeline_mode=pl.Buffered(1))


def _dotg(a, b, contract):
    return lax.dot_general(a, b, (((contract[0],), (contract[1],)), ((), ())),
                           precision=lax.Precision.DEFAULT, preferred_element_type=F32)


def _dot(a, b):
    return _dotg(a, b, (1, 0))


def _dot_hi_lo(x, w):
    hi = x.astype(BF16)
    lo = (x - hi.astype(F32)).astype(BF16)
    return _dot(hi, w) + _dot(lo, w)


def _rms_mod(h, g_norm, shift, scale):
    ms = jnp.mean(h * h, axis=-1, keepdims=True)
    y = h * lax.rsqrt(ms + NORM_EPS) * g_norm
    return y * (1.0 + scale) + shift


def _log_sigmoid(x):
    return jnp.minimum(x, 0.0) - jnp.log(1.0 + jnp.exp(-jnp.abs(x)))


def _ada_kernel(c_ref, w_ref, b_ref, o_ref):
    c = c_ref[...]
    sc = (c * jax.nn.sigmoid(c)).astype(BF16)
    o_ref[...] = _dot(sc, w_ref[...].astype(BF16)) + b_ref[...]


def _ada_call(cond, w_all, b_all, layer):
    rows, d = cond.shape
    depth, _, n = w_all.shape
    tn = 9 * LANES
    return pl.pallas_call(
        _ada_kernel,
        grid=(n // tn,),
        in_specs=[pl.BlockSpec((rows, d), lambda j: (0, 0)),
                  pl.BlockSpec((None, d, tn), lambda j: (layer, 0, j)),
                  pl.BlockSpec((None, 1, tn), lambda j: (layer, 0, j))],
        out_specs=pl.BlockSpec((rows, tn), lambda j: (0, j)),
        out_shape=jax.ShapeDtypeStruct((rows, n), F32),
        compiler_params=_params("arbitrary"),
        name="adaln",
    )(cond, w_all, b_all.reshape(depth, 1, n))


def _ffn_kernel(h_ref, mod_ref, gn_ref, wg_ref, wu_ref, wd_ref, fn_ref, o_ref, *, mod_base, final_norm):
    h = h_ref[0]
    shift, scale, gate = (mod_ref[0, mod_base + k:mod_base + k + 1] for k in range(3))
    norm_row = mod_base // 3
    n = _rms_mod(h, gn_ref[norm_row:norm_row + 1], shift, scale).astype(BF16)
    g = _dot(n, wg_ref[...])
    u = _dot(n, wu_ref[...])
    a = (g * jax.nn.sigmoid(g) * u).astype(BF16)
    y = h + (MACARON_WEIGHT * gate) * _dot(a, wd_ref[...])
    if final_norm:
        ms = jnp.mean(y * y, axis=-1, keepdims=True)
        y = y * lax.rsqrt(ms + NORM_EPS) * fn_ref[...]
    o_ref[0] = y


def _ffn_call(h, mods, norm_w, wg, wu, wd, final_w, *, layer, sub, mod_row, final_norm, tm):
    b, l, d = h.shape
    f = wg.shape[-1]
    tm = min(tm, l)
    wspec = lambda r, c: pl.BlockSpec((None, None, r, c), lambda bi, i: (layer, sub, 0, 0),
                                      pipeline_mode=pl.Buffered(1))
    return pl.pallas_call(
        functools.partial(_ffn_kernel, mod_base=6 * sub, final_norm=final_norm),
        grid=(b, l // tm),
        in_specs=[pl.BlockSpec((1, tm, d), lambda bi, i: (bi, i, 0)),
                  _mod_spec(d, mod_row), _norm_spec(d, layer),
                  wspec(d, f), wspec(d, f), wspec(f, d), _const_spec((1, d))],
        out_specs=pl.BlockSpec((1, tm, d), lambda bi, i: (bi, i, 0)),
        out_shape=jax.ShapeDtypeStruct((b, l, d), F32),
        compiler_params=_params("parallel", "parallel"),
        name="ffn",
    )(h, mods, norm_w, wg, wu, wd, final_w.reshape(1, d))


def _inproj_kernel(h_ref, mod_ref, gn_ref, wm_ref, wgz_ref, gw_ref, gb_ref, qw_ref, kw_ref,
                   bdq_ref, bdk_ref, cos_ref, sa_ref, sb_ref,
                   qk_ref, v_ref, r_ref, lg_ref, u_ref, uf_ref, aq_ref, ak_ref, av_ref, us_scr, *, rope):
    h = h_ref[0]
    tm = h.shape[0]
    n = _rms_mod(h, gn_ref[1:2], mod_ref[0, MIX_MOD:MIX_MOD + 1], mod_ref[0, MIX_MOD + 1:MIX_MOD + 2]).astype(BF16)
    z = _dot(n, wm_ref[...])
    o = 0
    q = z[:, o:o + GLA_QK] * (GLA_DK ** -0.5); o += GLA_QK
    k = z[:, o:o + GLA_QK]; o += GLA_QK
    qk_ref[0, :, 0:GLA_QK] = q
    qk_ref[0, :, GLA_QK:2 * GLA_QK] = k
    v_ref[0] = z[:, o:o + GLA_V]; o += GLA_V
    r_ref[0] = z[:, o:o + GLA_V]; o += GLA_V
    u_ref[0] = z[:, o:o + S5_WIDTH]; o += S5_WIDTH
    for sg in range(S5_SG):
        us_scr[sg] = z[:, o - S5_WIDTH + sg * LANES:o - S5_WIDTH + (sg + 1) * LANES]
        for t in range(S5_T):
            piece = us_scr[sg, pl.ds(t, tm // S5_T, stride=S5_T), :]
            uf_ref[sg, :, t * LANES:(t + 1) * LANES] = piece.astype(BF16)
    xq = z[:, o:o + ATT_Q]; o += ATT_Q
    xk = z[:, o:o + ATT_KV]; o += ATT_KV
    av_ref[0] = z[:, o:o + ATT_KV].astype(BF16)

    zg = _dot(n, wgz_ref[...]).astype(BF16)
    lg_ref[0] = _log_sigmoid(_dot(zg, gw_ref[...]) + gb_ref[...]) * (1.0 / GLA_TAU)

    def head_norm(x, bd_ref, w_ref):
        ss = _dot_hi_lo(x * x, bd_ref[...])
        return x * lax.rsqrt(ss * (1.0 / ATT_HEAD_DIM) + NORM_EPS) * w_ref[...]

    def rotary(x, width):
        if not rope:
            return x
        c, sa, sb = cos_ref[:, 0:width], sa_ref[:, 0:width], sb_ref[:, 0:width]
        return (x * c + pltpu.roll(x, width - ROPE_PAIRS, 1) * sa + pltpu.roll(x, ROPE_PAIRS, 1) * sb)

    ak_ref[0] = rotary(head_norm(xk, bdk_ref, kw_ref), ATT_KV).astype(BF16)
    aq = rotary(head_norm(xq, bdq_ref, qw_ref), ATT_Q)
    lane = lax.broadcasted_iota(jnp.int32, (tm, LANES), 1)
    for kv in range(ATT_KV_HEADS):
        keep = (lane >= kv * ATT_HEAD_DIM) & (lane < (kv + 1) * ATT_HEAD_DIM)
        for j in range(ATT_GROUP):
            hd = kv * ATT_GROUP + j
            blk = aq[:, (hd // 2) * LANES:(hd // 2 + 1) * LANES]
            if hd % 2 != kv:
                blk = pltpu.roll(blk, ATT_HEAD_DIM, 1)
            aq_ref[0, kv, j] = jnp.where(keep, blk, 0.0).astype(BF16)


def _inproj_call(h, mods, norm_w, lw, tables, *, layer, mod_row, rope, tm):
    b, l, d = h.shape
    tm = min(tm, l)
    tok = lambda w: pl.BlockSpec((1, tm, w), lambda bi, i: (bi, i, 0))
    tab = pl.BlockSpec((tm, ATT_Q), lambda bi, i: (i, 0))
    wm = lw["w_main"]
    nt = l // tm
    tok_out = lambda w, dt: (tok(w), jax.ShapeDtypeStruct((b, l, w), dt))
    outs = [tok_out(2 * GLA_QK, F32), tok_out(GLA_V, F32), tok_out(GLA_V, F32), tok_out(2 * GLA_QK, F32),
            tok_out(S5_WIDTH, F32),
            (pl.BlockSpec((S5_SG, tm // S5_T, S5_UW), lambda bi, i: (0, bi * nt + i, 0)),
             jax.ShapeDtypeStruct((S5_SG, b * l // S5_T, S5_UW), BF16)),
            (pl.BlockSpec((1, ATT_KV_HEADS, ATT_GROUP, tm, LANES), lambda bi, i: (bi, 0, 0, i, 0)),
             jax.ShapeDtypeStruct((b, ATT_KV_HEADS, ATT_GROUP, l, LANES), BF16)),
            tok_out(ATT_KV, BF16), tok_out(ATT_KV, BF16)]
    return pl.pallas_call(
        functools.partial(_inproj_kernel, rope=rope),
        grid=(b, nt),
        in_specs=[tok(d), _mod_spec(d, mod_row), _norm_spec(d, layer),
                  _const_spec(wm.shape), _const_spec((d, LANES)), _const_spec((LANES, 2 * GLA_QK)),
                  _const_spec((1, 2 * GLA_QK)), _const_spec((1, ATT_Q)), _const_spec((1, ATT_KV)),
                  _const_spec((ATT_Q, ATT_Q)), _const_spec((ATT_KV, ATT_KV)), tab, tab, tab],
        out_specs=[s for s, _ in outs],
        out_shape=[t for _, t in outs],
        scratch_shapes=[pltpu.VMEM((S5_SG, tm, LANES), F32)],
        compiler_params=_params("parallel", "parallel"),
        name="inproj",
    )(h, mods, norm_w, wm, lw["w_gz"], lw["gate_w_bd"], lw["gate_b"], lw["q_norm_w"],
      lw["k_norm_w"], lw["bd_q"], lw["bd_k"], *tables)


def _gla_kernel(qk_ref, v_ref, lg_ref, s0_ref, tri_ref, of_ref, r_ref, nw_ref, o_ref, sfin_ref, s_scr,
                *, reverse, readout, n_chunks, n_batch):
    i = pl.program_id(0)

    @pl.when(i == 0)
    def _():
        s_scr[...] = s0_ref[...]

    tri = tri_ref[...]
    c, nh = GLA_CHUNK, GLA_HEADS
    row = lax.broadcasted_iota(jnp.int32, (nh * c, c), 0) % c
    col = lax.broadcasted_iota(jnp.int32, (nh * c, c), 1)
    keep = (col >= row) if reverse else (col <= row)
    head_of_lane = lax.broadcasted_iota(jnp.int32, (c, GLA_QK), 1) // GLA_DK

    def chunk(bi, ci):
        rows = slice(ci * c, (ci + 1) * c)
        g = lg_ref[bi, rows, :]
        g_hi = g.astype(BF16)
        b = _dot(tri, g_hi) + _dot(tri, (g - g_hi.astype(F32)).astype(BF16))
        b_last = b[0:1, :] if reverse else b[c - 1:c, :]
        q_in = qk_ref[bi, rows, 0:GLA_QK] * jnp.exp(b)
        k = qk_ref[bi, rows, GLA_QK:2 * GLA_QK]
        k_in = (k * jnp.exp(-b)).astype(BF16)
        k_out = (k * jnp.exp(b_last - b)).astype(BF16)
        dec = jnp.exp(jnp.broadcast_to(b_last, (LANES, GLA_QK)).T)
        v = v_ref[bi, rows, :].astype(BF16)
        s_old = s_scr[bi]
        q_st = jnp.concatenate([jnp.where(head_of_lane == hd, q_in, 0.0) for hd in range(nh)],
                               axis=0).astype(BF16)
        att = _dotg(q_st, k_in, (1, 1))
        att = jnp.where(keep, att, 0.0).astype(BF16)
        o_intra = _dot(att, v)
        o_inter = _dot(q_st, s_old.astype(BF16))
        ds = _dotg(k_out, v, (0, 0))
        for hd in range(nh):
            tok = slice(hd * c, (hd + 1) * c)
            vcols = slice(hd * GLA_DV, (hd + 1) * GLA_DV)
            o_h = o_intra[tok, vcols] + o_inter[tok]
            if readout:
                o_h = o_h + of_ref[bi, rows, vcols]
                ms = jnp.mean(o_h * o_h, axis=-1, keepdims=True)
                y = o_h * lax.rsqrt(ms + NORM_EPS) * nw_ref[...]
                rr = r_ref[bi, rows, vcols]
                o_h = y * (rr * jax.nn.sigmoid(rr))
            o_ref[bi, rows, vcols] = o_h
        d_state = jnp.concatenate([ds[hd * GLA_DK:(hd + 1) * GLA_DK, hd * GLA_DV:(hd + 1) * GLA_DV]
                                   for hd in range(nh)], axis=0)
        s_scr[bi] = dec * s_old + d_state

    for ci in (range(n_chunks - 1, -1, -1) if reverse else range(n_chunks)):
        for bi in range(n_batch):
            chunk(bi, ci)

    @pl.when(i == pl.num_programs(0) - 1)
    def _():
        sfin_ref[...] = s_scr[...]


def _gla_call(qk, v, lg, s0, o_f, r, norm_w, *, reverse, readout, tb):
    b, l, _ = v.shape
    tb = min(tb, l)
    nb = l // tb
    idx = (lambda i: (0, nb - 1 - i, 0)) if reverse else (lambda i: (0, i, 0))
    tok = lambda w: pl.BlockSpec((b, tb, w), idx)
    lg_off = 1 if reverse else 0
    lg_spec = pl.BlockSpec((b, tb, GLA_QK), (lambda i: (0, nb - 1 - i, lg_off)) if reverse
                           else (lambda i: (0, i, lg_off)))
    state = pl.BlockSpec((b, GLA_QK, GLA_DV), lambda i: (0, 0, 0))
    ones = np.triu(np.ones((GLA_CHUNK, GLA_CHUNK), np.float32)) if reverse else \
        np.tril(np.ones((GLA_CHUNK, GLA_CHUNK), np.float32))
    return pl.pallas_call(
        functools.partial(_gla_kernel, reverse=reverse, readout=readout, n_chunks=tb // GLA_CHUNK, n_batch=b),
        grid=(nb,),
        in_specs=[tok(2 * GLA_QK), tok(GLA_V), lg_spec, state, _const_spec((GLA_CHUNK, GLA_CHUNK)),
                  tok(GLA_V), tok(GLA_V), _const_spec((1, GLA_DV))],
        out_specs=[tok(GLA_V), state],
        out_shape=[jax.ShapeDtypeStruct((b, l, GLA_V), F32), jax.ShapeDtypeStruct((b, GLA_QK, GLA_DV), F32)],
        scratch_shapes=[pltpu.VMEM((b, GLA_QK, GLA_DV), F32)],
        compiler_params=_params("arbitrary"),
        name="gla_bwd" if reverse else "gla_fwd",
    )(qk, v, lg, s0, jnp.asarray(ones, BF16), o_f, r, norm_w.reshape(1, GLA_DV))


def _gla_bidir(qk, v, lg, r, norm_w, s0_f, s0_b, *, tb):
    o_f, s_f = _gla_call(qk, v, lg, s0_f, v, r, norm_w, reverse=False, readout=False, tb=tb)
    y, s_b = _gla_call(qk, v, lg, s0_b, o_f, r, norm_w, reverse=True, readout=True, tb=tb)
    return y, s_f, s_b


def _s5_operators(a_re, a_im, log_dt, b_re, b_im, c_re, c_im):
    hp = lax.Precision.HIGHEST
    t = S5_T
    cr, ci_ = c_re.astype(F32), c_im.astype(F32)
    ks = jnp.arange(t + 1, dtype=F32)[:, None, None]
    per_dir = []
    for d in range(2):
        dt = jnp.exp(log_dt[d].astype(F32))[:, None]
        ar, ai = a_re[d].astype(F32), a_im[d].astype(F32)
        mag = jnp.exp(ks * dt * ar)
        pw_re, pw_im = mag * jnp.cos(ks * dt * ai), mag * jnp.sin(ks * dt * ai)
        den = ar * ar + ai * ai
        xr, xi = pw_re[1] - 1.0, pw_im[1]
        coef_re = (xr * ar + xi * ai) / den
        coef_im = (xi * ar - xr * ai) / den
        br, bi = b_re.astype(F32), b_im.astype(F32)
        bb_re = coef_re[..., None] * br - coef_im[..., None] * bi
        bb_im = coef_re[..., None] * bi + coef_im[..., None] * br
        ca_re = cr[None] * pw_re[:, :, None, :] - ci_[None] * pw_im[:, :, None, :]
        ca_im = cr[None] * pw_im[:, :, None, :] + ci_[None] * pw_re[:, :, None, :]
        kk = (jnp.einsum("kgop,gpi->kgoi", ca_re, bb_re, precision=hp)
              - jnp.einsum("kgop,gpi->kgoi", ca_im, bb_im, precision=hp))
        s_idx = jnp.arange(t)[:, None]
        t_idx = jnp.arange(t)[None, :]
        lag = (t_idx - s_idx) if d == 0 else (s_idx - t_idx)
        toe = jnp.where((lag >= 0)[:, :, None, None, None], kk[jnp.clip(lag, 0, t)], 0.0)
        m = jnp.transpose(toe, (2, 0, 4, 1, 3)).reshape(S5_GROUPS, S5_CW, S5_CW)
        p_pow = (t - 1 - jnp.arange(t)) if d == 0 else jnp.arange(t)
        pr, pi = pw_re[p_pow], pw_im[p_pow]
        p_re = pr[:, :, :, None] * bb_re[None] - pi[:, :, :, None] * bb_im[None]
        p_im = pr[:, :, :, None] * bb_im[None] + pi[:, :, :, None] * bb_re[None]
        flat_p = lambda x: jnp.transpose(x, (1, 0, 3, 2)).reshape(S5_GROUPS, S5_CW, S5_STATE)
        q_pow = (jnp.arange(t) + 1) if d == 0 else (t - jnp.arange(t))
        q_re, q_im = ca_re[q_pow], -ca_im[q_pow]
        flat_q = lambda x: jnp.transpose(x, (1, 3, 0, 2)).reshape(S5_GROUPS, S5_STATE, S5_CW)
        per_dir.append(dict(m=m, p_re=flat_p(p_re), p_im=flat_p(p_im), q_re=flat_q(q_re), q_im=flat_q(q_im),
                            at_re=pw_re[t], at_im=pw_im[t]))
    f, bk = per_dir
    zq = jnp.zeros_like(f["q_re"])
    p_cat = jnp.concatenate([f["p_re"], bk["p_re"], f["p_im"], bk["p_im"]], axis=-1)
    q_f = jnp.concatenate([f["q_re"], zq, f["q_im"], zq], axis=1)
    q_b = jnp.concatenate([zq, bk["q_re"], zq, bk["q_im"]], axis=1)
    at_re = jnp.concatenate([f["at_re"], bk["at_re"]], axis=-1)
    at_im = jnp.concatenate([f["at_im"], bk["at_im"]], axis=-1)

    sg, gpb, t, gc = S5_SG, S5_GPB, S5_T, S5_GROUP
    idx = jnp.arange(S5_UW)
    gl_tgc = (idx // gc) % gpb
    gl_gk = idx // S5_CW
    src = np.arange(S5_UW)
    spread = np.zeros((S5_CW, S5_UW), np.float32)
    spread[(src // LANES) * gc + src % gc, src] = 1.0
    spread = jnp.asarray(spread, BF16)

    def rows_tgc(a):
        k = a.shape[-1]
        return jnp.transpose(a.astype(BF16).reshape(sg, gpb, t, gc, k), (0, 2, 1, 3, 4)).reshape(sg, S5_UW, k)

    def cols_tgc(a):
        return lax.dot_general(a, spread, (((2,), (0,)), ((), ())), preferred_element_type=BF16)

    def keep(rows_gl, cols_gl, a):
        return jnp.where((rows_gl[:, None] == cols_gl[None, :])[None], a, jnp.zeros_like(a))

    p_rows = rows_tgc(p_cat)
    own = (gl_tgc[:, None] == jnp.arange(gpb)[None, :])[None, :, :, None]
    p_blk = jnp.where(own, p_rows[:, :, None, :], jnp.zeros((), BF16)).reshape(sg, S5_UW, S5_UW)
    m_blk = keep(gl_tgc, gl_tgc, cols_tgc(rows_tgc(f["m"] + bk["m"])))
    q_blk = keep(gl_gk, gl_tgc, cols_tgc((q_f + q_b).astype(BF16).reshape(sg, S5_UW, S5_CW)))
    return p_blk, m_blk, q_blk, at_re, at_im


def _s5_x_kernel(u_ref, p_ref, x_ref):
    x_ref[...] = _dot(u_ref[0], p_ref[0])


def _s5_x_call(uf, p_blk, layer, *, rt):
    sg, r, uw = uf.shape
    rt = min(rt, r)
    return pl.pallas_call(
        _s5_x_kernel,
        grid=(sg, r // rt),
        in_specs=[pl.BlockSpec((1, rt, uw), lambda si, i: (si, i, 0)),
                  pl.BlockSpec((None, 1, uw, uw), lambda si, i: (layer, si, 0, 0), pipeline_mode=pl.Buffered(1))],
        out_specs=pl.BlockSpec((rt, uw), lambda si, i: (i, si)),
        out_shape=jax.ShapeDtypeStruct((r, sg * uw), F32),
        compiler_params=_params("parallel", "parallel"),
        name="s5_x",
    )(uf, p_blk)


def _s5_scan_kernel(xf_ref, xb_ref, are_ref, aim_ref, h0_ref, hf_ref, hb_ref, hfin_ref, hr_scr, hi_scr, *, nb):
    i = pl.program_id(0)

    @pl.when(i == 0)
    def _():
        hr_scr[...] = h0_ref[:, 0]
        hi_scr[...] = h0_ref[:, 1]

    ar, ai = are_ref[...][None], aim_ref[...][None]
    fwd = lax.broadcasted_iota(jnp.int32, (1, S5_GROUPS, LANES), 2) < S5_STATE

    def step(j, carry):
        hr, hi = carry
        jb = nb - 1 - j
        xr = jnp.where(fwd, xf_ref[:, j, :, 0:LANES], xb_ref[:, jb, :, 0:LANES])
        xi = jnp.where(fwd, xf_ref[:, j, :, LANES:2 * LANES], xb_ref[:, jb, :, LANES:2 * LANES])
        hf_ref[:, j, :, 0:LANES] = hr
        hf_ref[:, j, :, LANES:2 * LANES] = hi
        hb_ref[:, jb, :, 0:LANES] = hr
        hb_ref[:, jb, :, LANES:2 * LANES] = hi
        return ar * hr - ai * hi + xr, ar * hi + ai * hr + xi

    hr, hi = lax.fori_loop(0, nb, step, (hr_scr[...], hi_scr[...]))
    hr_scr[...] = hr
    hi_scr[...] = hi

    @pl.when(i == pl.num_programs(0) - 1)
    def _():
        hfin_ref[:, 0] = hr
        hfin_ref[:, 1] = hi


def _s5_scan_call(x4, at_re, at_im, h0, *, nb):
    b, n, g, w = x4.shape
    nb = min(nb, n)
    steps = n // nb
    blk = lambda rev: pl.BlockSpec((b, nb, g, w), (lambda i: (0, steps - 1 - i, 0, 0)) if rev
                                   else (lambda i: (0, i, 0, 0)))
    st = pl.BlockSpec((b, 2, g, LANES), lambda i: (0, 0, 0, 0))
    return pl.pallas_call(
        functools.partial(_s5_scan_kernel, nb=nb),
        grid=(steps,),
        in_specs=[blk(False), blk(True), _const_spec((g, LANES)), _const_spec((g, LANES)), st],
        out_specs=[blk(False), blk(True), st],
        out_shape=[jax.ShapeDtypeStruct(x4.shape, F32), jax.ShapeDtypeStruct(x4.shape, F32),
                   jax.ShapeDtypeStruct((b, 2, g, LANES), F32)],
        scratch_shapes=[pltpu.VMEM((b, g, LANES), F32), pltpu.VMEM((b, g, LANES), F32)],
        compiler_params=_params("arbitrary"),
        name="s5_scan",
    )(x4, x4, at_re, at_im, h0)


def _s5_y_kernel(u_ref, hf_ref, hb_ref, m_ref, q_ref, y_ref):
    rt = hf_ref.shape[0]
    lane = lax.broadcasted_iota(jnp.int32, (rt, S5_UW), 1)
    h = jnp.where(lane % LANES < S5_STATE, hf_ref[...], hb_ref[...]).astype(BF16)
    y = _dot(u_ref[0], m_ref[0]) + _dot(h, q_ref[0])
    for t in range(S5_T):
        y_ref[pl.ds(t, rt, stride=S5_T), :] = y[:, t * LANES:(t + 1) * LANES]


def _s5_y_call(uf, hf2, hb2, m_blk, q_blk, layer, *, rt):
    sg, r, uw = uf.shape
    rt = min(rt, r)
    wspec = pl.BlockSpec((None, 1, uw, uw), lambda si, i: (layer, si, 0, 0), pipeline_mode=pl.Buffered(1))
    hspec = pl.BlockSpec((rt, uw), lambda si, i: (i, si))
    return pl.pallas_call(
        _s5_y_kernel,
        grid=(sg, r // rt),
        in_specs=[pl.BlockSpec((1, rt, uw), lambda si, i: (si, i, 0)), hspec, hspec, wspec, wspec],
        out_specs=pl.BlockSpec((rt * S5_T, LANES), lambda si, i: (i, si)),
        out_shape=jax.ShapeDtypeStruct((r * S5_T, sg * LANES), F32),
        compiler_params=_params("parallel", "parallel"),
        name="s5_y",
    )(uf, hf2, hb2, m_blk, q_blk)


def _s5_mix(uf, ops, layer, h0, bsz, *, nb, rt):
    p_blk, m_blk, q_blk, at_re, at_im = ops
    r = uf.shape[1]
    n = r // bsz
    x = _s5_x_call(uf, p_blk, layer, rt=rt)
    hf, hb, hfin = _s5_scan_call(x.reshape(bsz, n, S5_GROUPS, S5_CW), at_re[layer], at_im[layer], h0, nb=nb)
    y = _s5_y_call(uf, hf.reshape(r, S5_GROUPS * S5_CW), hb.reshape(r, S5_GROUPS * S5_CW),
                   m_blk, q_blk, layer, rt=rt)
    return y.reshape(bsz, n * S5_T, S5_WIDTH), hfin


def _attn_kernel(q_ref, k_ref, vt_ref, o_ref, m_scr, acc_scr, s_scr, mx_scr, *, tk, n_kv, n_str):
    grp, tq = q_ref.shape[2], q_ref.shape[3] // n_str
    cols = grp * tq
    q = [q_ref[0, 0, :, st * tq:(st + 1) * tq, :].reshape(cols, LANES) for st in range(n_str)]
    m_scr[...] = jnp.full(m_scr.shape, -jnp.inf, F32)
    acc_scr[...] = jnp.zeros(acc_scr.shape, F32)

    def scores(j, st, slot):
        off = pl.multiple_of(j * tk, tk)
        s = _dotg(k_ref[0, pl.ds(off, tk), :], q[st], (1, 1))
        s_scr[st, slot] = s
        mx_scr[st, slot] = jnp.broadcast_to(jnp.max(s, axis=0, keepdims=True), (SUBLANES, cols))

    def softmax_pv(j, st, slot):
        off = pl.multiple_of(j * tk, tk)
        m_prev = m_scr[st]
        m_new = jnp.maximum(m_prev, mx_scr[st, slot])
        alpha = jnp.exp2(m_prev - m_new)
        p = jnp.exp2(s_scr[st, slot] - m_new[0:1]).astype(BF16)
        acc_scr[st] = acc_scr[st] * alpha[0:1] + _dot(vt_ref[0, 0, :, pl.ds(off, tk)], p)
        m_scr[st] = m_new

    def step(j_next, j_cur, slot_next, slot_cur):
        for st in range(n_str):
            if j_next is not None:
                scores(j_next, st, slot_next)
            if j_cur is not None:
                softmax_pv(j_cur, st, slot_cur)

    step(0, None, 0, None)

    def pair(t, carry):
        j = 2 * t
        step(j + 1, j, 1, 0)
        step(j + 2, j + 1, 0, 1)
        return carry

    lax.fori_loop(0, (n_kv - 1) // 2, pair, 0)
    if n_kv % 2 == 0:
        step(n_kv - 1, n_kv - 2, 1, 0)
        step(None, n_kv - 1, None, 1)
    else:
        step(None, n_kv - 1, None, 0)
    for st in range(n_str):
        acc = acc_scr[st]
        out_t = acc / acc[ATT_HEAD_DIM:ATT_HEAD_DIM + 1]
        for j in range(grp):
            o_ref[0, 0, j, st * tq:(st + 1) * tq, :] = out_t[:, j * tq:(j + 1) * tq].T.astype(o_ref.dtype)


def _attn_call(qs, k, vt_ext, *, tq, tk, n_str):
    b, kvh, grp, l, _ = qs.shape
    lk = k.shape[1]
    tq, tk = min(tq, l // n_str), min(tk, lk)
    cols = grp * tq
    blk = pl.BlockSpec((1, 1, grp, n_str * tq, LANES), lambda bi, ki, i: (bi, ki, 0, i, 0))
    return pl.pallas_call(
        functools.partial(_attn_kernel, tk=tk, n_kv=lk // tk, n_str=n_str),
        grid=(b, kvh, l // (n_str * tq)),
        in_specs=[blk,
                  pl.BlockSpec((1, lk, LANES), lambda bi, ki, i: (bi, 0, 0), pipeline_mode=pl.Buffered(1)),
                  pl.BlockSpec((1, 1, LANES, lk), lambda bi, ki, i: (bi, ki, 0, 0), pipeline_mode=pl.Buffered(1))],
        out_specs=blk,
        out_shape=jax.ShapeDtypeStruct(qs.shape, BF16),
        scratch_shapes=[pltpu.VMEM((n_str, SUBLANES, cols), F32), pltpu.VMEM((n_str, LANES, cols), F32),
                        pltpu.VMEM((n_str, 2, tk, cols), F32), pltpu.VMEM((n_str, 2, SUBLANES, cols), F32)],
        compiler_params=_params("parallel", "parallel", "arbitrary"),
        name="attention",
    )(qs, k, vt_ext)


def _kv_layout(ak, av):
    b, lk, _ = av.shape
    vt = jnp.transpose(av.reshape(b, lk, ATT_KV_HEADS, ATT_HEAD_DIM), (0, 2, 3, 1))
    return ak, jnp.concatenate([vt, jnp.ones_like(vt)], axis=2)


def _merge_kernel(h_ref, mod_ref, gn_ref, yg_ref, ys_ref, u_ref, ya_ref, wbg_ref, bbg_ref, wp_ref, wpa_ref,
                  wo_ref, d_ref, gw_ref, gb_ref, o_ref):
    h = h_ref[0]
    d = h.shape[-1]
    n = _rms_mod(h, gn_ref[1:2], mod_ref[0, MIX_MOD:MIX_MOD + 1], mod_ref[0, MIX_MOD + 1:MIX_MOD + 2]).astype(BF16)
    g = jax.nn.sigmoid(_dot(n, wbg_ref[...]) + bbg_ref[...])
    y = jax.nn.gelu(ys_ref[0] + d_ref[...] * u_ref[0])
    y_s5 = y * jax.nn.sigmoid(_dot(y.astype(BF16), gw_ref[...]) + gb_ref[...])
    ya = jnp.concatenate([ya_ref[0, kv, j] for kv in range(ATT_KV_HEADS) for j in range(ATT_GROUP)], axis=-1)
    m = (g[:, 0:d] * _dot(yg_ref[0].astype(BF16), wp_ref[0])
         + g[:, d:2 * d] * _dot(y_s5.astype(BF16), wp_ref[1])
         + g[:, 2 * d:3 * d] * _dot(ya, wpa_ref[...]))
    o_ref[0] = h + mod_ref[0, MIX_MOD + 2:MIX_MOD + 3] * _dot(m.astype(BF16), wo_ref[...])


def _merge_call(h, mods, norm_w, y_gla, y_s5raw, u, y_att, lw, *, layer, mod_row, tm):
    b, l, d = h.shape
    tm = min(tm, l)
    tok = lambda w: pl.BlockSpec((1, tm, w), lambda bi, i: (bi, i, 0))
    bw = y_gla.shape[-1]
    att = pl.BlockSpec((1, ATT_KV_HEADS, ATT_GROUP, tm, LANES), lambda bi, i: (bi, 0, 0, i, 0))
    return pl.pallas_call(
        _merge_kernel,
        grid=(b, l // tm),
        in_specs=[tok(d), _mod_spec(d, mod_row), _norm_spec(d, layer),
                  tok(bw), tok(bw), tok(bw), att,
                  _const_spec((d, 3 * d)), _const_spec((1, 3 * d)), _const_spec((2, bw, d)),
                  _const_spec((ATT_HEADS * LANES, d)), _const_spec((d, d)),
                  _const_spec((1, bw)), _const_spec((bw, bw)), _const_spec((1, bw))],
        out_specs=tok(d),
        out_shape=jax.ShapeDtypeStruct((b, l, d), F32),
        compiler_params=_params("parallel", "parallel"),
        name="merge",
    )(h, mods, norm_w, y_gla, y_s5raw, u, y_att, lw["w_bgate"], lw["b_bgate"], lw["w_bproj"],
      lw["w_aproj"], lw["w_out"], lw["s5_d"], lw["glu_w"], lw["glu_b"])


def _rope_tables(n_tokens):
    rows = n_tokens // GRID_W
    row = jnp.repeat(jnp.arange(rows, dtype=F32), GRID_W)
    col = jnp.tile(jnp.arange(GRID_W, dtype=F32), rows)
    inv = ROPE_THETA ** (-jnp.arange(ROPE_PAIRS, dtype=F32) / ROPE_PAIRS)
    ang_r, ang_c = row[:, None] * inv, col[:, None] * inv
    zero = jnp.zeros_like(ang_r)
    cos = jnp.concatenate([jnp.cos(ang_r), jnp.cos(ang_r), jnp.cos(ang_c), jnp.cos(ang_c)], axis=-1)
    sin_a = jnp.concatenate([-jnp.sin(ang_r), zero, -jnp.sin(ang_c), zero], axis=-1)
    sin_b = jnp.concatenate([zero, jnp.sin(ang_r), zero, jnp.sin(ang_c)], axis=-1)
    tile = lambda t: jnp.tile(t, (1, ATT_HEADS))
    return tile(cos), tile(sin_a), tile(sin_b)


def _block_diag_ones(width, seg):
    idx = np.arange(width) // seg
    return jnp.asarray((idx[:, None] == idx[None, :]).astype(np.float32), BF16)


def _layer_weights(i, w_in, gla_gate_w, gla_gate_b, attn_q_norm_w, attn_k_norm_w, w_branch_gate,
                   b_branch_gate, w_branch_proj, w_out, s5_d, s5_glu_w, s5_glu_b):
    offs = np.concatenate([[0], np.cumsum(IN_WIDTHS)])
    col = lambda k: w_in[i][:, offs[k]:offs[k + 1]]
    w_main = jnp.concatenate([col(0), col(1), col(2), col(5), col(6), col(7), col(8), col(9)], axis=1)
    d = w_in.shape[1]
    rk = GLA_GATE_RANK
    w_gz = jnp.zeros((d, LANES), F32).at[:, 0:rk].set(col(3)).at[:, rk:2 * rk].set(col(4))
    gw = jnp.zeros((LANES, 2 * GLA_QK), F32)
    gw = gw.at[0:rk, 0:GLA_QK].set(gla_gate_w[i, 0]).at[rk:2 * rk, GLA_QK:].set(gla_gate_w[i, 1])
    wa = w_branch_proj[i, 2].reshape(ATT_HEADS, ATT_HEAD_DIM, d)
    w_aproj = jnp.concatenate([wa, jnp.zeros_like(wa)], axis=1).reshape(ATT_HEADS * LANES, d)
    return dict(
        w_aproj=w_aproj.astype(BF16),
        w_main=w_main.astype(BF16), w_gz=w_gz.astype(BF16), gate_w_bd=gw.astype(BF16),
        gate_b=gla_gate_b[i].reshape(1, 2 * GLA_QK),
        q_norm_w=(jnp.tile(attn_q_norm_w[i], ATT_HEADS) * (ATT_HEAD_DIM ** -0.5 * LOG2_E)).reshape(1, ATT_Q),
        k_norm_w=jnp.tile(attn_k_norm_w[i], ATT_KV_HEADS).reshape(1, ATT_KV),
        bd_q=_block_diag_ones(ATT_Q, ATT_HEAD_DIM), bd_k=_block_diag_ones(ATT_KV, ATT_HEAD_DIM),
        w_bgate=w_branch_gate[i].astype(BF16), b_bgate=b_branch_gate[i].reshape(1, -1),
        w_bproj=w_branch_proj[i, 0:2].astype(BF16), w_out=w_out[i].astype(BF16),
        s5_d=s5_d[i].reshape(1, -1), glu_w=s5_glu_w[i].astype(BF16), glu_b=s5_glu_b[i].reshape(1, -1),
    )


def kernel(x, c, ctx, c_ctx, w_ada, b_ada, norm_w, w_ffn_gate, w_ffn_up, w_ffn_down, w_in, gla_gate_w,
           gla_gate_b, gla_norm_w, s5_a_re, s5_a_im, s5_log_dt, s5_b_re, s5_b_im, s5_c_re, s5_c_im, s5_d,
           s5_glu_w, s5_glu_b, attn_q_norm_w, attn_k_norm_w, w_branch_gate, b_branch_gate, w_branch_proj,
           w_out, final_norm_w):
    bsz, seq, d = x.shape
    depth = w_ada.shape[0]
    tables = _rope_tables(seq)
    ctx_tables = tuple(t[0:ctx.shape[1]] for t in tables)
    cond = jnp.zeros((SUBLANES, d), F32).at[0:bsz].set(c).at[bsz].set(c_ctx)
    tm_ffn, tm_in, tm_merge, tb_gla, nb_s5, rt_s5, tq, tk = (
        TILES[k] for k in ("ffn", "inproj", "merge", "gla", "s5_scan", "s5_rows", "att_q", "att_k"))

    ffn_w = (w_ffn_gate.astype(BF16), w_ffn_up.astype(BF16), w_ffn_down.astype(BF16))
    s5_ops = jax.vmap(_s5_operators)(s5_a_re, s5_a_im, s5_log_dt, s5_b_re, s5_b_im, s5_c_re, s5_c_im)
    h_lat, h_ctx = x, ctx
    for i in range(depth):
        last = i == depth - 1
        mods = _ada_call(cond, w_ada, b_ada, i).reshape(SUBLANES, N_MOD, d)
        lat, cxt = None, bsz
        lw = _layer_weights(i, w_in, gla_gate_w, gla_gate_b, attn_q_norm_w, attn_k_norm_w, w_branch_gate,
                            b_branch_gate, w_branch_proj, w_out, s5_d, s5_glu_w, s5_glu_b)

        def ffn(h, mod_row, j, fin=False):
            return _ffn_call(h, mods, norm_w, *ffn_w, final_norm_w, layer=i, sub=j, mod_row=mod_row,
                             final_norm=fin, tm=tm_ffn)

        h_lat = ffn(h_lat, lat, 0)
        h_ctx = ffn(h_ctx, cxt, 0)

        qk_c, v_c, r_c, lg_c, u_c, uf_c, aq_c, ak_c, av_c = _inproj_call(
            h_ctx, mods, norm_w, lw, ctx_tables, layer=i, mod_row=cxt, rope=False, tm=tm_in)
        qk_l, v_l, r_l, lg_l, u_l, uf_l, aq_l, ak_l, av_l = _inproj_call(
            h_lat, mods, norm_w, lw, tables, layer=i, mod_row=lat, rope=True, tm=tm_in)
        s0 = jnp.zeros((bsz, GLA_QK, GLA_DV), F32)
        yg_c, sf_c, sb_c = _gla_bidir(qk_c, v_c, lg_c, r_c, gla_norm_w[i], s0, s0, tb=tb_gla)
        yg_l, _, _ = _gla_bidir(qk_l, v_l, lg_l, r_l, gla_norm_w[i], sf_c, sb_c, tb=tb_gla)
        h0 = jnp.zeros((bsz, 2, S5_GROUPS, LANES), F32)
        ys_c, hfin_c = _s5_mix(uf_c, s5_ops, i, h0, bsz, nb=nb_s5, rt=rt_s5)
        ys_l, _ = _s5_mix(uf_l, s5_ops, i, hfin_c, bsz, nb=nb_s5, rt=rt_s5)
        kt, v_ext = _kv_layout(jnp.concatenate([ak_c, ak_l], axis=1), jnp.concatenate([av_c, av_l], axis=1))
        ya_l = _attn_call(aq_l, kt, v_ext, tq=tq, tk=tk, n_str=TILES["att_streams"])
        h_lat = _merge_call(h_lat, mods, norm_w, yg_l, ys_l, u_l, ya_l, lw, layer=i, mod_row=lat, tm=tm_merge)
        h_lat = ffn(h_lat, lat, 1, last)
        if not last:
            kt_c, v_ext_c = _kv_layout(ak_c, av_c)
            ya_c = _attn_call(aq_c, kt_c, v_ext_c, tq=tq, tk=tk, n_str=TILES["att_streams"])
            h_ctx = _merge_call(h_ctx, mods, norm_w, yg_c, ys_c, u_c, ya_c, lw, layer=i, mod_row=cxt, tm=tm_merge)
            h_ctx = ffn(h_ctx, cxt, 1)
    return h_lat
```

```python
import functools

import jax
import jax.numpy as jnp
import numpy as np
from jax import lax
from jax.experimental import pallas as pl
from jax.experimental.pallas import tpu as pltpu

F32 = jnp.float32
BF16 = jnp.bfloat16

N_MOD = 9
MIX_MOD = 3
MACARON_WEIGHT = 0.5
NORM_EPS = 1e-6
GRID_W = 64
GLA_HEADS = 4
GLA_DK = 64
GLA_DV = 128
GLA_QK = GLA_HEADS * GLA_DK
GLA_V = GLA_HEADS * GLA_DV
GLA_GATE_RANK = 16
GLA_TAU = 16.0
GLA_CHUNK = 64
S5_WIDTH = 512
S5_GROUP = 16
S5_GROUPS = S5_WIDTH // S5_GROUP
S5_STATE = 64
ATT_HEADS = 8
ATT_KV_HEADS = 2
ATT_GROUP = ATT_HEADS // ATT_KV_HEADS
ATT_HEAD_DIM = 64
ATT_Q = ATT_HEADS * ATT_HEAD_DIM
ATT_KV = ATT_KV_HEADS * ATT_HEAD_DIM
ROPE_PAIRS = ATT_HEAD_DIM // 4
ROPE_THETA = 10000.0
IN_WIDTHS = (GLA_QK, GLA_QK, GLA_V, GLA_GATE_RANK, GLA_GATE_RANK, GLA_V, S5_WIDTH, ATT_Q, ATT_KV, ATT_KV)

LANES = 128
SUBLANES = 8
MXU_DIM = 256
VMEM_LIMIT = 56 * 1024 * 1024

LOG2_E = 1.4426950408889634

S5_T = MXU_DIM // S5_GROUP
S5_CW = S5_T * S5_GROUP
S5_GPB = LANES // S5_GROUP
S5_SG = S5_GROUPS // S5_GPB
S5_UW = S5_T * LANES

TILES = dict(ffn=512, inproj=512, merge=512, gla=512, s5_scan=32, s5_rows=256, att_q=128, att_k=1280,
             att_streams=4)


def _params(*sem):
    return pltpu.CompilerParams(dimension_semantics=sem, vmem_limit_bytes=VMEM_LIMIT)


def _const_spec(shape):
    nd = len(shape)
    return pl.BlockSpec(shape, lambda *_: (0,) * nd, pipeline_mode=pl.Buffered(1))


def _mod_spec(d, mod_row):
    if mod_row is None:
        return pl.BlockSpec((1, N_MOD, d), lambda bi, i: (bi, 0, 0))
    return pl.BlockSpec((1, N_MOD, d), lambda bi, i: (mod_row, 0, 0))


def _norm_spec(d, layer):
    return pl.BlockSpec((None, 3, d), lambda bi, i: (layer, 0, 0), pipeline_mode=pl.Buffered(1))


def _dotg(a, b, contract):
    return lax.dot_general(a, b, (((contract[0],), (contract[1],)), ((), ())),
                           precision=lax.Precision.DEFAULT, preferred_element_type=F32)


def _dot(a, b):
    return _dotg(a, b, (1, 0))


def _dot_hi_lo(x, w):
    hi = x.astype(BF16)
    lo = (x - hi.astype(F32)).astype(BF16)
    return _dot(hi, w) + _dot(lo, w)


def _rms_mod(h, g_norm, shift, scale):
    ms = jnp.mean(h * h, axis=-1, keepdims=True)
    y = h * lax.rsqrt(ms + NORM_EPS) * g_norm
    return y * (1.0 + scale) + shift


def _log_sigmoid(x):
    return jnp.minimum(x, 0.0) - jnp.log(1.0 + jnp.exp(-jnp.abs(x)))


def _ada_kernel(c_ref, w_ref, b_ref, o_ref):
    c = c_ref[...]
    sc = (c * jax.nn.sigmoid(c)).astype(BF16)
    o_ref[...] = _dot(sc, w_ref[...].astype(BF16)) + b_ref[...]


def _ada_call(cond, w_all, b_all, layer):
    rows, d = cond.shape
    depth, _, n = w_all.shape
    tn = 9 * LANES
    return pl.pallas_call(
        _ada_kernel,
        grid=(n // tn,),
        in_specs=[pl.BlockSpec((rows, d), lambda j: (0, 0)),
                  pl.BlockSpec((None, d, tn), lambda j: (layer, 0, j)),
                  pl.BlockSpec((None, 1, tn), lambda j: (layer, 0, j))],
        out_specs=pl.BlockSpec((rows, tn), lambda j: (0, j)),
        out_shape=jax.ShapeDtypeStruct((rows, n), F32),
        compiler_params=_params("arbitrary"),
        name="adaln",
    )(cond, w_all, b_all.reshape(depth, 1, n))


def _ffn_kernel(h_ref, mod_ref, gn_ref, wg_ref, wu_ref, wd_ref, fn_ref, o_ref, *, mod_base, final_norm):
    h = h_ref[0]
    shift, scale, gate = (mod_ref[0, mod_base + k:mod_base + k + 1] for k in range(3))
    norm_row = mod_base // 3
    n = _rms_mod(h, gn_ref[norm_row:norm_row + 1], shift, scale).astype(BF16)
    g = _dot(n, wg_ref[...])
    u = _dot(n, wu_ref[...])
    a = (g * jax.nn.sigmoid(g) * u).astype(BF16)
    y = h + (MACARON_WEIGHT * gate) * _dot(a, wd_ref[...])
    if final_norm:
        ms = jnp.mean(y * y, axis=-1, keepdims=True)
        y = y * lax.rsqrt(ms + NORM_EPS) * fn_ref[...]
    o_ref[0] = y


def _ffn_call(h, mods, norm_w, wg, wu, wd, final_w, *, layer, sub, mod_row, final_norm, tm):
    b, l, d = h.shape
    f = wg.shape[-1]
    tm = min(tm, l)
    wspec = lambda r, c: pl.BlockSpec((None, None, r, c), lambda bi, i: (layer, sub, 0, 0),
                                      pipeline_mode=pl.Buffered(1))
    return pl.pallas_call(
        functools.partial(_ffn_kernel, mod_base=6 * sub, final_norm=final_norm),
        grid=(b, l // tm),
        in_specs=[pl.BlockSpec((1, tm, d), lambda bi, i: (bi, i, 0)),
                  _mod_spec(d, mod_row), _norm_spec(d, layer),
                  wspec(d, f), wspec(d, f), wspec(f, d), _const_spec((1, d))],
        out_specs=pl.BlockSpec((1, tm, d), lambda bi, i: (bi, i, 0)),
        out_shape=jax.ShapeDtypeStruct((b, l, d), F32),
        compiler_params=_params("parallel", "parallel"),
        name="ffn",
    )(h, mods, norm_w, wg, wu, wd, final_w.reshape(1, d))


def _inproj_kernel(h_ref, mod_ref, gn_ref, wm_ref, wgz_ref, gw_ref, gb_ref, qw_ref, kw_ref,
                   bdq_ref, bdk_ref, cos_ref, sa_ref, sb_ref,
                   qk_ref, v_ref, r_ref, lg_ref, u_ref, uf_ref, aq_ref, ak_ref, av_ref, us_scr, *, rope):
    h = h_ref[0]
    tm = h.shape[0]
    n = _rms_mod(h, gn_ref[1:2], mod_ref[0, MIX_MOD:MIX_MOD + 1], mod_ref[0, MIX_MOD + 1:MIX_MOD + 2]).astype(BF16)
    z = _dot(n, wm_ref[...])
    o = 0
    q = z[:, o:o + GLA_QK] * (GLA_DK ** -0.5); o += GLA_QK
    k = z[:, o:o + GLA_QK]; o += GLA_QK
    qk_ref[0, :, 0:GLA_QK] = q
    qk_ref[0, :, GLA_QK:2 * GLA_QK] = k
    v_ref[0] = z[:, o:o + GLA_V]; o += GLA_V
    r_ref[0] = z[:, o:o + GLA_V]; o += GLA_V
    u_ref[0] = z[:, o:o + S5_WIDTH]; o += S5_WIDTH
    for sg in range(S5_SG):
        us_scr[sg] = z[:, o - S5_WIDTH + sg * LANES:o - S5_WIDTH + (sg + 1) * LANES]
        for t in range(S5_T):
            piece = us_scr[sg, pl.ds(t, tm // S5_T, stride=S5_T), :]
            uf_ref[sg, :, t * LANES:(t + 1) * LANES] = piece.astype(BF16)
    xq = z[:, o:o + ATT_Q]; o += ATT_Q
    xk = z[:, o:o + ATT_KV]; o += ATT_KV
    av_ref[0] = z[:, o:o + ATT_KV].astype(BF16)

    zg = _dot(n, wgz_ref[...]).astype(BF16)
    lg_ref[0] = _log_sigmoid(_dot(zg, gw_ref[...]) + gb_ref[...]) * (1.0 / GLA_TAU)

    def head_norm(x, bd_ref, w_ref):
        ss = _dot_hi_lo(x * x, bd_ref[...])
        return x * lax.rsqrt(ss * (1.0 / ATT_HEAD_DIM) + NORM_EPS) * w_ref[...]

    def rotary(x, width):
        if not rope:
            return x
        c, sa, sb = cos_ref[:, 0:width], sa_ref[:, 0:width], sb_ref[:, 0:width]
        return (x * c + pltpu.roll(x, width - ROPE_PAIRS, 1) * sa + pltpu.roll(x, ROPE_PAIRS, 1) * sb)

    ak_ref[0] = rotary(head_norm(xk, bdk_ref, kw_ref), ATT_KV).astype(BF16)
    aq = rotary(head_norm(xq, bdq_ref, qw_ref), ATT_Q)
    lane = lax.broadcasted_iota(jnp.int32, (tm, LANES), 1)
    for kv in range(ATT_KV_HEADS):
        keep = (lane >= kv * ATT_HEAD_DIM) & (lane < (kv + 1) * ATT_HEAD_DIM)
        for j in range(ATT_GROUP):
            hd = kv * ATT_GROUP + j
            blk = aq[:, (hd // 2) * LANES:(hd // 2 + 1) * LANES]
            if hd % 2 != kv:
                blk = pltpu.roll(blk, ATT_HEAD_DIM, 1)
            aq_ref[0, kv, j] = jnp.where(keep, blk, 0.0).astype(BF16)


def _inproj_call(h, mods, norm_w, lw, tables, *, layer, mod_row, rope, tm):
    b, l, d = h.shape
    tm = min(tm, l)
    tok = lambda w: pl.BlockSpec((1, tm, w), lambda bi, i: (bi, i, 0))
    tab = pl.BlockSpec((tm, ATT_Q), lambda bi, i: (i, 0))
    wm = lw["w_main"]
    nt = l // tm
    tok_out = lambda w, dt: (tok(w), jax.ShapeDtypeStruct((b, l, w), dt))
    outs = [tok_out(2 * GLA_QK, F32), tok_out(GLA_V, F32), tok_out(GLA_V, F32), tok_out(2 * GLA_QK, F32),
            tok_out(S5_WIDTH, F32),
            (pl.BlockSpec((S5_SG, tm // S5_T, S5_UW), lambda bi, i: (0, bi * nt + i, 0)),
             jax.ShapeDtypeStruct((S5_SG, b * l // S5_T, S5_UW), BF16)),
            (pl.BlockSpec((1, ATT_KV_HEADS, ATT_GROUP, tm, LANES), lambda bi, i: (bi, 0, 0, i, 0)),
             jax.ShapeDtypeStruct((b, ATT_KV_HEADS, ATT_GROUP, l, LANES), BF16)),
            tok_out(ATT_KV, BF16), tok_out(ATT_KV, BF16)]
    return pl.pallas_call(
        functools.partial(_inproj_kernel, rope=rope),
        grid=(b, nt),
        in_specs=[tok(d), _mod_spec(d, mod_row), _norm_spec(d, layer),
                  _const_spec(wm.shape), _const_spec((d, LANES)), _const_spec((LANES, 2 * GLA_QK)),
                  _const_spec((1, 2 * GLA_QK)), _const_spec((1, ATT_Q)), _const_spec((1, ATT_KV)),
                  _const_spec((ATT_Q, ATT_Q)), _const_spec((ATT_KV, ATT_KV)), tab, tab, tab],
        out_specs=[s for s, _ in outs],
        out_shape=[t for _, t in outs],
        scratch_shapes=[pltpu.VMEM((S5_SG, tm, LANES), F32)],
        compiler_params=_params("parallel", "parallel"),
        name="inproj",
    )(h, mods, norm_w, wm, lw["w_gz"], lw["gate_w_bd"], lw["gate_b"], lw["q_norm_w"],
      lw["k_norm_w"], lw["bd_q"], lw["bd_k"], *tables)


def _gla_kernel(qk_ref, v_ref, lg_ref, s0_ref, tri_ref, of_ref, r_ref, nw_ref, o_ref, sfin_ref, s_scr,
                *, reverse, readout, n_chunks, n_batch):
    i = pl.program_id(0)

    @pl.when(i == 0)
    def _():
        s_scr[...] = s0_ref[...]

    tri = tri_ref[...]
    c, nh = GLA_CHUNK, GLA_HEADS
    row = lax.broadcasted_iota(jnp.int32, (nh * c, c), 0) % c
    col = lax.broadcasted_iota(jnp.int32, (nh * c, c), 1)
    keep = (col >= row) if reverse else (col <= row)
    head_of_lane = lax.broadcasted_iota(jnp.int32, (c, GLA_QK), 1) // GLA_DK

    def chunk(bi, ci):
        rows = slice(ci * c, (ci + 1) * c)
        g = lg_ref[bi, rows, :]
        g_hi = g.astype(BF16)
        b = _dot(tri, g_hi) + _dot(tri, (g - g_hi.astype(F32)).astype(BF16))
        b_last = b[0:1, :] if reverse else b[c - 1:c, :]
        q_in = qk_ref[bi, rows, 0:GLA_QK] * jnp.exp(b)
        k = qk_ref[bi, rows, GLA_QK:2 * GLA_QK]
        k_in = (k * jnp.exp(-b)).astype(BF16)
        k_out = (k * jnp.exp(b_last - b)).astype(BF16)
        dec = jnp.exp(jnp.broadcast_to(b_last, (LANES, GLA_QK)).T)
        v = v_ref[bi, rows, :].astype(BF16)
        s_old = s_scr[bi]
        q_st = jnp.concatenate([jnp.where(head_of_lane == hd, q_in, 0.0) for hd in range(nh)],
                               axis=0).astype(BF16)
        att = _dotg(q_st, k_in, (1, 1))
        att = jnp.where(keep, att, 0.0).astype(BF16)
        o_inter = _dot(q_st, s_old.astype(BF16))
        o_intra, ds = [], []
        for pr in range(nh // 2):
            vp = v[:, 2 * pr * GLA_DV:(2 * pr + 2) * GLA_DV]
            o_intra.append(_dot(att[2 * pr * c:(2 * pr + 2) * c], vp))
            ds.append(_dotg(k_out[:, pr * LANES:(pr + 1) * LANES], vp, (0, 0)))
        for hd in range(nh):
            tok = slice(hd * c, (hd + 1) * c)
            vcols = slice(hd * GLA_DV, (hd + 1) * GLA_DV)
            e = hd % 2
            o_h = o_intra[hd // 2][e * c:(e + 1) * c, e * GLA_DV:(e + 1) * GLA_DV] + o_inter[tok]
            if readout:
                o_h = o_h + of_ref[bi, rows, vcols]
                ms = jnp.mean(o_h * o_h, axis=-1, keepdims=True)
                y = o_h * lax.rsqrt(ms + NORM_EPS) * nw_ref[...]
                rr = r_ref[bi, rows, vcols]
                o_h = y * (rr * jax.nn.sigmoid(rr))
            o_ref[bi, rows, vcols] = o_h
        d_state = jnp.concatenate([ds[hd // 2][(hd % 2) * GLA_DK:(hd % 2 + 1) * GLA_DK,
                                               (hd % 2) * GLA_DV:(hd % 2 + 1) * GLA_DV]
                                   for hd in range(nh)], axis=0)
        s_scr[bi] = dec * s_old + d_state

    for ci in (range(n_chunks - 1, -1, -1) if reverse else range(n_chunks)):
        for bi in range(n_batch):
            chunk(bi, ci)

    @pl.when(i == pl.num_programs(0) - 1)
    def _():
        sfin_ref[...] = s_scr[...]


def _gla_call(qk, v, lg, s0, o_f, r, norm_w, *, reverse, readout, tb):
    b, l, _ = v.shape
    tb = min(tb, l)
    nb = l // tb
    idx = (lambda i: (0, nb - 1 - i, 0)) if reverse else (lambda i: (0, i, 0))
    tok = lambda w: pl.BlockSpec((b, tb, w), idx)
    lg_off = 1 if reverse else 0
    lg_spec = pl.BlockSpec((b, tb, GLA_QK), (lambda i: (0, nb - 1 - i, lg_off)) if reverse
                           else (lambda i: (0, i, lg_off)))
    state = pl.BlockSpec((b, GLA_QK, GLA_DV), lambda i: (0, 0, 0))
    ones = np.triu(np.ones((GLA_CHUNK, GLA_CHUNK), np.float32)) if reverse else \
        np.tril(np.ones((GLA_CHUNK, GLA_CHUNK), np.float32))
    return pl.pallas_call(
        functools.partial(_gla_kernel, reverse=reverse, readout=readout, n_chunks=tb // GLA_CHUNK, n_batch=b),
        grid=(nb,),
        in_specs=[tok(2 * GLA_QK), tok(GLA_V), lg_spec, state, _const_spec((GLA_CHUNK, GLA_CHUNK)),
                  tok(GLA_V), tok(GLA_V), _const_spec((1, GLA_DV))],
        out_specs=[tok(GLA_V), state],
        out_shape=[jax.ShapeDtypeStruct((b, l, GLA_V), F32), jax.ShapeDtypeStruct((b, GLA_QK, GLA_DV), F32)],
        scratch_shapes=[pltpu.VMEM((b, GLA_QK, GLA_DV), F32)],
        compiler_params=_params("arbitrary"),
        name="gla_bwd" if reverse else "gla_fwd",
    )(qk, v, lg, s0, jnp.asarray(ones, BF16), o_f, r, norm_w.reshape(1, GLA_DV))


def _gla_bidir(qk, v, lg, r, norm_w, s0_f, s0_b, *, tb):
    o_f, s_f = _gla_call(qk, v, lg, s0_f, v, r, norm_w, reverse=False, readout=False, tb=tb)
    y, s_b = _gla_call(qk, v, lg, s0_b, o_f, r, norm_w, reverse=True, readout=True, tb=tb)
    return y, s_f, s_b


def _s5_operators(a_re, a_im, log_dt, b_re, b_im, c_re, c_im):
    hp = lax.Precision.HIGHEST
    t = S5_T
    cr, ci_ = c_re.astype(F32), c_im.astype(F32)
    ks = jnp.arange(t + 1, dtype=F32)[:, None, None]
    per_dir = []
    for d in range(2):
        dt = jnp.exp(log_dt[d].astype(F32))[:, None]
        ar, ai = a_re[d].astype(F32), a_im[d].astype(F32)
        mag = jnp.exp(ks * dt * ar)
        pw_re, pw_im = mag * jnp.cos(ks * dt * ai), mag * jnp.sin(ks * dt * ai)
        den = ar * ar + ai * ai
        xr, xi = pw_re[1] - 1.0, pw_im[1]
        coef_re = (xr * ar + xi * ai) / den
        coef_im = (xi * ar - xr * ai) / den
        br, bi = b_re.astype(F32), b_im.astype(F32)
        bb_re = coef_re[..., None] * br - coef_im[..., None] * bi
        bb_im = coef_re[..., None] * bi + coef_im[..., None] * br
        ca_re = cr[None] * pw_re[:, :, None, :] - ci_[None] * pw_im[:, :, None, :]
        ca_im = cr[None] * pw_im[:, :, None, :] + ci_[None] * pw_re[:, :, None, :]
        kk = (jnp.einsum("kgop,gpi->kgoi", ca_re, bb_re, precision=hp)
              - jnp.einsum("kgop,gpi->kgoi", ca_im, bb_im, precision=hp))
        s_idx = jnp.arange(t)[:, None]
        t_idx = jnp.arange(t)[None, :]
        lag = (t_idx - s_idx) if d == 0 else (s_idx - t_idx)
        toe = jnp.where((lag >= 0)[:, :, None, None, None], kk[jnp.clip(lag, 0, t)], 0.0)
        m = jnp.transpose(toe, (2, 0, 4, 1, 3)).reshape(S5_GROUPS, S5_CW, S5_CW)
        p_pow = (t - 1 - jnp.arange(t)) if d == 0 else jnp.arange(t)
        pr, pi = pw_re[p_pow], pw_im[p_pow]
        p_re = pr[:, :, :, None] * bb_re[None] - pi[:, :, :, None] * bb_im[None]
        p_im = pr[:, :, :, None] * bb_im[None] + pi[:, :, :, None] * bb_re[None]
        flat_p = lambda x: jnp.transpose(x, (1, 0, 3, 2)).reshape(S5_GROUPS, S5_CW, S5_STATE)
        q_pow = (jnp.arange(t) + 1) if d == 0 else (t - jnp.arange(t))
        q_re, q_im = ca_re[q_pow], -ca_im[q_pow]
        flat_q = lambda x: jnp.transpose(x, (1, 3, 0, 2)).reshape(S5_GROUPS, S5_STATE, S5_CW)
        per_dir.append(dict(m=m, p_re=flat_p(p_re), p_im=flat_p(p_im), q_re=flat_q(q_re), q_im=flat_q(q_im),
                            at_re=pw_re[t], at_im=pw_im[t]))
    f, bk = per_dir
    zq = jnp.zeros_like(f["q_re"])
    p_cat = jnp.concatenate([f["p_re"], bk["p_re"], f["p_im"], bk["p_im"]], axis=-1)
    q_f = jnp.concatenate([f["q_re"], zq, f["q_im"], zq], axis=1)
    q_b = jnp.concatenate([zq, bk["q_re"], zq, bk["q_im"]], axis=1)
    at_re = jnp.concatenate([f["at_re"], bk["at_re"]], axis=-1)
    at_im = jnp.concatenate([f["at_im"], bk["at_im"]], axis=-1)

    sg, gpb, t, gc = S5_SG, S5_GPB, S5_T, S5_GROUP
    idx = jnp.arange(S5_UW)
    gl_tgc = (idx // gc) % gpb
    gl_gk = idx // S5_CW
    src = np.arange(S5_UW)
    spread = np.zeros((S5_CW, S5_UW), np.float32)
    spread[(src // LANES) * gc + src % gc, src] = 1.0
    spread = jnp.asarray(spread, BF16)

    def rows_tgc(a):
        k = a.shape[-1]
        return jnp.transpose(a.astype(BF16).reshape(sg, gpb, t, gc, k), (0, 2, 1, 3, 4)).reshape(sg, S5_UW, k)

    def cols_tgc(a):
        return lax.dot_general(a, spread, (((2,), (0,)), ((), ())), preferred_element_type=BF16)

    def keep(rows_gl, cols_gl, a):
        return jnp.where((rows_gl[:, None] == cols_gl[None, :])[None], a, jnp.zeros_like(a))

    p_rows = rows_tgc(p_cat)
    own = (gl_tgc[:, None] == jnp.arange(gpb)[None, :])[None, :, :, None]
    p_blk = jnp.where(own, p_rows[:, :, None, :], jnp.zeros((), BF16)).reshape(sg, S5_UW, S5_UW)
    m_blk = keep(gl_tgc, gl_tgc, cols_tgc(rows_tgc(f["m"] + bk["m"])))
    q_blk = keep(gl_gk, gl_tgc, cols_tgc((q_f + q_b).astype(BF16).reshape(sg, S5_UW, S5_CW)))
    return p_blk, m_blk, q_blk, at_re, at_im


def _s5_x_kernel(u_ref, p_ref, x_ref):
    x_ref[...] = _dot(u_ref[0], p_ref[0])


def _s5_x_call(uf, p_blk, layer, *, rt):
    sg, r, uw = uf.shape
    rt = min(rt, r)
    return pl.pallas_call(
        _s5_x_kernel,
        grid=(sg, r // rt),
        in_specs=[pl.BlockSpec((1, rt, uw), lambda si, i: (si, i, 0)),
                  pl.BlockSpec((None, 1, uw, uw), lambda si, i: (layer, si, 0, 0), pipeline_mode=pl.Buffered(1))],
        out_specs=pl.BlockSpec((rt, uw), lambda si, i: (i, si)),
        out_shape=jax.ShapeDtypeStruct((r, sg * uw), F32),
        compiler_params=_params("parallel", "parallel"),
        name="s5_x",
    )(uf, p_blk)


def _s5_scan_kernel(xf_ref, xb_ref, are_ref, aim_ref, h0_ref, hf_ref, hb_ref, hfin_ref, hr_scr, hi_scr, *, nb):
    i = pl.program_id(0)

    @pl.when(i == 0)
    def _():
        hr_scr[...] = h0_ref[:, 0]
        hi_scr[...] = h0_ref[:, 1]

    ar, ai = are_ref[...][None], aim_ref[...][None]
    fwd = lax.broadcasted_iota(jnp.int32, (1, S5_GROUPS, LANES), 2) < S5_STATE

    def step(j, carry):
        hr, hi = carry
        jb = nb - 1 - j
        xr = jnp.where(fwd, xf_ref[:, j, :, 0:LANES], xb_ref[:, jb, :, 0:LANES])
        xi = jnp.where(fwd, xf_ref[:, j, :, LANES:2 * LANES], xb_ref[:, jb, :, LANES:2 * LANES])
        hf_ref[:, j, :, 0:LANES] = hr
        hf_ref[:, j, :, LANES:2 * LANES] = hi
        hb_ref[:, jb, :, 0:LANES] = hr
        hb_ref[:, jb, :, LANES:2 * LANES] = hi
        return ar * hr - ai * hi + xr, ar * hi + ai * hr + xi

    hr, hi = lax.fori_loop(0, nb, step, (hr_scr[...], hi_scr[...]))
    hr_scr[...] = hr
    hi_scr[...] = hi

    @pl.when(i == pl.num_programs(0) - 1)
    def _():
        hfin_ref[:, 0] = hr
        hfin_ref[:, 1] = hi


def _s5_scan_call(x4, at_re, at_im, h0, *, nb):
    b, n, g, w = x4.shape
    nb = min(nb, n)
    steps = n // nb
    blk = lambda rev: pl.BlockSpec((b, nb, g, w), (lambda i: (0, steps - 1 - i, 0, 0)) if rev
                                   else (lambda i: (0, i, 0, 0)))
    st = pl.BlockSpec((b, 2, g, LANES), lambda i: (0, 0, 0, 0))
    return pl.pallas_call(
        functools.partial(_s5_scan_kernel, nb=nb),
        grid=(steps,),
        in_specs=[blk(False), blk(True), _const_spec((g, LANES)), _const_spec((g, LANES)), st],
        out_specs=[blk(False), blk(True), st],
        out_shape=[jax.ShapeDtypeStruct(x4.shape, F32), jax.ShapeDtypeStruct(x4.shape, F32),
                   jax.ShapeDtypeStruct((b, 2, g, LANES), F32)],
        scratch_shapes=[pltpu.VMEM((b, g, LANES), F32), pltpu.VMEM((b, g, LANES), F32)],
        compiler_params=_params("arbitrary"),
        name="s5_scan",
    )(x4, x4, at_re, at_im, h0)


def _s5_y_kernel(u_ref, hf_ref, hb_ref, m_ref, q_ref, y_ref):
    rt = hf_ref.shape[0]
    lane = lax.broadcasted_iota(jnp.int32, (rt, S5_UW), 1)
    h = jnp.where(lane % LANES < S5_STATE, hf_ref[...], hb_ref[...]).astype(BF16)
    y = _dot(u_ref[0], m_ref[0]) + _dot(h, q_ref[0])
    for t in range(S5_T):
        y_ref[pl.ds(t, rt, stride=S5_T), :] = y[:, t * LANES:(t + 1) * LANES]


def _s5_y_call(uf, hf2, hb2, m_blk, q_blk, layer, *, rt):
    sg, r, uw = uf.shape
    rt = min(rt, r)
    wspec = pl.BlockSpec((None, 1, uw, uw), lambda si, i: (layer, si, 0, 0), pipeline_mode=pl.Buffered(1))
    hspec = pl.BlockSpec((rt, uw), lambda si, i: (i, si))
    return pl.pallas_call(
        _s5_y_kernel,
        grid=(sg, r // rt),
        in_specs=[pl.BlockSpec((1, rt, uw), lambda si, i: (si, i, 0)), hspec, hspec, wspec, wspec],
        out_specs=pl.BlockSpec((rt * S5_T, LANES), lambda si, i: (i, si)),
        out_shape=jax.ShapeDtypeStruct((r * S5_T, sg * LANES), F32),
        compiler_params=_params("parallel", "parallel"),
        name="s5_y",
    )(uf, hf2, hb2, m_blk, q_blk)


def _s5_mix(uf, ops, layer, h0, bsz, *, nb, rt):
    p_blk, m_blk, q_blk, at_re, at_im = ops
    r = uf.shape[1]
    n = r // bsz
    x = _s5_x_call(uf, p_blk, layer, rt=rt)
    hf, hb, hfin = _s5_scan_call(x.reshape(bsz, n, S5_GROUPS, S5_CW), at_re[layer], at_im[layer], h0, nb=nb)
    y = _s5_y_call(uf, hf.reshape(r, S5_GROUPS * S5_CW), hb.reshape(r, S5_GROUPS * S5_CW),
                   m_blk, q_blk, layer, rt=rt)
    return y.reshape(bsz, n * S5_T, S5_WIDTH), hfin


def _attn_kernel(q_ref, k_ref, vt_ref, o_ref, m_scr, acc_scr, s_scr, mx_scr, *, tk, n_kv, n_str):
    grp, tq = q_ref.shape[2], q_ref.shape[3] // n_str
    cols = grp * tq
    q = [q_ref[0, 0, :, st * tq:(st + 1) * tq, :].reshape(cols, LANES) for st in range(n_str)]
    m_scr[...] = jnp.full(m_scr.shape, -jnp.inf, F32)
    acc_scr[...] = jnp.zeros(acc_scr.shape, F32)

    def scores(j, st, slot):
        off = pl.multiple_of(j * tk, tk)
        s = _dotg(k_ref[0, pl.ds(off, tk), :], q[st], (1, 1))
        s_scr[st, slot] = s
        mx_scr[st, slot] = jnp.broadcast_to(jnp.max(s, axis=0, keepdims=True), (SUBLANES, cols))

    def softmax_pv(j, st, slot):
        off = pl.multiple_of(j * tk, tk)
        m_prev = m_scr[st]
        m_new = jnp.maximum(m_prev, mx_scr[st, slot])
        alpha = jnp.exp2(m_prev - m_new)
        p = jnp.exp2(s_scr[st, slot] - m_new[0:1]).astype(BF16)
        acc_scr[st] = acc_scr[st] * alpha[0:1] + _dot(vt_ref[0, 0, :, pl.ds(off, tk)], p)
        m_scr[st] = m_new

    def step(j_next, j_cur, slot_next, slot_cur):
        for st in range(n_str):
            if j_next is not None:
                scores(j_next, st, slot_next)
            if j_cur is not None:
                softmax_pv(j_cur, st, slot_cur)

    step(0, None, 0, None)

    def pair(t, carry):
        j = 2 * t
        step(j + 1, j, 1, 0)
        step(j + 2, j + 1, 0, 1)
        return carry

    lax.fori_loop(0, (n_kv - 1) // 2, pair, 0)
    if n_kv % 2 == 0:
        step(n_kv - 1, n_kv - 2, 1, 0)
        step(None, n_kv - 1, None, 1)
    else:
        step(None, n_kv - 1, None, 0)
    for st in range(n_str):
        acc = acc_scr[st]
        out_t = acc / acc[ATT_HEAD_DIM:ATT_HEAD_DIM + 1]
        for j in range(grp):
            o_ref[0, 0, j, st * tq:(st + 1) * tq, :] = out_t[:, j * tq:(j + 1) * tq].T.astype(o_ref.dtype)


def _attn_call(qs, k, vt_ext, *, tq, tk, n_str):
    b, kvh, grp, l, _ = qs.shape
    lk = k.shape[1]
    tq, tk = min(tq, l // n_str), min(tk, lk)
    cols = grp * tq
    blk = pl.BlockSpec((1, 1, grp, n_str * tq, LANES), lambda bi, ki, i: (bi, ki, 0, i, 0))
    return pl.pallas_call(
        functools.partial(_attn_kernel, tk=tk, n_kv=lk // tk, n_str=n_str),
        grid=(b, kvh, l // (n_str * tq)),
        in_specs=[blk,
                  pl.BlockSpec((1, lk, LANES), lambda bi, ki, i: (bi, 0, 0), pipeline_mode=pl.Buffered(1)),
                  pl.BlockSpec((1, 1, LANES, lk), lambda bi, ki, i: (bi, ki, 0, 0), pipeline_mode=pl.Buffered(1))],
        out_specs=blk,
        out_shape=jax.ShapeDtypeStruct(qs.shape, BF16),
        scratch_shapes=[pltpu.VMEM((n_str, SUBLANES, cols), F32), pltpu.VMEM((n_str, LANES, cols), F32),
                        pltpu.VMEM((n_str, 2, tk, cols), F32), pltpu.VMEM((n_str, 2, SUBLANES, cols), F32)],
        compiler_params=_params("parallel", "parallel", "arbitrary"),
        name="attention",
    )(qs, k, vt_ext)


def _kv_layout(ak, av):
    b, lk, _ = av.shape
    vt = jnp.transpose(av.reshape(b, lk, ATT_KV_HEADS, ATT_HEAD_DIM), (0, 2, 3, 1))
    return ak, jnp.concatenate([vt, jnp.ones_like(vt)], axis=2)


def _merge_kernel(h_ref, mod_ref, gn_ref, yg_ref, ys_ref, u_ref, ya_ref, wbg_ref, bbg_ref, wp_ref, wpa_ref,
                  wo_ref, d_ref, gw_ref, gb_ref, o_ref):
    h = h_ref[0]
    d = h.shape[-1]
    n = _rms_mod(h, gn_ref[1:2], mod_ref[0, MIX_MOD:MIX_MOD + 1], mod_ref[0, MIX_MOD + 1:MIX_MOD + 2]).astype(BF16)
    g = jax.nn.sigmoid(_dot(n, wbg_ref[...]) + bbg_ref[...])
    y = jax.nn.gelu(ys_ref[0] + d_ref[...] * u_ref[0])
    y_s5 = y * jax.nn.sigmoid(_dot(y.astype(BF16), gw_ref[...]) + gb_ref[...])
    ya = jnp.concatenate([ya_ref[0, kv, j] for kv in range(ATT_KV_HEADS) for j in range(ATT_GROUP)], axis=-1)
    m = (g[:, 0:d] * _dot(yg_ref[0].astype(BF16), wp_ref[0])
         + g[:, d:2 * d] * _dot(y_s5.astype(BF16), wp_ref[1])
         + g[:, 2 * d:3 * d] * _dot(ya, wpa_ref[...]))
    o_ref[0] = h + mod_ref[0, MIX_MOD + 2:MIX_MOD + 3] * _dot(m.astype(BF16), wo_ref[...])


def _merge_call(h, mods, norm_w, y_gla, y_s5raw, u, y_att, lw, *, layer, mod_row, tm):
    b, l, d = h.shape
    tm = min(tm, l)
    tok = lambda w: pl.BlockSpec((1, tm, w), lambda bi, i: (bi, i, 0))
    bw = y_gla.shape[-1]
    att = pl.BlockSpec((1, ATT_KV_HEADS, ATT_GROUP, tm, LANES), lambda bi, i: (bi, 0, 0, i, 0))
    return pl.pallas_call(
        _merge_kernel,
        grid=(b, l // tm),
        in_specs=[tok(d), _mod_spec(d, mod_row), _norm_spec(d, layer),
                  tok(bw), tok(bw), tok(bw), att,
                  _const_spec((d, 3 * d)), _const_spec((1, 3 * d)), _const_spec((2, bw, d)),
                  _const_spec((ATT_HEADS * LANES, d)), _const_spec((d, d)),
                  _const_spec((1, bw)), _const_spec((bw, bw)), _const_spec((1, bw))],
        out_specs=tok(d),
        out_shape=jax.ShapeDtypeStruct((b, l, d), F32),
        compiler_params=_params("parallel", "parallel"),
        name="merge",
    )(h, mods, norm_w, y_gla, y_s5raw, u, y_att, lw["w_bgate"], lw["b_bgate"], lw["w_bproj"],
      lw["w_aproj"], lw["w_out"], lw["s5_d"], lw["glu_w"], lw["glu_b"])


def _rope_tables(n_tokens):
    rows = n_tokens // GRID_W
    row = jnp.repeat(jnp.arange(rows, dtype=F32), GRID_W)
    col = jnp.tile(jnp.arange(GRID_W, dtype=F32), rows)
    inv = ROPE_THETA ** (-jnp.arange(ROPE_PAIRS, dtype=F32) / ROPE_PAIRS)
    ang_r, ang_c = row[:, None] * inv, col[:, None] * inv
    zero = jnp.zeros_like(ang_r)
    cos = jnp.concatenate([jnp.cos(ang_r), jnp.cos(ang_r), jnp.cos(ang_c), jnp.cos(ang_c)], axis=-1)
    sin_a = jnp.concatenate([-jnp.sin(ang_r), zero, -jnp.sin(ang_c), zero], axis=-1)
    sin_b = jnp.concatenate([zero, jnp.sin(ang_r), zero, jnp.sin(ang_c)], axis=-1)
    tile = lambda t: jnp.tile(t, (1, ATT_HEADS))
    return tile(cos), tile(sin_a), tile(sin_b)


def _block_diag_ones(width, seg):
    idx = np.arange(width) // seg
    return jnp.asarray((idx[:, None] == idx[None, :]).astype(np.float32), BF16)


def _layer_weights(i, w_in, gla_gate_w, gla_gate_b, attn_q_norm_w, attn_k_norm_w, w_branch_gate,
                   b_branch_gate, w_branch_proj, w_out, s5_d, s5_glu_w, s5_glu_b):
    offs = np.concatenate([[0], np.cumsum(IN_WIDTHS)])
    col = lambda k: w_in[i][:, offs[k]:offs[k + 1]]
    w_main = jnp.concatenate([col(0), col(1), col(2), col(5), col(6), col(7), col(8), col(9)], axis=1)
    d = w_in.shape[1]
    rk = GLA_GATE_RANK
    w_gz = jnp.zeros((d, LANES), F32).at[:, 0:rk].set(col(3)).at[:, rk:2 * rk].set(col(4))
    gw = jnp.zeros((LANES, 2 * GLA_QK), F32)
    gw = gw.at[0:rk, 0:GLA_QK].set(gla_gate_w[i, 0]).at[rk:2 * rk, GLA_QK:].set(gla_gate_w[i, 1])
    wa = w_branch_proj[i, 2].reshape(ATT_HEADS, ATT_HEAD_DIM, d)
    w_aproj = jnp.concatenate([wa, jnp.zeros_like(wa)], axis=1).reshape(ATT_HEADS * LANES, d)
    return dict(
        w_aproj=w_aproj.astype(BF16),
        w_main=w_main.astype(BF16), w_gz=w_gz.astype(BF16), gate_w_bd=gw.astype(BF16),
        gate_b=gla_gate_b[i].reshape(1, 2 * GLA_QK),
        q_norm_w=(jnp.tile(attn_q_norm_w[i], ATT_HEADS) * (ATT_HEAD_DIM ** -0.5 * LOG2_E)).reshape(1, ATT_Q),
        k_norm_w=jnp.tile(attn_k_norm_w[i], ATT_KV_HEADS).reshape(1, ATT_KV),
        bd_q=_block_diag_ones(ATT_Q, ATT_HEAD_DIM), bd_k=_block_diag_ones(ATT_KV, ATT_HEAD_DIM),
        w_bgate=w_branch_gate[i].astype(BF16), b_bgate=b_branch_gate[i].reshape(1, -1),
        w_bproj=w_branch_proj[i, 0:2].astype(BF16), w_out=w_out[i].astype(BF16),
        s5_d=s5_d[i].reshape(1, -1), glu_w=s5_glu_w[i].astype(BF16), glu_b=s5_glu_b[i].reshape(1, -1),
    )


def kernel(x, c, ctx, c_ctx, w_ada, b_ada, norm_w, w_ffn_gate, w_ffn_up, w_ffn_down, w_in, gla_gate_w,
           gla_gate_b, gla_norm_w, s5_a_re, s5_a_im, s5_log_dt, s5_b_re, s5_b_im, s5_c_re, s5_c_im, s5_d,
           s5_glu_w, s5_glu_b, attn_q_norm_w, attn_k_norm_w, w_branch_gate, b_branch_gate, w_branch_proj,
           w_out, final_norm_w):
    bsz, seq, d = x.shape
    depth = w_ada.shape[0]
    tables = _rope_tables(seq)
    ctx_tables = tuple(t[0:ctx.shape[1]] for t in tables)
    cond = jnp.zeros((SUBLANES, d), F32).at[0:bsz].set(c).at[bsz].set(c_ctx)
    tm_ffn, tm_in, tm_merge, tb_gla, nb_s5, rt_s5, tq, tk = (
        TILES[k] for k in ("ffn", "inproj", "merge", "gla", "s5_scan", "s5_rows", "att_q", "att_k"))

    ffn_w = (w_ffn_gate.astype(BF16), w_ffn_up.astype(BF16), w_ffn_down.astype(BF16))
    s5_ops = jax.vmap(_s5_operators)(s5_a_re, s5_a_im, s5_log_dt, s5_b_re, s5_b_im, s5_c_re, s5_c_im)
    h_lat, h_ctx = x, ctx
    for i in range(depth):
        last = i == depth - 1
        mods = _ada_call(cond, w_ada, b_ada, i).reshape(SUBLANES, N_MOD, d)
        lat, cxt = None, bsz
        lw = _layer_weights(i, w_in, gla_gate_w, gla_gate_b, attn_q_norm_w, attn_k_norm_w, w_branch_gate,
                            b_branch_gate, w_branch_proj, w_out, s5_d, s5_glu_w, s5_glu_b)

        def ffn(h, mod_row, j, fin=False):
            return _ffn_call(h, mods, norm_w, *ffn_w, final_norm_w, layer=i, sub=j, mod_row=mod_row,
                             final_norm=fin, tm=tm_ffn)

        h_lat = ffn(h_lat, lat, 0)
        h_ctx = ffn(h_ctx, cxt, 0)

        qk_c, v_c, r_c, lg_c, u_c, uf_c, aq_c, ak_c, av_c = _inproj_call(
            h_ctx, mods, norm_w, lw, ctx_tables, layer=i, mod_row=cxt, rope=False, tm=tm_in)
        qk_l, v_l, r_l, lg_l, u_l, uf_l, aq_l, ak_l, av_l = _inproj_call(
            h_lat, mods, norm_w, lw, tables, layer=i, mod_row=lat, rope=True, tm=tm_in)
        s0 = jnp.zeros((bsz, GLA_QK, GLA_DV), F32)
        yg_c, sf_c, sb_c = _gla_bidir(qk_c, v_c, lg_c, r_c, gla_norm_w[i], s0, s0, tb=tb_gla)
        yg_l, _, _ = _gla_bidir(qk_l, v_l, lg_l, r_l, gla_norm_w[i], sf_c, sb_c, tb=tb_gla)
        h0 = jnp.zeros((bsz, 2, S5_GROUPS, LANES), F32)
        ys_c, hfin_c = _s5_mix(uf_c, s5_ops, i, h0, bsz, nb=nb_s5, rt=rt_s5)
        ys_l, _ = _s5_mix(uf_l, s5_ops, i, hfin_c, bsz, nb=nb_s5, rt=rt_s5)
        kt, v_ext = _kv_layout(jnp.concatenate([ak_c, ak_l], axis=1), jnp.concatenate([av_c, av_l], axis=1))
        ya_l = _attn_call(aq_l, kt, v_ext, tq=tq, tk=tk, n_str=TILES["att_streams"])
        h_lat = _merge_call(h_lat, mods, norm_w, yg_l, ys_l, u_l, ya_l, lw, layer=i, mod_row=lat, tm=tm_merge)
        h_lat = ffn(h_lat, lat, 1, last)
        if not last:
            kt_c, v_ext_c = _kv_layout(ak_c, av_c)
            ya_c = _attn_call(aq_c, kt_c, v_ext_c, tq=tq, tk=tk, n_str=TILES["att_streams"])
            h_ctx = _merge_call(h_ctx, mods, norm_w, yg_c, ys_c, u_c, ya_c, lw, layer=i, mod_row=cxt, tm=tm_merge)
            h_ctx = ffn(h_ctx, cxt, 1)
    return h_lat
```

```python
import functools

import jax
import jax.numpy as jnp
import numpy as np
from jax import lax
from jax.experimental import pallas as pl
from jax.experimental.pallas import tpu as pltpu

F32 = jnp.float32
BF16 = jnp.bfloat16

N_MOD = 9
MIX_MOD = 3
MACARON_WEIGHT = 0.5
NORM_EPS = 1e-6
GRID_W = 64
GLA_HEADS = 4
GLA_DK = 64
GLA_DV = 128
GLA_QK = GLA_HEADS * GLA_DK
GLA_V = GLA_HEADS * GLA_DV
GLA_GATE_RANK = 16
GLA_TAU = 16.0
GLA_CHUNK = 64
S5_WIDTH = 512
S5_GROUP = 16
S5_GROUPS = S5_WIDTH // S5_GROUP
S5_STATE = 64
ATT_HEADS = 8
ATT_KV_HEADS = 2
ATT_GROUP = ATT_HEADS // ATT_KV_HEADS
ATT_HEAD_DIM = 64
ATT_Q = ATT_HEADS * ATT_HEAD_DIM
ATT_KV = ATT_KV_HEADS * ATT_HEAD_DIM
ROPE_PAIRS = ATT_HEAD_DIM // 4
ROPE_THETA = 10000.0
IN_WIDTHS = (GLA_QK, GLA_QK, GLA_V, GLA_GATE_RANK, GLA_GATE_RANK, GLA_V, S5_WIDTH, ATT_Q, ATT_KV, ATT_KV)

LANES = 128
SUBLANES = 8
MXU_DIM = 256
VMEM_LIMIT = 56 * 1024 * 1024

LOG2_E = 1.4426950408889634

S5_T = MXU_DIM // S5_GROUP
S5_CW = S5_T * S5_GROUP
S5_GPB = LANES // S5_GROUP
S5_SG = S5_GROUPS // S5_GPB
S5_UW = S5_T * LANES

TILES = dict(ffn=512, inproj=512, merge=512, gla=512, s5_scan=32, s5_rows=256, att_q=128, att_k=1280,
             att_streams=4)


def _params(*sem):
    return pltpu.CompilerParams(dimension_semantics=sem, vmem_limit_bytes=VMEM_LIMIT)


def _const_spec(shape):
    nd = len(shape)
    return pl.BlockSpec(shape, lambda *_: (0,) * nd, pipeline_mode=pl.Buffered(1))


def _mod_spec(d, mod_row):
    if mod_row is None:
        return pl.BlockSpec((1, N_MOD, d), lambda bi, i: (bi, 0, 0))
    return pl.BlockSpec((1, N_MOD, d), lambda bi, i: (mod_row, 0, 0))


def _norm_spec(d, layer):
    return pl.BlockSpec((None, 3, d), lambda bi, i: (layer, 0, 0), pipeline_mode=pl.Buffered(1))


def _dotg(a, b, contract):
    return lax.dot_general(a, b, (((contract[0],), (contract[1],)), ((), ())),
                           precision=lax.Precision.DEFAULT, preferred_element_type=F32)


def _dot(a, b):
    return _dotg(a, b, (1, 0))


def _dot_hi_lo(x, w):
    hi = x.astype(BF16)
    lo = (x - hi.astype(F32)).astype(BF16)
    return _dot(hi, w) + _dot(lo, w)


def _rms_mod(h, g_norm, shift, scale):
    ms = jnp.mean(h * h, axis=-1, keepdims=True)
    y = h * lax.rsqrt(ms + NORM_EPS) * g_norm
    return y * (1.0 + scale) + shift


def _log_sigmoid(x):
    return jnp.minimum(x, 0.0) - jnp.log(1.0 + jnp.exp(-jnp.abs(x)))


def _ada_kernel(c_ref, w_ref, b_ref, o_ref):
    c = c_ref[...]
    sc = (c * jax.nn.sigmoid(c)).astype(BF16)
    o_ref[...] = _dot(sc, w_ref[...].astype(BF16)) + b_ref[...]


def _ada_call(cond, w_all, b_all, layer):
    rows, d = cond.shape
    depth, _, n = w_all.shape
    tn = 9 * LANES
    return pl.pallas_call(
        _ada_kernel,
        grid=(n // tn,),
        in_specs=[pl.BlockSpec((rows, d), lambda j: (0, 0)),
                  pl.BlockSpec((None, d, tn), lambda j: (layer, 0, j)),
                  pl.BlockSpec((None, 1, tn), lambda j: (layer, 0, j))],
        out_specs=pl.BlockSpec((rows, tn), lambda j: (0, j)),
        out_shape=jax.ShapeDtypeStruct((rows, n), F32),
        compiler_params=_params("arbitrary"),
        name="adaln",
    )(cond, w_all, b_all.reshape(depth, 1, n))


def _ffn_kernel(h_ref, mod_ref, gn_ref, wg_ref, wu_ref, wd_ref, fn_ref, o_ref, *, mod_base, final_norm):
    h = h_ref[0]
    shift, scale, gate = (mod_ref[0, mod_base + k:mod_base + k + 1] for k in range(3))
    norm_row = mod_base // 3
    n = _rms_mod(h, gn_ref[norm_row:norm_row + 1], shift, scale).astype(BF16)
    g = _dot(n, wg_ref[...])
    u = _dot(n, wu_ref[...])
    a = (g * jax.nn.sigmoid(g) * u).astype(BF16)
    y = h + (MACARON_WEIGHT * gate) * _dot(a, wd_ref[...])
    if final_norm:
        ms = jnp.mean(y * y, axis=-1, keepdims=True)
        y = y * lax.rsqrt(ms + NORM_EPS) * fn_ref[...]
    o_ref[0] = y


def _ffn_call(h, mods, norm_w, wg, wu, wd, final_w, *, layer, sub, mod_row, final_norm, tm):
    b, l, d = h.shape
    f = wg.shape[-1]
    tm = min(tm, l)
    wspec = lambda r, c: pl.BlockSpec((None, None, r, c), lambda bi, i: (layer, sub, 0, 0),
                                      pipeline_mode=pl.Buffered(1))
    return pl.pallas_call(
        functools.partial(_ffn_kernel, mod_base=6 * sub, final_norm=final_norm),
        grid=(b, l // tm),
        in_specs=[pl.BlockSpec((1, tm, d), lambda bi, i: (bi, i, 0)),
                  _mod_spec(d, mod_row), _norm_spec(d, layer),
                  wspec(d, f), wspec(d, f), wspec(f, d), _const_spec((1, d))],
        out_specs=pl.BlockSpec((1, tm, d), lambda bi, i: (bi, i, 0)),
        out_shape=jax.ShapeDtypeStruct((b, l, d), F32),
        compiler_params=_params("parallel", "parallel"),
        name="ffn",
    )(h, mods, norm_w, wg, wu, wd, final_w.reshape(1, d))


def _inproj_kernel(h_ref, mod_ref, gn_ref, wm_ref, wgz_ref, gw_ref, gb_ref, qw_ref, kw_ref,
                   bdq_ref, bdk_ref, cos_ref, sa_ref, sb_ref,
                   qk_ref, v_ref, r_ref, lg_ref, u_ref, uf_ref, aq_ref, ak_ref, av_ref, us_scr, *, rope):
    h = h_ref[0]
    tm = h.shape[0]
    n = _rms_mod(h, gn_ref[1:2], mod_ref[0, MIX_MOD:MIX_MOD + 1], mod_ref[0, MIX_MOD + 1:MIX_MOD + 2]).astype(BF16)
    z = _dot(n, wm_ref[...])
    o = 0
    q = z[:, o:o + GLA_QK] * (GLA_DK ** -0.5); o += GLA_QK
    k = z[:, o:o + GLA_QK]; o += GLA_QK
    qk_ref[0, :, 0:GLA_QK] = q
    qk_ref[0, :, GLA_QK:2 * GLA_QK] = k
    v_ref[0] = z[:, o:o + GLA_V]; o += GLA_V
    r_ref[0] = z[:, o:o + GLA_V]; o += GLA_V
    u_ref[0] = z[:, o:o + S5_WIDTH]; o += S5_WIDTH
    for sg in range(S5_SG):
        us_scr[sg] = z[:, o - S5_WIDTH + sg * LANES:o - S5_WIDTH + (sg + 1) * LANES]
        for t in range(S5_T):
            piece = us_scr[sg, pl.ds(t, tm // S5_T, stride=S5_T), :]
            uf_ref[sg, :, t * LANES:(t + 1) * LANES] = piece.astype(BF16)
    xq = z[:, o:o + ATT_Q]; o += ATT_Q
    xk = z[:, o:o + ATT_KV]; o += ATT_KV
    av_ref[0] = z[:, o:o + ATT_KV].astype(BF16)

    zg = _dot(n, wgz_ref[...]).astype(BF16)
    lg_ref[0] = _log_sigmoid(_dot(zg, gw_ref[...]) + gb_ref[...]) * (1.0 / GLA_TAU)

    def head_norm(x, bd_ref, w_ref):
        ss = _dot_hi_lo(x * x, bd_ref[...])
        return x * lax.rsqrt(ss * (1.0 / ATT_HEAD_DIM) + NORM_EPS) * w_ref[...]

    def rotary(x, width):
        if not rope:
            return x
        c, sa, sb = cos_ref[:, 0:width], sa_ref[:, 0:width], sb_ref[:, 0:width]
        return (x * c + pltpu.roll(x, width - ROPE_PAIRS, 1) * sa + pltpu.roll(x, ROPE_PAIRS, 1) * sb)

    ak_ref[0] = rotary(head_norm(xk, bdk_ref, kw_ref), ATT_KV).astype(BF16)
    aq = rotary(head_norm(xq, bdq_ref, qw_ref), ATT_Q)
    lane = lax.broadcasted_iota(jnp.int32, (tm, LANES), 1)
    for kv in range(ATT_KV_HEADS):
        keep = (lane >= kv * ATT_HEAD_DIM) & (lane < (kv + 1) * ATT_HEAD_DIM)
        for j in range(ATT_GROUP):
            hd = kv * ATT_GROUP + j
            blk = aq[:, (hd // 2) * LANES:(hd // 2 + 1) * LANES]
            if hd % 2 != kv:
                blk = pltpu.roll(blk, ATT_HEAD_DIM, 1)
            aq_ref[0, kv, j] = jnp.where(keep, blk, 0.0).astype(BF16)


def _inproj_call(h, mods, norm_w, lw, tables, *, layer, mod_row, rope, tm):
    b, l, d = h.shape
    tm = min(tm, l)
    tok = lambda w: pl.BlockSpec((1, tm, w), lambda bi, i: (bi, i, 0))
    tab = pl.BlockSpec((tm, ATT_Q), lambda bi, i: (i, 0))
    wm = lw["w_main"]
    nt = l // tm
    tok_out = lambda w, dt: (tok(w), jax.ShapeDtypeStruct((b, l, w), dt))
    outs = [tok_out(2 * GLA_QK, F32), tok_out(GLA_V, F32), tok_out(GLA_V, F32), tok_out(2 * GLA_QK, F32),
            tok_out(S5_WIDTH, F32),
            (pl.BlockSpec((S5_SG, tm // S5_T, S5_UW), lambda bi, i: (0, bi * nt + i, 0)),
             jax.ShapeDtypeStruct((S5_SG, b * l // S5_T, S5_UW), BF16)),
            (pl.BlockSpec((1, ATT_KV_HEADS, ATT_GROUP, tm, LANES), lambda bi, i: (bi, 0, 0, i, 0)),
             jax.ShapeDtypeStruct((b, ATT_KV_HEADS, ATT_GROUP, l, LANES), BF16)),
            tok_out(ATT_KV, BF16), tok_out(ATT_KV, BF16)]
    return pl.pallas_call(
        functools.partial(_inproj_kernel, rope=rope),
        grid=(b, nt),
        in_specs=[tok(d), _mod_spec(d, mod_row), _norm_spec(d, layer),
                  _const_spec(wm.shape), _const_spec((d, LANES)), _const_spec((LANES, 2 * GLA_QK)),
                  _const_spec((1, 2 * GLA_QK)), _const_spec((1, ATT_Q)), _const_spec((1, ATT_KV)),
                  _const_spec((ATT_Q, ATT_Q)), _const_spec((ATT_KV, ATT_KV)), tab, tab, tab],
        out_specs=[s for s, _ in outs],
        out_shape=[t for _, t in outs],
        scratch_shapes=[pltpu.VMEM((S5_SG, tm, LANES), F32)],
        compiler_params=_params("parallel", "parallel"),
        name="inproj",
    )(h, mods, norm_w, wm, lw["w_gz"], lw["gate_w_bd"], lw["gate_b"], lw["q_norm_w"],
      lw["k_norm_w"], lw["bd_q"], lw["bd_k"], *tables)


def _gla_kernel(qk_ref, v_ref, lg_ref, s0_ref, tri_ref, of_ref, r_ref, nw_ref, o_ref, sfin_ref, s_scr,
                *, reverse, readout, n_chunks, n_batch):
    i = pl.program_id(0)

    @pl.when(i == 0)
    def _():
        s_scr[...] = s0_ref[...]

    tri = tri_ref[...]
    c, nh = GLA_CHUNK, GLA_HEADS
    row = lax.broadcasted_iota(jnp.int32, (nh * c, c), 0) % c
    col = lax.broadcasted_iota(jnp.int32, (nh * c, c), 1)
    keep = (col >= row) if reverse else (col <= row)
    head_of_lane = lax.broadcasted_iota(jnp.int32, (c, GLA_QK), 1) // GLA_DK

    def chunk(bi, ci):
        rows = slice(ci * c, (ci + 1) * c)
        g = lg_ref[bi, rows, :]
        g_hi = g.astype(BF16)
        b = _dot(tri, g_hi) + _dot(tri, (g - g_hi.astype(F32)).astype(BF16))
        b_last = b[0:1, :] if reverse else b[c - 1:c, :]
        q_in = qk_ref[bi, rows, 0:GLA_QK] * jnp.exp(b)
        k = qk_ref[bi, rows, GLA_QK:2 * GLA_QK]
        k_in = (k * jnp.exp(-b)).astype(BF16)
        k_out = (k * jnp.exp(b_last - b)).astype(BF16)
        dec = jnp.exp(jnp.broadcast_to(b_last, (LANES, GLA_QK)).T)
        v = v_ref[bi, rows, :].astype(BF16)
        s_old = s_scr[bi]
        q_st = jnp.concatenate([jnp.where(head_of_lane == hd, q_in, 0.0) for hd in range(nh)],
                               axis=0).astype(BF16)
        att = _dotg(q_st, k_in, (1, 1))
        att = jnp.where(keep, att, 0.0).astype(BF16)
        o_inter = _dot(q_st, s_old.astype(BF16))
        o_intra, ds = [], []
        for pr in range(nh // 2):
            vp = v[:, 2 * pr * GLA_DV:(2 * pr + 2) * GLA_DV]
            o_intra.append(_dot(att[2 * pr * c:(2 * pr + 2) * c], vp))
            ds.append(_dotg(k_out[:, pr * LANES:(pr + 1) * LANES], vp, (0, 0)))
        for hd in range(nh):
            tok = slice(hd * c, (hd + 1) * c)
            vcols = slice(hd * GLA_DV, (hd + 1) * GLA_DV)
            e = hd % 2
            o_h = o_intra[hd // 2][e * c:(e + 1) * c, e * GLA_DV:(e + 1) * GLA_DV] + o_inter[tok]
            if readout:
                o_h = o_h + of_ref[bi, rows, vcols]
                ms = jnp.mean(o_h * o_h, axis=-1, keepdims=True)
                y = o_h * lax.rsqrt(ms + NORM_EPS) * nw_ref[...]
                rr = r_ref[bi, rows, vcols]
                o_h = y * (rr * jax.nn.sigmoid(rr))
            o_ref[bi, rows, vcols] = o_h
        d_state = jnp.concatenate([ds[hd // 2][(hd % 2) * GLA_DK:(hd % 2 + 1) * GLA_DK,
                                               (hd % 2) * GLA_DV:(hd % 2 + 1) * GLA_DV]
                                   for hd in range(nh)], axis=0)
        s_scr[bi] = dec * s_old + d_state

    for ci in (range(n_chunks - 1, -1, -1) if reverse else range(n_chunks)):
        for bi in range(n_batch):
            chunk(bi, ci)

    @pl.when(i == pl.num_programs(0) - 1)
    def _():
        sfin_ref[...] = s_scr[...]


def _gla_call(qk, v, lg, s0, o_f, r, norm_w, *, reverse, readout, tb):
    b, l, _ = v.shape
    tb = min(tb, l)
    nb = l // tb
    idx = (lambda i: (0, nb - 1 - i, 0)) if reverse else (lambda i: (0, i, 0))
    tok = lambda w: pl.BlockSpec((b, tb, w), idx)
    lg_off = 1 if reverse else 0
    lg_spec = pl.BlockSpec((b, tb, GLA_QK), (lambda i: (0, nb - 1 - i, lg_off)) if reverse
                           else (lambda i: (0, i, lg_off)))
    state = pl.BlockSpec((b, GLA_QK, GLA_DV), lambda i: (0, 0, 0))
    ones = np.triu(np.ones((GLA_CHUNK, GLA_CHUNK), np.float32)) if reverse else \
        np.tril(np.ones((GLA_CHUNK, GLA_CHUNK), np.float32))
    return pl.pallas_call(
        functools.partial(_gla_kernel, reverse=reverse, readout=readout, n_chunks=tb // GLA_CHUNK, n_batch=b),
        grid=(nb,),
        in_specs=[tok(2 * GLA_QK), tok(GLA_V), lg_spec, state, _const_spec((GLA_CHUNK, GLA_CHUNK)),
                  tok(GLA_V), tok(GLA_V), _const_spec((1, GLA_DV))],
        out_specs=[tok(GLA_V), state],
        out_shape=[jax.ShapeDtypeStruct((b, l, GLA_V), F32), jax.ShapeDtypeStruct((b, GLA_QK, GLA_DV), F32)],
        scratch_shapes=[pltpu.VMEM((b, GLA_QK, GLA_DV), F32)],
        compiler_params=_params("arbitrary"),
        name="gla_bwd" if reverse else "gla_fwd",
    )(qk, v, lg, s0, jnp.asarray(ones, BF16), o_f, r, norm_w.reshape(1, GLA_DV))


def _gla_bidir(qk, v, lg, r, norm_w, s0_f, s0_b, *, tb):
    o_f, s_f = _gla_call(qk, v, lg, s0_f, v, r, norm_w, reverse=False, readout=False, tb=tb)
    y, s_b = _gla_call(qk, v, lg, s0_b, o_f, r, norm_w, reverse=True, readout=True, tb=tb)
    return y, s_f, s_b


def _s5_operators(a_re, a_im, log_dt, b_re, b_im, c_re, c_im):
    hp = lax.Precision.HIGHEST
    t = S5_T
    cr, ci_ = c_re.astype(F32), c_im.astype(F32)
    ks = jnp.arange(t + 1, dtype=F32)[:, None, None]
    per_dir = []
    for d in range(2):
        dt = jnp.exp(log_dt[d].astype(F32))[:, None]
        ar, ai = a_re[d].astype(F32), a_im[d].astype(F32)
        mag = jnp.exp(ks * dt * ar)
        pw_re, pw_im = mag * jnp.cos(ks * dt * ai), mag * jnp.sin(ks * dt * ai)
        den = ar * ar + ai * ai
        xr, xi = pw_re[1] - 1.0, pw_im[1]
        coef_re = (xr * ar + xi * ai) / den
        coef_im = (xi * ar - xr * ai) / den
        br, bi = b_re.astype(F32), b_im.astype(F32)
        bb_re = coef_re[..., None] * br - coef_im[..., None] * bi
        bb_im = coef_re[..., None] * bi + coef_im[..., None] * br
        ca_re = cr[None] * pw_re[:, :, None, :] - ci_[None] * pw_im[:, :, None, :]
        ca_im = cr[None] * pw_im[:, :, None, :] + ci_[None] * pw_re[:, :, None, :]
        kk = (jnp.einsum("kgop,gpi->kgoi", ca_re, bb_re, precision=hp)
              - jnp.einsum("kgop,gpi->kgoi", ca_im, bb_im, precision=hp))
        s_idx = jnp.arange(t)[:, None]
        t_idx = jnp.arange(t)[None, :]
        lag = (t_idx - s_idx) if d == 0 else (s_idx - t_idx)
        toe = jnp.where((lag >= 0)[:, :, None, None, None], kk[jnp.clip(lag, 0, t)], 0.0)
        m = jnp.transpose(toe, (2, 0, 4, 1, 3)).reshape(S5_GROUPS, S5_CW, S5_CW)
        p_pow = (t - 1 - jnp.arange(t)) if d == 0 else jnp.arange(t)
        pr, pi = pw_re[p_pow], pw_im[p_pow]
        p_re = pr[:, :, :, None] * bb_re[None] - pi[:, :, :, None] * bb_im[None]
        p_im = pr[:, :, :, None] * bb_im[None] + pi[:, :, :, None] * bb_re[None]
        flat_p = lambda x: jnp.transpose(x, (1, 0, 3, 2)).reshape(S5_GROUPS, S5_CW, S5_STATE)
        q_pow = (jnp.arange(t) + 1) if d == 0 else (t - jnp.arange(t))
        q_re, q_im = ca_re[q_pow], -ca_im[q_pow]
        flat_q = lambda x: jnp.transpose(x, (1, 3, 0, 2)).reshape(S5_GROUPS, S5_STATE, S5_CW)
        per_dir.append(dict(m=m, p_re=flat_p(p_re), p_im=flat_p(p_im), q_re=flat_q(q_re), q_im=flat_q(q_im),
                            at_re=pw_re[t], at_im=pw_im[t]))
    f, bk = per_dir
    zq = jnp.zeros_like(f["q_re"])
    p_cat = jnp.concatenate([f["p_re"], bk["p_re"], f["p_im"], bk["p_im"]], axis=-1)
    q_f = jnp.concatenate([f["q_re"], zq, f["q_im"], zq], axis=1)
    q_b = jnp.concatenate([zq, bk["q_re"], zq, bk["q_im"]], axis=1)
    at_re = jnp.concatenate([f["at_re"], bk["at_re"]], axis=-1)
    at_im = jnp.concatenate([f["at_im"], bk["at_im"]], axis=-1)

    sg, gpb, t, gc = S5_SG, S5_GPB, S5_T, S5_GROUP
    idx = jnp.arange(S5_UW)
    gl_tgc = (idx // gc) % gpb
    gl_gk = idx // S5_CW
    src = np.arange(S5_UW)
    spread = np.zeros((S5_CW, S5_UW), np.float32)
    spread[(src // LANES) * gc + src % gc, src] = 1.0
    spread = jnp.asarray(spread, BF16)

    def rows_tgc(a):
        k = a.shape[-1]
        return jnp.transpose(a.astype(BF16).reshape(sg, gpb, t, gc, k), (0, 2, 1, 3, 4)).reshape(sg, S5_UW, k)

    def cols_tgc(a):
        return lax.dot_general(a, spread, (((2,), (0,)), ((), ())), preferred_element_type=BF16)

    def keep(rows_gl, cols_gl, a):
        return jnp.where((rows_gl[:, None] == cols_gl[None, :])[None], a, jnp.zeros_like(a))

    p_rows = rows_tgc(p_cat)
    own = (gl_tgc[:, None] == jnp.arange(gpb)[None, :])[None, :, :, None]
    p_blk = jnp.where(own, p_rows[:, :, None, :], jnp.zeros((), BF16)).reshape(sg, S5_UW, S5_UW)
    m_blk = keep(gl_tgc, gl_tgc, cols_tgc(rows_tgc(f["m"] + bk["m"])))
    q_blk = keep(gl_gk, gl_tgc, cols_tgc((q_f + q_b).astype(BF16).reshape(sg, S5_UW, S5_CW)))
    return p_blk, m_blk, q_blk, at_re, at_im


def _s5_x_kernel(u_ref, p_ref, x_ref):
    x_ref[...] = _dot(u_ref[0], p_ref[0])


def _s5_x_call(uf, p_blk, layer, *, rt):
    sg, r, uw = uf.shape
    rt = min(rt, r)
    return pl.pallas_call(
        _s5_x_kernel,
        grid=(sg, r // rt),
        in_specs=[pl.BlockSpec((1, rt, uw), lambda si, i: (si, i, 0)),
                  pl.BlockSpec((None, 1, uw, uw), lambda si, i: (layer, si, 0, 0), pipeline_mode=pl.Buffered(1))],
        out_specs=pl.BlockSpec((rt, uw), lambda si, i: (i, si)),
        out_shape=jax.ShapeDtypeStruct((r, sg * uw), F32),
        compiler_params=_params("parallel", "parallel"),
        name="s5_x",
    )(uf, p_blk)


def _s5_scan_kernel(xf_ref, xb_ref, are_ref, aim_ref, h0_ref, hf_ref, hb_ref, hfin_ref, hr_scr, hi_scr, *, nb):
    i = pl.program_id(0)

    @pl.when(i == 0)
    def _():
        hr_scr[...] = h0_ref[:, 0]
        hi_scr[...] = h0_ref[:, 1]

    ar, ai = are_ref[...][None], aim_ref[...][None]
    fwd = lax.broadcasted_iota(jnp.int32, (1, S5_GROUPS, LANES), 2) < S5_STATE

    def step(j, carry):
        hr, hi = carry
        jb = nb - 1 - j
        xr = jnp.where(fwd, xf_ref[:, j, :, 0:LANES], xb_ref[:, jb, :, 0:LANES])
        xi = jnp.where(fwd, xf_ref[:, j, :, LANES:2 * LANES], xb_ref[:, jb, :, LANES:2 * LANES])
        hf_ref[:, j, :, 0:LANES] = hr
        hf_ref[:, j, :, LANES:2 * LANES] = hi
        hb_ref[:, jb, :, 0:LANES] = hr
        hb_ref[:, jb, :, LANES:2 * LANES] = hi
        return ar * hr - ai * hi + xr, ar * hi + ai * hr + xi

    hr, hi = lax.fori_loop(0, nb, step, (hr_scr[...], hi_scr[...]))
    hr_scr[...] = hr
    hi_scr[...] = hi

    @pl.when(i == pl.num_programs(0) - 1)
    def _():
        hfin_ref[:, 0] = hr
        hfin_ref[:, 1] = hi


def _s5_scan_call(x4, at_re, at_im, h0, *, nb):
    b, n, g, w = x4.shape
    nb = min(nb, n)
    steps = n // nb
    blk = lambda rev: pl.BlockSpec((b, nb, g, w), (lambda i: (0, steps - 1 - i, 0, 0)) if rev
                                   else (lambda i: (0, i, 0, 0)))
    st = pl.BlockSpec((b, 2, g, LANES), lambda i: (0, 0, 0, 0))
    return pl.pallas_call(
        functools.partial(_s5_scan_kernel, nb=nb),
        grid=(steps,),
        in_specs=[blk(False), blk(True), _const_spec((g, LANES)), _const_spec((g, LANES)), st],
        out_specs=[blk(False), blk(True), st],
        out_shape=[jax.ShapeDtypeStruct(x4.shape, F32), jax.ShapeDtypeStruct(x4.shape, F32),
                   jax.ShapeDtypeStruct((b, 2, g, LANES), F32)],
        scratch_shapes=[pltpu.VMEM((b, g, LANES), F32), pltpu.VMEM((b, g, LANES), F32)],
        compiler_params=_params("arbitrary"),
        name="s5_scan",
    )(x4, x4, at_re, at_im, h0)


def _s5_y_kernel(u_ref, hf_ref, hb_ref, m_ref, q_ref, y_ref):
    rt = hf_ref.shape[0]
    lane = lax.broadcasted_iota(jnp.int32, (rt, S5_UW), 1)
    h = jnp.where(lane % LANES < S5_STATE, hf_ref[...], hb_ref[...]).astype(BF16)
    y = _dot(u_ref[0], m_ref[0]) + _dot(h, q_ref[0])
    for t in range(S5_T):
        y_ref[pl.ds(t, rt, stride=S5_T), :] = y[:, t * LANES:(t + 1) * LANES]


def _s5_y_call(uf, hf2, hb2, m_blk, q_blk, layer, *, rt):
    sg, r, uw = uf.shape
    rt = min(rt, r)
    wspec = pl.BlockSpec((None, 1, uw, uw), lambda si, i: (layer, si, 0, 0), pipeline_mode=pl.Buffered(1))
    hspec = pl.BlockSpec((rt, uw), lambda si, i: (i, si))
    return pl.pallas_call(
        _s5_y_kernel,
        grid=(sg, r // rt),
        in_specs=[pl.BlockSpec((1, rt, uw), lambda si, i: (si, i, 0)), hspec, hspec, wspec, wspec],
        out_specs=pl.BlockSpec((rt * S5_T, LANES), lambda si, i: (i, si)),
        out_shape=jax.ShapeDtypeStruct((r * S5_T, sg * LANES), F32),
        compiler_params=_params("parallel", "parallel"),
        name="s5_y",
    )(uf, hf2, hb2, m_blk, q_blk)


def _s5_mix(uf, ops, layer, h0, bsz, *, nb, rt):
    p_blk, m_blk, q_blk, at_re, at_im = ops
    r = uf.shape[1]
    n = r // bsz
    x = _s5_x_call(uf, p_blk, layer, rt=rt)
    hf, hb, hfin = _s5_scan_call(x.reshape(bsz, n, S5_GROUPS, S5_CW), at_re[layer], at_im[layer], h0, nb=nb)
    y = _s5_y_call(uf, hf.reshape(r, S5_GROUPS * S5_CW), hb.reshape(r, S5_GROUPS * S5_CW),
                   m_blk, q_blk, layer, rt=rt)
    return y.reshape(bsz, n * S5_T, S5_WIDTH), hfin


def _attn_kernel(q_ref, k_ref, vt_ref, o_ref, m_scr, acc_scr, s_scr, mx_scr, *, tk, n_kv, n_str):
    grp, tq = q_ref.shape[2], q_ref.shape[3] // n_str
    cols = grp * tq
    q = [q_ref[0, 0, :, st * tq:(st + 1) * tq, :].reshape(cols, LANES) for st in range(n_str)]
    m_scr[...] = jnp.full(m_scr.shape, -jnp.inf, F32)
    acc_scr[...] = jnp.zeros(acc_scr.shape, F32)

    def scores(j, st, slot):
        off = pl.multiple_of(j * tk, tk)
        s = _dotg(k_ref[0, pl.ds(off, tk), :], q[st], (1, 1))
        s_scr[st, slot] = s
        mx_scr[st, slot] = jnp.broadcast_to(jnp.max(s, axis=0, keepdims=True), (SUBLANES, cols))

    def softmax_pv(j, st, slot):
        off = pl.multiple_of(j * tk, tk)
        m_prev = m_scr[st]
        m_new = jnp.maximum(m_prev, mx_scr[st, slot])
        alpha = jnp.exp2(m_prev - m_new)
        p = jnp.exp2((s_scr[st, slot] - m_new[0:1]).astype(BF16))
        acc_scr[st] = acc_scr[st] * alpha[0:1] + _dot(vt_ref[0, 0, :, pl.ds(off, tk)], p)
        m_scr[st] = m_new

    def step(j_next, j_cur, slot_next, slot_cur):
        for st in range(n_str):
            if j_next is not None:
                scores(j_next, st, slot_next)
            if j_cur is not None:
                softmax_pv(j_cur, st, slot_cur)

    step(0, None, 0, None)

    def pair(t, carry):
        j = 2 * t
        step(j + 1, j, 1, 0)
        step(j + 2, j + 1, 0, 1)
        return carry

    lax.fori_loop(0, (n_kv - 1) // 2, pair, 0)
    if n_kv % 2 == 0:
        step(n_kv - 1, n_kv - 2, 1, 0)
        step(None, n_kv - 1, None, 1)
    else:
        step(None, n_kv - 1, None, 0)
    for st in range(n_str):
        acc = acc_scr[st]
        out_t = acc / acc[ATT_HEAD_DIM:ATT_HEAD_DIM + 1]
        for j in range(grp):
            o_ref[0, 0, j, st * tq:(st + 1) * tq, :] = out_t[:, j * tq:(j + 1) * tq].T.astype(o_ref.dtype)


def _attn_call(qs, k, vt_ext, *, tq, tk, n_str):
    b, kvh, grp, l, _ = qs.shape
    lk = k.shape[1]
    tq, tk = min(tq, l // n_str), min(tk, lk)
    cols = grp * tq
    blk = pl.BlockSpec((1, 1, grp, n_str * tq, LANES), lambda bi, ki, i: (bi, ki, 0, i, 0))
    return pl.pallas_call(
        functools.partial(_attn_kernel, tk=tk, n_kv=lk // tk, n_str=n_str),
        grid=(b, kvh, l // (n_str * tq)),
        in_specs=[blk,
                  pl.BlockSpec((1, lk, LANES), lambda bi, ki, i: (bi, 0, 0), pipeline_mode=pl.Buffered(1)),
                  pl.BlockSpec((1, 1, LANES, lk), lambda bi, ki, i: (bi, ki, 0, 0), pipeline_mode=pl.Buffered(1))],
        out_specs=blk,
        out_shape=jax.ShapeDtypeStruct(qs.shape, BF16),
        scratch_shapes=[pltpu.VMEM((n_str, SUBLANES, cols), F32), pltpu.VMEM((n_str, LANES, cols), F32),
                        pltpu.VMEM((n_str, 2, tk, cols), F32), pltpu.VMEM((n_str, 2, SUBLANES, cols), F32)],
        compiler_params=_params("parallel", "parallel", "arbitrary"),
        name="attention",
    )(qs, k, vt_ext)


def _kv_layout(ak, av):
    b, lk, _ = av.shape
    vt = jnp.transpose(av.reshape(b, lk, ATT_KV_HEADS, ATT_HEAD_DIM), (0, 2, 3, 1))
    return ak, jnp.concatenate([vt, jnp.ones_like(vt)], axis=2)


def _merge_kernel(h_ref, mod_ref, gn_ref, yg_ref, ys_ref, u_ref, ya_ref, wbg_ref, bbg_ref, wp_ref, wpa_ref,
                  wo_ref, d_ref, gw_ref, gb_ref, o_ref):
    h = h_ref[0]
    d = h.shape[-1]
    n = _rms_mod(h, gn_ref[1:2], mod_ref[0, MIX_MOD:MIX_MOD + 1], mod_ref[0, MIX_MOD + 1:MIX_MOD + 2]).astype(BF16)
    g = jax.nn.sigmoid(_dot(n, wbg_ref[...]) + bbg_ref[...])
    y = jax.nn.gelu(ys_ref[0] + d_ref[...] * u_ref[0])
    y_s5 = y * jax.nn.sigmoid(_dot(y.astype(BF16), gw_ref[...]) + gb_ref[...])
    ya = jnp.concatenate([ya_ref[0, kv, j] for kv in range(ATT_KV_HEADS) for j in range(ATT_GROUP)], axis=-1)
    m = (g[:, 0:d] * _dot(yg_ref[0].astype(BF16), wp_ref[0])
         + g[:, d:2 * d] * _dot(y_s5.astype(BF16), wp_ref[1])
         + g[:, 2 * d:3 * d] * _dot(ya, wpa_ref[...]))
    o_ref[0] = h + mod_ref[0, MIX_MOD + 2:MIX_MOD + 3] * _dot(m.astype(BF16), wo_ref[...])


def _merge_call(h, mods, norm_w, y_gla, y_s5raw, u, y_att, lw, *, layer, mod_row, tm):
    b, l, d = h.shape
    tm = min(tm, l)
    tok = lambda w: pl.BlockSpec((1, tm, w), lambda bi, i: (bi, i, 0))
    bw = y_gla.shape[-1]
    att = pl.BlockSpec((1, ATT_KV_HEADS, ATT_GROUP, tm, LANES), lambda bi, i: (bi, 0, 0, i, 0))
    return pl.pallas_call(
        _merge_kernel,
        grid=(b, l // tm),
        in_specs=[tok(d), _mod_spec(d, mod_row), _norm_spec(d, layer),
                  tok(bw), tok(bw), tok(bw), att,
                  _const_spec((d, 3 * d)), _const_spec((1, 3 * d)), _const_spec((2, bw, d)),
                  _const_spec((ATT_HEADS * LANES, d)), _const_spec((d, d)),
                  _const_spec((1, bw)), _const_spec((bw, bw)), _const_spec((1, bw))],
        out_specs=tok(d),
        out_shape=jax.ShapeDtypeStruct((b, l, d), F32),
        compiler_params=_params("parallel", "parallel"),
        name="merge",
    )(h, mods, norm_w, y_gla, y_s5raw, u, y_att, lw["w_bgate"], lw["b_bgate"], lw["w_bproj"],
      lw["w_aproj"], lw["w_out"], lw["s5_d"], lw["glu_w"], lw["glu_b"])


def _rope_tables(n_tokens):
    rows = n_tokens // GRID_W
    row = jnp.repeat(jnp.arange(rows, dtype=F32), GRID_W)
    col = jnp.tile(jnp.arange(GRID_W, dtype=F32), rows)
    inv = ROPE_THETA ** (-jnp.arange(ROPE_PAIRS, dtype=F32) / ROPE_PAIRS)
    ang_r, ang_c = row[:, None] * inv, col[:, None] * inv
    zero = jnp.zeros_like(ang_r)
    cos = jnp.concatenate([jnp.cos(ang_r), jnp.cos(ang_r), jnp.cos(ang_c), jnp.cos(ang_c)], axis=-1)
    sin_a = jnp.concatenate([-jnp.sin(ang_r), zero, -jnp.sin(ang_c), zero], axis=-1)
    sin_b = jnp.concatenate([zero, jnp.sin(ang_r), zero, jnp.sin(ang_c)], axis=-1)
    tile = lambda t: jnp.tile(t, (1, ATT_HEADS))
    return tile(cos), tile(sin_a), tile(sin_b)


def _block_diag_ones(width, seg):
    idx = np.arange(width) // seg
    return jnp.asarray((idx[:, None] == idx[None, :]).astype(np.float32), BF16)


def _layer_weights(i, w_in, gla_gate_w, gla_gate_b, attn_q_norm_w, attn_k_norm_w, w_branch_gate,
                   b_branch_gate, w_branch_proj, w_out, s5_d, s5_glu_w, s5_glu_b):
    offs = np.concatenate([[0], np.cumsum(IN_WIDTHS)])
    col = lambda k: w_in[i][:, offs[k]:offs[k + 1]]
    w_main = jnp.concatenate([col(0), col(1), col(2), col(5), col(6), col(7), col(8), col(9)], axis=1)
    d = w_in.shape[1]
    rk = GLA_GATE_RANK
    w_gz = jnp.zeros((d, LANES), F32).at[:, 0:rk].set(col(3)).at[:, rk:2 * rk].set(col(4))
    gw = jnp.zeros((LANES, 2 * GLA_QK), F32)
    gw = gw.at[0:rk, 0:GLA_QK].set(gla_gate_w[i, 0]).at[rk:2 * rk, GLA_QK:].set(gla_gate_w[i, 1])
    wa = w_branch_proj[i, 2].reshape(ATT_HEADS, ATT_HEAD_DIM, d)
    w_aproj = jnp.concatenate([wa, jnp.zeros_like(wa)], axis=1).reshape(ATT_HEADS * LANES, d)
    return dict(
        w_aproj=w_aproj.astype(BF16),
        w_main=w_main.astype(BF16), w_gz=w_gz.astype(BF16), gate_w_bd=gw.astype(BF16),
        gate_b=gla_gate_b[i].reshape(1, 2 * GLA_QK),
        q_norm_w=(jnp.tile(attn_q_norm_w[i], ATT_HEADS) * (ATT_HEAD_DIM ** -0.5 * LOG2_E)).reshape(1, ATT_Q),
        k_norm_w=jnp.tile(attn_k_norm_w[i], ATT_KV_HEADS).reshape(1, ATT_KV),
        bd_q=_block_diag_ones(ATT_Q, ATT_HEAD_DIM), bd_k=_block_diag_ones(ATT_KV, ATT_HEAD_DIM),
        w_bgate=w_branch_gate[i].astype(BF16), b_bgate=b_branch_gate[i].reshape(1, -1),
        w_bproj=w_branch_proj[i, 0:2].astype(BF16), w_out=w_out[i].astype(BF16),
        s5_d=s5_d[i].reshape(1, -1), glu_w=s5_glu_w[i].astype(BF16), glu_b=s5_glu_b[i].reshape(1, -1),
    )


def kernel(x, c, ctx, c_ctx, w_ada, b_ada, norm_w, w_ffn_gate, w_ffn_up, w_ffn_down, w_in, gla_gate_w,
           gla_gate_b, gla_norm_w, s5_a_re, s5_a_im, s5_log_dt, s5_b_re, s5_b_im, s5_c_re, s5_c_im, s5_d,
           s5_glu_w, s5_glu_b, attn_q_norm_w, attn_k_norm_w, w_branch_gate, b_branch_gate, w_branch_proj,
           w_out, final_norm_w):
    bsz, seq, d = x.shape
    depth = w_ada.shape[0]
    tables = _rope_tables(seq)
    ctx_tables = tuple(t[0:ctx.shape[1]] for t in tables)
    cond = jnp.zeros((SUBLANES, d), F32).at[0:bsz].set(c).at[bsz].set(c_ctx)
    tm_ffn, tm_in, tm_merge, tb_gla, nb_s5, rt_s5, tq, tk = (
        TILES[k] for k in ("ffn", "inproj", "merge", "gla", "s5_scan", "s5_rows", "att_q", "att_k"))

    ffn_w = (w_ffn_gate.astype(BF16), w_ffn_up.astype(BF16), w_ffn_down.astype(BF16))
    s5_ops = jax.vmap(_s5_operators)(s5_a_re, s5_a_im, s5_log_dt, s5_b_re, s5_b_im, s5_c_re, s5_c_im)
    h_lat, h_ctx = x, ctx
    for i in range(depth):
        last = i == depth - 1
        mods = _ada_call(cond, w_ada, b_ada, i).reshape(SUBLANES, N_MOD, d)
        lat, cxt = None, bsz
        lw = _layer_weights(i, w_in, gla_gate_w, gla_gate_b, attn_q_norm_w, attn_k_norm_w, w_branch_gate,
                            b_branch_gate, w_branch_proj, w_out, s5_d, s5_glu_w, s5_glu_b)

        def ffn(h, mod_row, j, fin=False):
            return _ffn_call(h, mods, norm_w, *ffn_w, final_norm_w, layer=i, sub=j, mod_row=mod_row,
                             final_norm=fin, tm=tm_ffn)

        h_lat = ffn(h_lat, lat, 0)
        h_ctx = ffn(h_ctx, cxt, 0)

        qk_c, v_c, r_c, lg_c, u_c, uf_c, aq_c, ak_c, av_c = _inproj_call(
            h_ctx, mods, norm_w, lw, ctx_tables, layer=i, mod_row=cxt, rope=False, tm=tm_in)
        qk_l, v_l, r_l, lg_l, u_l, uf_l, aq_l, ak_l, av_l = _inproj_call(
            h_lat, mods, norm_w, lw, tables, layer=i, mod_row=lat, rope=True, tm=tm_in)
        s0 = jnp.zeros((bsz, GLA_QK, GLA_DV), F32)
        yg_c, sf_c, sb_c = _gla_bidir(qk_c, v_c, lg_c, r_c, gla_norm_w[i], s0, s0, tb=tb_gla)
        yg_l, _, _ = _gla_bidir(qk_l, v_l, lg_l, r_l, gla_norm_w[i], sf_c, sb_c, tb=tb_gla)
        h0 = jnp.zeros((bsz, 2, S5_GROUPS, LANES), F32)
        ys_c, hfin_c = _s5_mix(uf_c, s5_ops, i, h0, bsz, nb=nb_s5, rt=rt_s5)
        ys_l, _ = _s5_mix(uf_l, s5_ops, i, hfin_c, bsz, nb=nb_s5, rt=rt_s5)
        kt, v_ext = _kv_layout(jnp.concatenate([ak_c, ak_l], axis=1), jnp.concatenate([av_c, av_l], axis=1))
        ya_l = _attn_call(aq_l, kt, v_ext, tq=tq, tk=tk, n_str=TILES["att_streams"])
        h_lat = _merge_call(h_lat, mods, norm_w, yg_l, ys_l, u_l, ya_l, lw, layer=i, mod_row=lat, tm=tm_merge)
        h_lat = ffn(h_lat, lat, 1, last)
        if not last:
            kt_c, v_ext_c = _kv_layout(ak_c, av_c)
            ya_c = _attn_call(aq_c, kt_c, v_ext_c, tq=tq, tk=tk, n_str=TILES["att_streams"])
            h_ctx = _merge_call(h_ctx, mods, norm_w, yg_c, ys_c, u_c, ya_c, lw, layer=i, mod_row=cxt, tm=tm_merge)
            h_ctx = ffn(h_ctx, cxt, 1)
    return h_lat
```

```python
import functools

import jax
import jax.numpy as jnp
import numpy as np
from jax import lax
from jax.experimental import pallas as pl
from jax.experimental.pallas import tpu as pltpu

F32 = jnp.float32
BF16 = jnp.bfloat16

N_MOD = 9
MIX_MOD = 3
MACARON_WEIGHT = 0.5
NORM_EPS = 1e-6
GRID_W = 64
GLA_HEADS = 4
GLA_DK = 64
GLA_DV = 128
GLA_QK = GLA_HEADS * GLA_DK
GLA_V = GLA_HEADS * GLA_DV
GLA_GATE_RANK = 16
GLA_TAU = 16.0
GLA_CHUNK = 64
S5_WIDTH = 512
S5_GROUP = 16
S5_GROUPS = S5_WIDTH // S5_GROUP
S5_STATE = 64
ATT_HEADS = 8
ATT_KV_HEADS = 2
ATT_GROUP = ATT_HEADS // ATT_KV_HEADS
ATT_HEAD_DIM = 64
ATT_Q = ATT_HEADS * ATT_HEAD_DIM
ATT_KV = ATT_KV_HEADS * ATT_HEAD_DIM
ROPE_PAIRS = ATT_HEAD_DIM // 4
ROPE_THETA = 10000.0
IN_WIDTHS = (GLA_QK, GLA_QK, GLA_V, GLA_GATE_RANK, GLA_GATE_RANK, GLA_V, S5_WIDTH, ATT_Q, ATT_KV, ATT_KV)

LANES = 128
SUBLANES = 8
MXU_DIM = 256
VMEM_LIMIT = 56 * 1024 * 1024

LOG2_E = 1.4426950408889634

S5_T = MXU_DIM // S5_GROUP
S5_CW = S5_T * S5_GROUP
S5_GPB = LANES // S5_GROUP
S5_SG = S5_GROUPS // S5_GPB
S5_UW = S5_T * LANES

TILES = dict(ffn=512, inproj=512, merge=512, gla=512, s5_scan=32, s5_rows=256, att_q=128, att_k=1280,
             att_streams=4)


def _params(*sem):
    return pltpu.CompilerParams(dimension_semantics=sem, vmem_limit_bytes=VMEM_LIMIT)


def _const_spec(shape):
    nd = len(shape)
    return pl.BlockSpec(shape, lambda *_: (0,) * nd, pipeline_mode=pl.Buffered(1))


def _mod_spec(d, mod_row):
    if mod_row is None:
        return pl.BlockSpec((1, N_MOD, d), lambda bi, i: (bi, 0, 0))
    return pl.BlockSpec((1, N_MOD, d), lambda bi, i: (mod_row, 0, 0))


def _norm_spec(d, layer):
    return pl.BlockSpec((None, 3, d), lambda bi, i: (layer, 0, 0), pipeline_mode=pl.Buffered(1))


def _dotg(a, b, contract):
    return lax.dot_general(a, b, (((contract[0],), (contract[1],)), ((), ())),
                           precision=lax.Precision.DEFAULT, preferred_element_type=F32)


def _dot(a, b):
    return _dotg(a, b, (1, 0))


def _dot_hi_lo(x, w):
    hi = x.astype(BF16)
    lo = (x - hi.astype(F32)).astype(BF16)
    return _dot(hi, w) + _dot(lo, w)


def _rms_mod(h, g_norm, shift, scale):
    ms = jnp.mean(h * h, axis=-1, keepdims=True)
    y = h * lax.rsqrt(ms + NORM_EPS) * g_norm
    return y * (1.0 + scale) + shift


def _log_sigmoid(x):
    return jnp.minimum(x, 0.0) - jnp.log(1.0 + jnp.exp(-jnp.abs(x)))


def _ada_kernel(c_ref, w_ref, b_ref, o_ref):
    c = c_ref[...]
    sc = (c * jax.nn.sigmoid(c)).astype(BF16)
    o_ref[...] = _dot(sc, w_ref[...].astype(BF16)) + b_ref[...]


def _ada_call(cond, w_all, b_all, layer):
    rows, d = cond.shape
    depth, _, n = w_all.shape
    tn = 9 * LANES
    return pl.pallas_call(
        _ada_kernel,
        grid=(n // tn,),
        in_specs=[pl.BlockSpec((rows, d), lambda j: (0, 0)),
                  pl.BlockSpec((None, d, tn), lambda j: (layer, 0, j)),
                  pl.BlockSpec((None, 1, tn), lambda j: (layer, 0, j))],
        out_specs=pl.BlockSpec((rows, tn), lambda j: (0, j)),
        out_shape=jax.ShapeDtypeStruct((rows, n), F32),
        compiler_params=_params("arbitrary"),
        name="adaln",
    )(cond, w_all, b_all.reshape(depth, 1, n))


def _ffn_kernel(h_ref, mod_ref, gn_ref, wg_ref, wu_ref, wd_ref, fn_ref, o_ref, *, mod_base, final_norm):
    h = h_ref[0]
    shift, scale, gate = (mod_ref[0, mod_base + k:mod_base + k + 1] for k in range(3))
    norm_row = mod_base // 3
    n = _rms_mod(h, gn_ref[norm_row:norm_row + 1], shift, scale).astype(BF16)
    g = _dot(n, wg_ref[...])
    u = _dot(n, wu_ref[...])
    a = (g * jax.nn.sigmoid(g) * u).astype(BF16)
    y = h + (MACARON_WEIGHT * gate) * _dot(a, wd_ref[...])
    if final_norm:
        ms = jnp.mean(y * y, axis=-1, keepdims=True)
        y = y * lax.rsqrt(ms + NORM_EPS) * fn_ref[...]
    o_ref[0] = y


def _ffn_call(h, mods, norm_w, wg, wu, wd, final_w, *, layer, sub, mod_row, final_norm, tm):
    b, l, d = h.shape
    f = wg.shape[-1]
    tm = min(tm, l)
    wspec = lambda r, c: pl.BlockSpec((None, None, r, c), lambda bi, i: (layer, sub, 0, 0),
                                      pipeline_mode=pl.Buffered(1))
    return pl.pallas_call(
        functools.partial(_ffn_kernel, mod_base=6 * sub, final_norm=final_norm),
        grid=(b, l // tm),
        in_specs=[pl.BlockSpec((1, tm, d), lambda bi, i: (bi, i, 0)),
                  _mod_spec(d, mod_row), _norm_spec(d, layer),
                  wspec(d, f), wspec(d, f), wspec(f, d), _const_spec((1, d))],
        out_specs=pl.BlockSpec((1, tm, d), lambda bi, i: (bi, i, 0)),
        out_shape=jax.ShapeDtypeStruct((b, l, d), F32),
        compiler_params=_params("parallel", "parallel"),
        name="ffn",
    )(h, mods, norm_w, wg, wu, wd, final_w.reshape(1, d))


def _inproj_kernel(h_ref, mod_ref, gn_ref, wm_ref, wgz_ref, gw_ref, gb_ref, qw_ref, kw_ref,
                   bdq_ref, bdk_ref, cos_ref, sa_ref, sb_ref,
                   qk_ref, v_ref, r_ref, lg_ref, u_ref, uf_ref, aq_ref, ak_ref, av_ref, us_scr, *, rope):
    h = h_ref[0]
    tm = h.shape[0]
    n = _rms_mod(h, gn_ref[1:2], mod_ref[0, MIX_MOD:MIX_MOD + 1], mod_ref[0, MIX_MOD + 1:MIX_MOD + 2]).astype(BF16)
    z = _dot(n, wm_ref[...])
    o = 0
    q = z[:, o:o + GLA_QK] * (GLA_DK ** -0.5); o += GLA_QK
    k = z[:, o:o + GLA_QK]; o += GLA_QK
    qk_ref[0, :, 0:GLA_QK] = q
    qk_ref[0, :, GLA_QK:2 * GLA_QK] = k
    v_ref[0] = z[:, o:o + GLA_V]; o += GLA_V
    r_ref[0] = z[:, o:o + GLA_V]; o += GLA_V
    u_ref[0] = z[:, o:o + S5_WIDTH]; o += S5_WIDTH
    for sg in range(S5_SG):
        us_scr[sg] = z[:, o - S5_WIDTH + sg * LANES:o - S5_WIDTH + (sg + 1) * LANES]
        for t in range(S5_T):
            piece = us_scr[sg, pl.ds(t, tm // S5_T, stride=S5_T), :]
            uf_ref[sg, :, t * LANES:(t + 1) * LANES] = piece.astype(BF16)
    xq = z[:, o:o + ATT_Q]; o += ATT_Q
    xk = z[:, o:o + ATT_KV]; o += ATT_KV
    av_ref[0] = z[:, o:o + ATT_KV].astype(BF16)

    zg = _dot(n, wgz_ref[...]).astype(BF16)
    lg_ref[0] = _log_sigmoid(_dot(zg, gw_ref[...]) + gb_ref[...]) * (1.0 / GLA_TAU)

    def head_norm(x, bd_ref, w_ref):
        ss = _dot_hi_lo(x * x, bd_ref[...])
        return x * lax.rsqrt(ss * (1.0 / ATT_HEAD_DIM) + NORM_EPS) * w_ref[...]

    def rotary(x, width):
        if not rope:
            return x
        c, sa, sb = cos_ref[:, 0:width], sa_ref[:, 0:width], sb_ref[:, 0:width]
        return (x * c + pltpu.roll(x, width - ROPE_PAIRS, 1) * sa + pltpu.roll(x, ROPE_PAIRS, 1) * sb)

    ak_ref[0] = rotary(head_norm(xk, bdk_ref, kw_ref), ATT_KV).astype(BF16)
    aq = rotary(head_norm(xq, bdq_ref, qw_ref), ATT_Q)
    lane = lax.broadcasted_iota(jnp.int32, (tm, LANES), 1)
    for kv in range(ATT_KV_HEADS):
        keep = (lane >= kv * ATT_HEAD_DIM) & (lane < (kv + 1) * ATT_HEAD_DIM)
        for j in range(ATT_GROUP):
            hd = kv * ATT_GROUP + j
            blk = aq[:, (hd // 2) * LANES:(hd // 2 + 1) * LANES]
            if hd % 2 != kv:
                blk = pltpu.roll(blk, ATT_HEAD_DIM, 1)
            aq_ref[0, kv, j] = jnp.where(keep, blk, 0.0).astype(BF16)


def _inproj_call(h, mods, norm_w, lw, tables, *, layer, mod_row, rope, tm):
    b, l, d = h.shape
    tm = min(tm, l)
    tok = lambda w: pl.BlockSpec((1, tm, w), lambda bi, i: (bi, i, 0))
    tab = pl.BlockSpec((tm, ATT_Q), lambda bi, i: (i, 0))
    wm = lw["w_main"]
    nt = l // tm
    tok_out = lambda w, dt: (tok(w), jax.ShapeDtypeStruct((b, l, w), dt))
    outs = [tok_out(2 * GLA_QK, F32), tok_out(GLA_V, F32), tok_out(GLA_V, F32), tok_out(2 * GLA_QK, F32),
            tok_out(S5_WIDTH, F32),
            (pl.BlockSpec((S5_SG, tm // S5_T, S5_UW), lambda bi, i: (0, bi * nt + i, 0)),
             jax.ShapeDtypeStruct((S5_SG, b * l // S5_T, S5_UW), BF16)),
            (pl.BlockSpec((1, ATT_KV_HEADS, ATT_GROUP, tm, LANES), lambda bi, i: (bi, 0, 0, i, 0)),
             jax.ShapeDtypeStruct((b, ATT_KV_HEADS, ATT_GROUP, l, LANES), BF16)),
            tok_out(ATT_KV, BF16), tok_out(ATT_KV, BF16)]
    return pl.pallas_call(
        functools.partial(_inproj_kernel, rope=rope),
        grid=(b, nt),
        in_specs=[tok(d), _mod_spec(d, mod_row), _norm_spec(d, layer),
                  _const_spec(wm.shape), _const_spec((d, LANES)), _const_spec((LANES, 2 * GLA_QK)),
                  _const_spec((1, 2 * GLA_QK)), _const_spec((1, ATT_Q)), _const_spec((1, ATT_KV)),
                  _const_spec((ATT_Q, ATT_Q)), _const_spec((ATT_KV, ATT_KV)), tab, tab, tab],
        out_specs=[s for s, _ in outs],
        out_shape=[t for _, t in outs],
        scratch_shapes=[pltpu.VMEM((S5_SG, tm, LANES), F32)],
        compiler_params=_params("parallel", "parallel"),
        name="inproj",
    )(h, mods, norm_w, wm, lw["w_gz"], lw["gate_w_bd"], lw["gate_b"], lw["q_norm_w"],
      lw["k_norm_w"], lw["bd_q"], lw["bd_k"], *tables)


def _gla_kernel(qk_ref, v_ref, lg_ref, s0_ref, tri_ref, of_ref, r_ref, nw_ref, o_ref, sfin_ref, s_scr,
                *, reverse, readout, n_chunks, n_batch):
    i = pl.program_id(0)

    @pl.when(i == 0)
    def _():
        s_scr[...] = s0_ref[...]

    tri = tri_ref[...]
    c, nh = GLA_CHUNK, GLA_HEADS
    row = lax.broadcasted_iota(jnp.int32, (nh * c, c), 0) % c
    col = lax.broadcasted_iota(jnp.int32, (nh * c, c), 1)
    keep = (col >= row) if reverse else (col <= row)
    head_of_lane = lax.broadcasted_iota(jnp.int32, (c, GLA_QK), 1) // GLA_DK

    def chunk(bi, ci):
        rows = slice(ci * c, (ci + 1) * c)
        g = lg_ref[bi, rows, :]
        g_hi = g.astype(BF16)
        b = _dot(tri, g_hi) + _dot(tri, (g - g_hi.astype(F32)).astype(BF16))
        b_last = b[0:1, :] if reverse else b[c - 1:c, :]
        q_in = qk_ref[bi, rows, 0:GLA_QK] * jnp.exp(b)
        k = qk_ref[bi, rows, GLA_QK:2 * GLA_QK]
        k_in = (k * jnp.exp(-b)).astype(BF16)
        k_out = (k * jnp.exp(b_last - b)).astype(BF16)
        dec = jnp.exp(jnp.broadcast_to(b_last, (LANES, GLA_QK)).T)
        v = v_ref[bi, rows, :].astype(BF16)
        s_old = s_scr[bi]
        q_st = jnp.concatenate([jnp.where(head_of_lane == hd, q_in, 0.0) for hd in range(nh)],
                               axis=0).astype(BF16)
        att = _dotg(q_st, k_in, (1, 1))
        att = jnp.where(keep, att, 0.0).astype(BF16)
        o_inter = _dot(q_st, s_old.astype(BF16))
        o_intra, ds = [], []
        for pr in range(nh // 2):
            vp = v[:, 2 * pr * GLA_DV:(2 * pr + 2) * GLA_DV]
            o_intra.append(_dot(att[2 * pr * c:(2 * pr + 2) * c], vp))
            ds.append(_dotg(k_out[:, pr * LANES:(pr + 1) * LANES], vp, (0, 0)))
        for hd in range(nh):
            tok = slice(hd * c, (hd + 1) * c)
            vcols = slice(hd * GLA_DV, (hd + 1) * GLA_DV)
            e = hd % 2
            o_h = o_intra[hd // 2][e * c:(e + 1) * c, e * GLA_DV:(e + 1) * GLA_DV] + o_inter[tok]
            if readout:
                o_h = o_h + of_ref[bi, rows, vcols]
                ms = jnp.mean(o_h * o_h, axis=-1, keepdims=True)
                y = o_h * lax.rsqrt(ms + NORM_EPS) * nw_ref[...]
                rr = r_ref[bi, rows, vcols]
                o_h = y * (rr * jax.nn.sigmoid(rr))
            o_ref[bi, rows, vcols] = o_h
        d_state = jnp.concatenate([ds[hd // 2][(hd % 2) * GLA_DK:(hd % 2 + 1) * GLA_DK,
                                               (hd % 2) * GLA_DV:(hd % 2 + 1) * GLA_DV]
                                   for hd in range(nh)], axis=0)
        s_scr[bi] = dec * s_old + d_state

    for ci in (range(n_chunks - 1, -1, -1) if reverse else range(n_chunks)):
        for bi in range(n_batch):
            chunk(bi, ci)

    @pl.when(i == pl.num_programs(0) - 1)
    def _():
        sfin_ref[...] = s_scr[...]


def _gla_call(qk, v, lg, s0, o_f, r, norm_w, *, reverse, readout, tb):
    b, l, _ = v.shape
    tb = min(tb, l)
    nb = l // tb
    idx = (lambda i: (0, nb - 1 - i, 0)) if reverse else (lambda i: (0, i, 0))
    tok = lambda w: pl.BlockSpec((b, tb, w), idx)
    lg_off = 1 if reverse else 0
    lg_spec = pl.BlockSpec((b, tb, GLA_QK), (lambda i: (0, nb - 1 - i, lg_off)) if reverse
                           else (lambda i: (0, i, lg_off)))
    state = pl.BlockSpec((b, GLA_QK, GLA_DV), lambda i: (0, 0, 0))
    ones = np.triu(np.ones((GLA_CHUNK, GLA_CHUNK), np.float32)) if reverse else \
        np.tril(np.ones((GLA_CHUNK, GLA_CHUNK), np.float32))
    return pl.pallas_call(
        functools.partial(_gla_kernel, reverse=reverse, readout=readout, n_chunks=tb // GLA_CHUNK, n_batch=b),
        grid=(nb,),
        in_specs=[tok(2 * GLA_QK), tok(GLA_V), lg_spec, state, _const_spec((GLA_CHUNK, GLA_CHUNK)),
                  tok(GLA_V), tok(GLA_V), _const_spec((1, GLA_DV))],
        out_specs=[tok(GLA_V), state],
        out_shape=[jax.ShapeDtypeStruct((b, l, GLA_V), F32), jax.ShapeDtypeStruct((b, GLA_QK, GLA_DV), F32)],
        scratch_shapes=[pltpu.VMEM((b, GLA_QK, GLA_DV), F32)],
        compiler_params=_params("arbitrary"),
        name="gla_bwd" if reverse else "gla_fwd",
    )(qk, v, lg, s0, jnp.asarray(ones, BF16), o_f, r, norm_w.reshape(1, GLA_DV))


def _gla_bidir(qk, v, lg, r, norm_w, s0_f, s0_b, *, tb):
    o_f, s_f = _gla_call(qk, v, lg, s0_f, v, r, norm_w, reverse=False, readout=False, tb=tb)
    y, s_b = _gla_call(qk, v, lg, s0_b, o_f, r, norm_w, reverse=True, readout=True, tb=tb)
    return y, s_f, s_b


def _s5_operators(a_re, a_im, log_dt, b_re, b_im, c_re, c_im):
    hp = lax.Precision.HIGHEST
    t = S5_T
    cr, ci_ = c_re.astype(F32), c_im.astype(F32)
    ks = jnp.arange(t + 1, dtype=F32)[:, None, None]
    per_dir = []
    for d in range(2):
        dt = jnp.exp(log_dt[d].astype(F32))[:, None]
        ar, ai = a_re[d].astype(F32), a_im[d].astype(F32)
        mag = jnp.exp(ks * dt * ar)
        pw_re, pw_im = mag * jnp.cos(ks * dt * ai), mag * jnp.sin(ks * dt * ai)
        den = ar * ar + ai * ai
        xr, xi = pw_re[1] - 1.0, pw_im[1]
        coef_re = (xr * ar + xi * ai) / den
        coef_im = (xi * ar - xr * ai) / den
        br, bi = b_re.astype(F32), b_im.astype(F32)
        bb_re = coef_re[..., None] * br - coef_im[..., None] * bi
        bb_im = coef_re[..., None] * bi + coef_im[..., None] * br
        ca_re = cr[None] * pw_re[:, :, None, :] - ci_[None] * pw_im[:, :, None, :]
        ca_im = cr[None] * pw_im[:, :, None, :] + ci_[None] * pw_re[:, :, None, :]
        kk = (jnp.einsum("kgop,gpi->kgoi", ca_re, bb_re, precision=hp)
              - jnp.einsum("kgop,gpi->kgoi", ca_im, bb_im, precision=hp))
        s_idx = jnp.arange(t)[:, None]
        t_idx = jnp.arange(t)[None, :]
        lag = (t_idx - s_idx) if d == 0 else (s_idx - t_idx)
        toe = jnp.where((lag >= 0)[:, :, None, None, None], kk[jnp.clip(lag, 0, t)], 0.0)
        m = jnp.transpose(toe, (2, 0, 4, 1, 3)).reshape(S5_GROUPS, S5_CW, S5_CW)
        p_pow = (t - 1 - jnp.arange(t)) if d == 0 else jnp.arange(t)
        pr, pi = pw_re[p_pow], pw_im[p_pow]
        p_re = pr[:, :, :, None] * bb_re[None] - pi[:, :, :, None] * bb_im[None]
        p_im = pr[:, :, :, None] * bb_im[None] + pi[:, :, :, None] * bb_re[None]
        flat_p = lambda x: jnp.transpose(x, (1, 0, 3, 2)).reshape(S5_GROUPS, S5_CW, S5_STATE)
        q_pow = (jnp.arange(t) + 1) if d == 0 else (t - jnp.arange(t))
        q_re, q_im = ca_re[q_pow], -ca_im[q_pow]
        flat_q = lambda x: jnp.transpose(x, (1, 3, 0, 2)).reshape(S5_GROUPS, S5_STATE, S5_CW)
        per_dir.append(dict(m=m, p_re=flat_p(p_re), p_im=flat_p(p_im), q_re=flat_q(q_re), q_im=flat_q(q_im),
                            at_re=pw_re[t], at_im=pw_im[t]))
    f, bk = per_dir
    zq = jnp.zeros_like(f["q_re"])
    p_cat = jnp.concatenate([f["p_re"], bk["p_re"], f["p_im"], bk["p_im"]], axis=-1)
    q_f = jnp.concatenate([f["q_re"], zq, f["q_im"], zq], axis=1)
    q_b = jnp.concatenate([zq, bk["q_re"], zq, bk["q_im"]], axis=1)
    at_re = jnp.concatenate([f["at_re"], bk["at_re"]], axis=-1)
    at_im = jnp.concatenate([f["at_im"], bk["at_im"]], axis=-1)

    sg, gpb, t, gc = S5_SG, S5_GPB, S5_T, S5_GROUP
    idx = jnp.arange(S5_UW)
    gl_tgc = (idx // gc) % gpb
    gl_gk = idx // S5_CW
    src = np.arange(S5_UW)
    spread = np.zeros((S5_CW, S5_UW), np.float32)
    spread[(src // LANES) * gc + src % gc, src] = 1.0
    spread = jnp.asarray(spread, BF16)

    def rows_tgc(a):
        k = a.shape[-1]
        return jnp.transpose(a.astype(BF16).reshape(sg, gpb, t, gc, k), (0, 2, 1, 3, 4)).reshape(sg, S5_UW, k)

    def cols_tgc(a):
        return lax.dot_general(a, spread, (((2,), (0,)), ((), ())), preferred_element_type=BF16)

    def keep(rows_gl, cols_gl, a):
        return jnp.where((rows_gl[:, None] == cols_gl[None, :])[None], a, jnp.zeros_like(a))

    spread_k = np.zeros((S5_CW, S5_UW), np.float32)
    spread_k[src % S5_CW, src] = 1.0
    p_wide = lax.dot_general(rows_tgc(p_cat), jnp.asarray(spread_k, BF16), (((2,), (0,)), ((), ())),
                             preferred_element_type=BF16)
    p_blk = keep(gl_tgc, gl_gk, p_wide)
    m_blk = keep(gl_tgc, gl_tgc, cols_tgc(rows_tgc(f["m"] + bk["m"])))
    q_blk = keep(gl_gk, gl_tgc, cols_tgc((q_f + q_b).astype(BF16).reshape(sg, S5_UW, S5_CW)))
    return p_blk, m_blk, q_blk, at_re, at_im


def _s5_x_kernel(u_ref, p_ref, x_ref):
    x_ref[...] = _dot(u_ref[0], p_ref[0])


def _s5_x_call(uf, p_blk, layer, *, rt):
    sg, r, uw = uf.shape
    rt = min(rt, r)
    return pl.pallas_call(
        _s5_x_kernel,
        grid=(sg, r // rt),
        in_specs=[pl.BlockSpec((1, rt, uw), lambda si, i: (si, i, 0)),
                  pl.BlockSpec((None, 1, uw, uw), lambda si, i: (layer, si, 0, 0), pipeline_mode=pl.Buffered(1))],
        out_specs=pl.BlockSpec((rt, uw), lambda si, i: (i, si)),
        out_shape=jax.ShapeDtypeStruct((r, sg * uw), F32),
        compiler_params=_params("parallel", "parallel"),
        name="s5_x",
    )(uf, p_blk)


def _s5_scan_kernel(xf_ref, xb_ref, are_ref, aim_ref, h0_ref, hf_ref, hb_ref, hfin_ref, hr_scr, hi_scr, *, nb):
    i = pl.program_id(0)

    @pl.when(i == 0)
    def _():
        hr_scr[...] = h0_ref[:, 0]
        hi_scr[...] = h0_ref[:, 1]

    ar, ai = are_ref[...][None], aim_ref[...][None]
    fwd = lax.broadcasted_iota(jnp.int32, (1, S5_GROUPS, LANES), 2) < S5_STATE

    def step(j, carry):
        hr, hi = carry
        jb = nb - 1 - j
        xr = jnp.where(fwd, xf_ref[:, j, :, 0:LANES], xb_ref[:, jb, :, 0:LANES])
        xi = jnp.where(fwd, xf_ref[:, j, :, LANES:2 * LANES], xb_ref[:, jb, :, LANES:2 * LANES])
        hf_ref[:, j, :, 0:LANES] = hr
        hf_ref[:, j, :, LANES:2 * LANES] = hi
        hb_ref[:, jb, :, 0:LANES] = hr
        hb_ref[:, jb, :, LANES:2 * LANES] = hi
        return ar * hr - ai * hi + xr, ar * hi + ai * hr + xi

    hr, hi = lax.fori_loop(0, nb, step, (hr_scr[...], hi_scr[...]))
    hr_scr[...] = hr
    hi_scr[...] = hi

    @pl.when(i == pl.num_programs(0) - 1)
    def _():
        hfin_ref[:, 0] = hr
        hfin_ref[:, 1] = hi


def _s5_scan_call(x4, at_re, at_im, h0, *, nb):
    b, n, g, w = x4.shape
    nb = min(nb, n)
    steps = n // nb
    blk = lambda rev: pl.BlockSpec((b, nb, g, w), (lambda i: (0, steps - 1 - i, 0, 0)) if rev
                                   else (lambda i: (0, i, 0, 0)))
    st = pl.BlockSpec((b, 2, g, LANES), lambda i: (0, 0, 0, 0))
    return pl.pallas_call(
        functools.partial(_s5_scan_kernel, nb=nb),
        grid=(steps,),
        in_specs=[blk(False), blk(True), _const_spec((g, LANES)), _const_spec((g, LANES)), st],
        out_specs=[blk(False), blk(True), st],
        out_shape=[jax.ShapeDtypeStruct(x4.shape, F32), jax.ShapeDtypeStruct(x4.shape, F32),
                   jax.ShapeDtypeStruct((b, 2, g, LANES), F32)],
        scratch_shapes=[pltpu.VMEM((b, g, LANES), F32), pltpu.VMEM((b, g, LANES), F32)],
        compiler_params=_params("arbitrary"),
        name="s5_scan",
    )(x4, x4, at_re, at_im, h0)


def _s5_y_kernel(u_ref, hf_ref, hb_ref, m_ref, q_ref, y_ref):
    rt = hf_ref.shape[0]
    lane = lax.broadcasted_iota(jnp.int32, (rt, S5_UW), 1)
    h = jnp.where(lane % LANES < S5_STATE, hf_ref[...], hb_ref[...]).astype(BF16)
    y = _dot(u_ref[0], m_ref[0]) + _dot(h, q_ref[0])
    for t in range(S5_T):
        y_ref[pl.ds(t, rt, stride=S5_T), :] = y[:, t * LANES:(t + 1) * LANES]


def _s5_y_call(uf, hf2, hb2, m_blk, q_blk, layer, *, rt):
    sg, r, uw = uf.shape
    rt = min(rt, r)
    wspec = pl.BlockSpec((None, 1, uw, uw), lambda si, i: (layer, si, 0, 0), pipeline_mode=pl.Buffered(1))
    hspec = pl.BlockSpec((rt, uw), lambda si, i: (i, si))
    return pl.pallas_call(
        _s5_y_kernel,
        grid=(sg, r // rt),
        in_specs=[pl.BlockSpec((1, rt, uw), lambda si, i: (si, i, 0)), hspec, hspec, wspec, wspec],
        out_specs=pl.BlockSpec((rt * S5_T, LANES), lambda si, i: (i, si)),
        out_shape=jax.ShapeDtypeStruct((r * S5_T, sg * LANES), F32),
        compiler_params=_params("parallel", "parallel"),
        name="s5_y",
    )(uf, hf2, hb2, m_blk, q_blk)


def _s5_mix(uf, ops, layer, h0, bsz, *, nb, rt):
    p_blk, m_blk, q_blk, at_re, at_im = ops
    r = uf.shape[1]
    n = r // bsz
    x = _s5_x_call(uf, p_blk, layer, rt=rt)
    hf, hb, hfin = _s5_scan_call(x.reshape(bsz, n, S5_GROUPS, S5_CW), at_re[layer], at_im[layer], h0, nb=nb)
    y = _s5_y_call(uf, hf.reshape(r, S5_GROUPS * S5_CW), hb.reshape(r, S5_GROUPS * S5_CW),
                   m_blk, q_blk, layer, rt=rt)
    return y.reshape(bsz, n * S5_T, S5_WIDTH), hfin


def _attn_kernel(q_ref, k_ref, vt_ref, o_ref, m_scr, acc_scr, s_scr, mx_scr, *, tk, n_kv, n_str):
    grp, tq = q_ref.shape[2], q_ref.shape[3] // n_str
    cols = grp * tq
    q = [q_ref[0, 0, :, st * tq:(st + 1) * tq, :].reshape(cols, LANES) for st in range(n_str)]
    m_scr[...] = jnp.full(m_scr.shape, -jnp.inf, F32)
    acc_scr[...] = jnp.zeros(acc_scr.shape, F32)

    def scores(j, st, slot):
        off = pl.multiple_of(j * tk, tk)
        s = _dotg(k_ref[0, pl.ds(off, tk), :], q[st], (1, 1))
        s_scr[st, slot] = s
        mx_scr[st, slot] = jnp.broadcast_to(jnp.max(s, axis=0, keepdims=True), (SUBLANES, cols))

    def softmax_pv(j, st, slot):
        off = pl.multiple_of(j * tk, tk)
        m_prev = m_scr[st]
        m_new = jnp.maximum(m_prev, mx_scr[st, slot])
        alpha = jnp.exp2(m_prev - m_new)
        p = jnp.exp2(s_scr[st, slot] - m_new[0:1]).astype(BF16)
        acc_scr[st] = acc_scr[st] * alpha[0:1] + _dot(vt_ref[0, 0, :, pl.ds(off, tk)], p)
        m_scr[st] = m_new

    def step(j_next, j_cur, slot_next, slot_cur):
        for st in range(n_str):
            if j_next is not None:
                scores(j_next, st, slot_next)
            if j_cur is not None:
                softmax_pv(j_cur, st, slot_cur)

    step(0, None, 0, None)

    def pair(t, carry):
        j = 2 * t
        step(j + 1, j, 1, 0)
        step(j + 2, j + 1, 0, 1)
        return carry

    lax.fori_loop(0, (n_kv - 1) // 2, pair, 0)
    if n_kv % 2 == 0:
        step(n_kv - 1, n_kv - 2, 1, 0)
        step(None, n_kv - 1, None, 1)
    else:
        step(None, n_kv - 1, None, 0)
    for st in range(n_str):
        acc = acc_scr[st]
        out_t = acc / acc[ATT_HEAD_DIM:ATT_HEAD_DIM + 1]
        for j in range(grp):
            o_ref[0, 0, j, st * tq:(st + 1) * tq, :] = out_t[:, j * tq:(j + 1) * tq].T.astype(o_ref.dtype)


def _attn_call(qs, k, vt_ext, *, tq, tk, n_str):
    b, kvh, grp, l, _ = qs.shape
    lk = k.shape[1]
    tq, tk = min(tq, l // n_str), min(tk, lk)
    cols = grp * tq
    blk = pl.BlockSpec((1, 1, grp, n_str * tq, LANES), lambda bi, ki, i: (bi, ki, 0, i, 0))
    return pl.pallas_call(
        functools.partial(_attn_kernel, tk=tk, n_kv=lk // tk, n_str=n_str),
        grid=(b, kvh, l // (n_str * tq)),
        in_specs=[blk,
                  pl.BlockSpec((1, lk, LANES), lambda bi, ki, i: (bi, 0, 0), pipeline_mode=pl.Buffered(1)),
                  pl.BlockSpec((1, 1, LANES, lk), lambda bi, ki, i: (bi, ki, 0, 0), pipeline_mode=pl.Buffered(1))],
        out_specs=blk,
        out_shape=jax.ShapeDtypeStruct(qs.shape, BF16),
        scratch_shapes=[pltpu.VMEM((n_str, SUBLANES, cols), F32), pltpu.VMEM((n_str, LANES, cols), F32),
                        pltpu.VMEM((n_str, 2, tk, cols), F32), pltpu.VMEM((n_str, 2, SUBLANES, cols), F32)],
        compiler_params=_params("parallel", "parallel", "arbitrary"),
        name="attention",
    )(qs, k, vt_ext)


def _kv_layout(ak, av):
    b, lk, _ = av.shape
    vt = jnp.transpose(av.reshape(b, lk, ATT_KV_HEADS, ATT_HEAD_DIM), (0, 2, 3, 1))
    return ak, jnp.concatenate([vt, jnp.ones_like(vt)], axis=2)


def _merge_kernel(h_ref, mod_ref, gn_ref, yg_ref, ys_ref, u_ref, ya_ref, wbg_ref, bbg_ref, wp_ref, wpa_ref,
                  wo_ref, d_ref, gw_ref, gb_ref, o_ref):
    h = h_ref[0]
    d = h.shape[-1]
    n = _rms_mod(h, gn_ref[1:2], mod_ref[0, MIX_MOD:MIX_MOD + 1], mod_ref[0, MIX_MOD + 1:MIX_MOD + 2]).astype(BF16)
    g = jax.nn.sigmoid(_dot(n, wbg_ref[...]) + bbg_ref[...])
    y = jax.nn.gelu(ys_ref[0] + d_ref[...] * u_ref[0])
    y_s5 = y * jax.nn.sigmoid(_dot(y.astype(BF16), gw_ref[...]) + gb_ref[...])
    ya = jnp.concatenate([ya_ref[0, kv, j] for kv in range(ATT_KV_HEADS) for j in range(ATT_GROUP)], axis=-1)
    m = (g[:, 0:d] * _dot(yg_ref[0].astype(BF16), wp_ref[0])
         + g[:, d:2 * d] * _dot(y_s5.astype(BF16), wp_ref[1])
         + g[:, 2 * d:3 * d] * _dot(ya, wpa_ref[...]))
    o_ref[0] = h + mod_ref[0, MIX_MOD + 2:MIX_MOD + 3] * _dot(m.astype(BF16), wo_ref[...])


def _merge_call(h, mods, norm_w, y_gla, y_s5raw, u, y_att, lw, *, layer, mod_row, tm):
    b, l, d = h.shape
    tm = min(tm, l)
    tok = lambda w: pl.BlockSpec((1, tm, w), lambda bi, i: (bi, i, 0))
    bw = y_gla.shape[-1]
    att = pl.BlockSpec((1, ATT_KV_HEADS, ATT_GROUP, tm, LANES), lambda bi, i: (bi, 0, 0, i, 0))
    return pl.pallas_call(
        _merge_kernel,
        grid=(b, l // tm),
        in_specs=[tok(d), _mod_spec(d, mod_row), _norm_spec(d, layer),
                  tok(bw), tok(bw), tok(bw), att,
                  _const_spec((d, 3 * d)), _const_spec((1, 3 * d)), _const_spec((2, bw, d)),
                  _const_spec((ATT_HEADS * LANES, d)), _const_spec((d, d)),
                  _const_spec((1, bw)), _const_spec((bw, bw)), _const_spec((1, bw))],
        out_specs=tok(d),
        out_shape=jax.ShapeDtypeStruct((b, l, d), F32),
        compiler_params=_params("parallel", "parallel"),
        name="merge",
    )(h, mods, norm_w, y_gla, y_s5raw, u, y_att, lw["w_bgate"], lw["b_bgate"], lw["w_bproj"],
      lw["w_aproj"], lw["w_out"], lw["s5_d"], lw["glu_w"], lw["glu_b"])


def _rope_tables(n_tokens):
    rows = n_tokens // GRID_W
    row = jnp.repeat(jnp.arange(rows, dtype=F32), GRID_W)
    col = jnp.tile(jnp.arange(GRID_W, dtype=F32), rows)
    inv = ROPE_THETA ** (-jnp.arange(ROPE_PAIRS, dtype=F32) / ROPE_PAIRS)
    ang_r, ang_c = row[:, None] * inv, col[:, None] * inv
    zero = jnp.zeros_like(ang_r)
    cos = jnp.concatenate([jnp.cos(ang_r), jnp.cos(ang_r), jnp.cos(ang_c), jnp.cos(ang_c)], axis=-1)
    sin_a = jnp.concatenate([-jnp.sin(ang_r), zero, -jnp.sin(ang_c), zero], axis=-1)
    sin_b = jnp.concatenate([zero, jnp.sin(ang_r), zero, jnp.sin(ang_c)], axis=-1)
    tile = lambda t: jnp.tile(t, (1, ATT_HEADS))
    return tile(cos), tile(sin_a), tile(sin_b)


def _block_diag_ones(width, seg):
    idx = np.arange(width) // seg
    return jnp.asarray((idx[:, None] == idx[None, :]).astype(np.float32), BF16)


def _layer_weights(i, w_in, gla_gate_w, gla_gate_b, attn_q_norm_w, attn_k_norm_w, w_branch_gate,
                   b_branch_gate, w_branch_proj, w_out, s5_d, s5_glu_w, s5_glu_b):
    offs = np.concatenate([[0], np.cumsum(IN_WIDTHS)])
    col = lambda k: w_in[i][:, offs[k]:offs[k + 1]]
    w_main = jnp.concatenate([col(0), col(1), col(2), col(5), col(6), col(7), col(8), col(9)], axis=1)
    d = w_in.shape[1]
    rk = GLA_GATE_RANK
    w_gz = jnp.zeros((d, LANES), F32).at[:, 0:rk].set(col(3)).at[:, rk:2 * rk].set(col(4))
    gw = jnp.zeros((LANES, 2 * GLA_QK), F32)
    gw = gw.at[0:rk, 0:GLA_QK].set(gla_gate_w[i, 0]).at[rk:2 * rk, GLA_QK:].set(gla_gate_w[i, 1])
    wa = w_branch_proj[i, 2].reshape(ATT_HEADS, ATT_HEAD_DIM, d)
    w_aproj = jnp.concatenate([wa, jnp.zeros_like(wa)], axis=1).reshape(ATT_HEADS * LANES, d)
    return dict(
        w_aproj=w_aproj.astype(BF16),
        w_main=w_main.astype(BF16), w_gz=w_gz.astype(BF16), gate_w_bd=gw.astype(BF16),
        gate_b=gla_gate_b[i].reshape(1, 2 * GLA_QK),
        q_norm_w=(jnp.tile(attn_q_norm_w[i], ATT_HEADS) * (ATT_HEAD_DIM ** -0.5 * LOG2_E)).reshape(1, ATT_Q),
        k_norm_w=jnp.tile(attn_k_norm_w[i], ATT_KV_HEADS).reshape(1, ATT_KV),
        bd_q=_block_diag_ones(ATT_Q, ATT_HEAD_DIM), bd_k=_block_diag_ones(ATT_KV, ATT_HEAD_DIM),
        w_bgate=w_branch_gate[i].astype(BF16), b_bgate=b_branch_gate[i].reshape(1, -1),
        w_bproj=w_branch_proj[i, 0:2].astype(BF16), w_out=w_out[i].astype(BF16),
        s5_d=s5_d[i].reshape(1, -1), glu_w=s5_glu_w[i].astype(BF16), glu_b=s5_glu_b[i].reshape(1, -1),
    )


def kernel(x, c, ctx, c_ctx, w_ada, b_ada, norm_w, w_ffn_gate, w_ffn_up, w_ffn_down, w_in, gla_gate_w,
           gla_gate_b, gla_norm_w, s5_a_re, s5_a_im, s5_log_dt, s5_b_re, s5_b_im, s5_c_re, s5_c_im, s5_d,
           s5_glu_w, s5_glu_b, attn_q_norm_w, attn_k_norm_w, w_branch_gate, b_branch_gate, w_branch_proj,
           w_out, final_norm_w):
    bsz, seq, d = x.shape
    depth = w_ada.shape[0]
    tables = _rope_tables(seq)
    ctx_tables = tuple(t[0:ctx.shape[1]] for t in tables)
    cond = jnp.zeros((SUBLANES, d), F32).at[0:bsz].set(c).at[bsz].set(c_ctx)
    tm_ffn, tm_in, tm_merge, tb_gla, nb_s5, rt_s5, tq, tk = (
        TILES[k] for k in ("ffn", "inproj", "merge", "gla", "s5_scan", "s5_rows", "att_q", "att_k"))

    ffn_w = (w_ffn_gate.astype(BF16), w_ffn_up.astype(BF16), w_ffn_down.astype(BF16))
    s5_ops = jax.vmap(_s5_operators)(s5_a_re, s5_a_im, s5_log_dt, s5_b_re, s5_b_im, s5_c_re, s5_c_im)
    h_lat, h_ctx = x, ctx
    for i in range(depth):
        last = i == depth - 1
        mods = _ada_call(cond, w_ada, b_ada, i).reshape(SUBLANES, N_MOD, d)
        lat, cxt = None, bsz
        lw = _layer_weights(i, w_in, gla_gate_w, gla_gate_b, attn_q_norm_w, attn_k_norm_w, w_branch_gate,
                            b_branch_gate, w_branch_proj, w_out, s5_d, s5_glu_w, s5_glu_b)

        def ffn(h, mod_row, j, fin=False):
            return _ffn_call(h, mods, norm_w, *ffn_w, final_norm_w, layer=i, sub=j, mod_row=mod_row,
                             final_norm=fin, tm=tm_ffn)

        h_lat = ffn(h_lat, lat, 0)
        h_ctx = ffn(h_ctx, cxt, 0)

        qk_c, v_c, r_c, lg_c, u_c, uf_c, aq_c, ak_c, av_c = _inproj_call(
            h_ctx, mods, norm_w, lw, ctx_tables, layer=i, mod_row=cxt, rope=False, tm=tm_in)
        qk_l, v_l, r_l, lg_l, u_l, uf_l, aq_l, ak_l, av_l = _inproj_call(
            h_lat, mods, norm_w, lw, tables, layer=i, mod_row=lat, rope=True, tm=tm_in)
        s0 = jnp.zeros((bsz, GLA_QK, GLA_DV), F32)
        yg_c, sf_c, sb_c = _gla_bidir(qk_c, v_c, lg_c, r_c, gla_norm_w[i], s0, s0, tb=tb_gla)
        yg_l, _, _ = _gla_bidir(qk_l, v_l, lg_l, r_l, gla_norm_w[i], sf_c, sb_c, tb=tb_gla)
        h0 = jnp.zeros((bsz, 2, S5_GROUPS, LANES), F32)
        ys_c, hfin_c = _s5_mix(uf_c, s5_ops, i, h0, bsz, nb=nb_s5, rt=rt_s5)
        ys_l, _ = _s5_mix(uf_l, s5_ops, i, hfin_c, bsz, nb=nb_s5, rt=rt_s5)
        kt, v_ext = _kv_layout(jnp.concatenate([ak_c, ak_l], axis=1), jnp.concatenate([av_c, av_l], axis=1))
        ya_l = _attn_call(aq_l, kt, v_ext, tq=tq, tk=tk, n_str=TILES["att_streams"])
        h_lat = _merge_call(h_lat, mods, norm_w, yg_l, ys_l, u_l, ya_l, lw, layer=i, mod_row=lat, tm=tm_merge)
        h_lat = ffn(h_lat, lat, 1, last)
        if not last:
            kt_c, v_ext_c = _kv_layout(ak_c, av_c)
            ya_c = _attn_call(aq_c, kt_c, v_ext_c, tq=tq, tk=tk, n_str=TILES["att_streams"])
            h_ctx = _merge_call(h_ctx, mods, norm_w, yg_c, ys_c, u_c, ya_c, lw, layer=i, mod_row=cxt, tm=tm_merge)
            h_ctx = ffn(h_ctx, cxt, 1)
    return h_lat
```
